```python
import jax, jax.numpy as jnp
from jax import lax
import numpy as np

D_MODEL = 1024
BATCH = 8
SEQ = 2048
DEPTH = 2
DEC_BATCH = 32
DEC_SEQ = 8
PAST_LEN = 8192
PAGE_SIZE = 128

HEAD_DIM = 64
H_RET = 6
H_SB = 5
H_FOX = 5
D_RET = H_RET * HEAD_DIM
D_SB = H_SB * HEAD_DIM
D_FOX = H_FOX * HEAD_DIM
D_MIX = D_RET + D_SB + D_FOX
IN_WIDTHS = (D_RET, D_RET, D_RET, D_RET, D_SB, D_SB, D_SB, D_SB, D_FOX, D_FOX, D_FOX, D_FOX, H_FOX)
IN_COLS = 4 * D_MIX + H_FOX
PLE_DIM = 256
Q_BLOCK = 128
RET_CHUNK = 128
ROPE_BASE = 10000.0
EPS = 1e-6

kernel_name = "hymba_retention_stickbreak_fox_step"

F32 = jnp.float32


def rms_norm(x, g):
    xf = x.astype(F32)
    y = xf * lax.rsqrt(jnp.mean(xf * xf, axis=-1, keepdims=True) + EPS)
    return (y * g.astype(F32)).astype(x.dtype)


def group_norm_heads(x, g):
    xf = x.astype(F32)
    mu = jnp.mean(xf, axis=-1, keepdims=True)
    var = jnp.mean(jnp.square(xf - mu), axis=-1, keepdims=True)
    y = (xf - mu) * lax.rsqrt(var + EPS)
    return (y * g.astype(F32).reshape(x.shape[2], x.shape[3])).astype(x.dtype)


def heads(a, h):
    return a.reshape(a.shape[0], a.shape[1], h, HEAD_DIM)


def rotary(x, pos):
    half = HEAD_DIM // 2
    inv = ROPE_BASE ** (-jnp.arange(half, dtype=F32) / half)
    ang = pos.astype(F32)[:, None] * inv[None, :]
    cos = jnp.cos(ang)[None, :, None, :]
    sin = jnp.sin(ang)[None, :, None, :]
    xf = x.astype(F32)
    x1, x2 = xf[..., :half], xf[..., half:]
    return jnp.concatenate([x1 * cos - x2 * sin, x1 * sin + x2 * cos], axis=-1).astype(x.dtype)


def to_blocks(a, blk):
    b, t = a.shape[0], a.shape[1]
    return jnp.moveaxis(a.reshape(b, t // blk, blk, *a.shape[2:]), 1, 0)


def from_blocks(a):
    a = jnp.moveaxis(a, 0, 1)
    return a.reshape(a.shape[0], a.shape[1] * a.shape[2], *a.shape[3:])


def retention(q, k, v, s0):
    h = q.shape[2]
    t = q.shape[1]
    blk = min(RET_CHUNK, t)
    log_g = jnp.log1p(-jnp.exp2(-5.0 - jnp.arange(h, dtype=F32)))
    i = jnp.arange(blk, dtype=F32)
    diff = i[:, None] - i[None, :]
    dmask = jnp.where(diff[None] >= 0,
                      jnp.exp(jnp.maximum(diff, 0.0)[None] * log_g[:, None, None]), 0.0)
    q_dec = jnp.exp((i + 1.0)[:, None] * log_g[None, :])
    k_dec = jnp.exp((blk - 1.0 - i)[:, None] * log_g[None, :])
    c_dec = jnp.exp(blk * log_g)

    def step(s, inp):
        qc, kc, vc = inp
        sc = jnp.einsum('bihd,bjhd->bhij', qc, kc) * dmask[None]
        inner = jnp.einsum('bhij,bjhe->bihe', sc, vc)
        cross = jnp.einsum('bihd,bhde->bihe', qc, s) * q_dec[None, :, :, None]
        s_new = s * c_dec[None, :, None, None] + jnp.einsum('bjhd,bjhe->bhde', kc * k_dec[None, :, :, None], vc)
        return s_new, inner + cross

    s_fin, out = lax.scan(step, s0.astype(F32), (to_blocks(q, blk), to_blocks(k, blk), to_blocks(v, blk)))
    return from_blocks(out), s_fin


def stick_breaking(q, k, v, q_pos, k_pos):
    blk = min(Q_BLOCK, q.shape[1])
    scale = HEAD_DIM ** -0.5

    def one(args):
        qb, pb = args
        z = (jnp.einsum('bthd,bshd->bhts', qb, k) * scale).astype(F32)
        mask = k_pos[None, :] < pb[:, None]
        ln1m = jnp.where(mask, jax.nn.log_sigmoid(-z), 0.0)
        rc = lax.cumsum(ln1m, axis=3, reverse=True)
        excl = jnp.concatenate([rc[..., 1:], jnp.zeros_like(rc[..., :1])], axis=-1)
        a = jnp.where(mask, jnp.exp(jax.nn.log_sigmoid(z) + excl), 0.0)
        return jnp.einsum('bhts,bshd->bthd', a.astype(v.dtype), v)

    out = lax.map(one, (to_blocks(q, blk), q_pos.reshape(-1, blk)))
    return from_blocks(out)


def forgetting_attention(q, k, v, q_pos, k_pos, c_q, c_k):
    blk = min(Q_BLOCK, q.shape[1])
    scale = HEAD_DIM ** -0.5
    ck = jnp.transpose(c_k, (0, 2, 1))[:, :, None, :]

    def one(args):
        qb, pb, cb = args
        s = (jnp.einsum('bthd,bshd->bhts', qb, k) * scale).astype(F32)
        s = s + jnp.transpose(cb, (0, 2, 1))[..., None] - ck
        mask = k_pos[None, :] <= pb[:, None]
        p = jax.nn.softmax(jnp.where(mask, s, -jnp.inf), axis=-1)
        return jnp.einsum('bhts,bshd->bthd', p.astype(v.dtype), v)

    out = lax.map(one, (to_blocks(q, blk), q_pos.reshape(-1, blk), to_blocks(c_q, blk)))
    return from_blocks(out)


def mixer(h, q_pos, ret_s0, sb_pk, sb_pv, fox_pk, fox_pv, fox_plogf,
          w_in, b_f, g_ret_gn, g_fox_q, g_fox_k, w_out):
    b, t, _ = h.shape
    proj = h @ w_in
    splits = np.cumsum(IN_WIDTHS)[:-1].tolist()
    (q_r, k_r, v_r, g_r, q_s, k_s, v_s, g_s, q_f, k_f, v_f, g_f, f_logit) = jnp.split(proj, splits, axis=-1)

    qr = rotary(heads(q_r, H_RET), q_pos)
    kr = rotary(heads(k_r, H_RET), q_pos) * (HEAD_DIM ** -0.5)
    o_r, s_new = retention(qr, kr, heads(v_r, H_RET), ret_s0)
    o_r = group_norm_heads(o_r.astype(h.dtype), g_ret_gn).reshape(b, t, D_RET)

    ks_new, vs_new = heads(k_s, H_SB), heads(v_s, H_SB)
    past = sb_pk.shape[1]
    k_pos = jnp.arange(past + t, dtype=jnp.int32)
    ks_all = jnp.concatenate([sb_pk.astype(ks_new.dtype), ks_new], axis=1)
    vs_all = jnp.concatenate([sb_pv.astype(vs_new.dtype), vs_new], axis=1)
    o_s = stick_breaking(heads(q_s, H_SB), ks_all, vs_all, q_pos, k_pos).reshape(b, t, D_SB)

    qf = rms_norm(heads(q_f, H_FOX), g_fox_q)
    kf_new = rms_norm(heads(k_f, H_FOX), g_fox_k)
    vf_new = heads(v_f, H_FOX)
    logf_new = jax.nn.log_sigmoid((f_logit + b_f).astype(F32))
    kf_all = jnp.concatenate([fox_pk.astype(kf_new.dtype), kf_new], axis=1)
    vf_all = jnp.concatenate([fox_pv.astype(vf_new.dtype), vf_new], axis=1)
    logf_all = jnp.concatenate([fox_plogf.astype(F32), logf_new], axis=1)
    c_all = lax.cumsum(logf_all, axis=1)
    o_f = forgetting_attention(qf, kf_all, vf_all, q_pos, k_pos, c_all[:, -t:], c_all).reshape(b, t, D_FOX)

    o = jnp.concatenate([o_r * jax.nn.silu(g_r), o_s * jax.nn.silu(g_s), o_f * jax.nn.silu(g_f)], axis=-1)
    return (o @ w_out, s_new, ks_new, vs_new, kf_new, vf_new, logf_new)


def per_layer_embed(x, p, w_pe, g_pe, w_pg):
    gate = jax.nn.sigmoid((rms_norm(x, g_pe) @ w_pg).astype(F32)).astype(x.dtype)
    return x + (p @ w_pe) * gate


def gather_pages(pool, page_table):
    g = pool[page_table]
    return g.reshape(g.shape[0], g.shape[1] * g.shape[2], *g.shape[3:])


def setup_inputs(seed: int = 0) -> dict:
    key = jax.random.key(seed)
    ks = jax.random.split(key, 24)
    n_pages = PAST_LEN // PAGE_SIZE
    used = DEC_BATCH * n_pages
    n_pool = used + max(1, used // 4)
    perm = jax.random.permutation(ks[0], n_pool)
    page_table = perm[:used].reshape(DEC_BATCH, n_pages).astype(jnp.int32)
    nrm = jax.random.normal
    return {
        "x_prompt": nrm(ks[1], (BATCH, SEQ, D_MODEL), F32),
        "x_sample": nrm(ks[2], (DEC_BATCH, DEC_SEQ, D_MODEL), F32),
        "state_ret": 0.3 * nrm(ks[3], (DEPTH, DEC_BATCH, H_RET, HEAD_DIM, HEAD_DIM), F32),
        "cache_sb_k": nrm(ks[4], (DEPTH, n_pool, PAGE_SIZE, H_SB, HEAD_DIM), F32),
        "cache_sb_v": nrm(ks[5], (DEPTH, n_pool, PAGE_SIZE, H_SB, HEAD_DIM), F32),
        "cache_fox_k": nrm(ks[6], (DEPTH, n_pool, PAGE_SIZE, H_FOX, HEAD_DIM), F32),
        "cache_fox_v": nrm(ks[7], (DEPTH, n_pool, PAGE_SIZE, H_FOX, HEAD_DIM), F32),
        "cache_fox_logf": jax.nn.log_sigmoid(3.0 + nrm(ks[8], (DEPTH, n_pool, PAGE_SIZE, H_FOX), F32)),
        "page_table": page_table,
        "p_prompt": nrm(ks[9], (DEPTH, BATCH, SEQ, PLE_DIM), F32),
        "p_sample": nrm(ks[10], (DEPTH, DEC_BATCH, DEC_SEQ, PLE_DIM), F32),
        "g_norm": 1.0 + 0.05 * nrm(ks[11], (DEPTH, D_MODEL), F32),
        "w_in": nrm(ks[12], (DEPTH, D_MODEL, IN_COLS), F32) * D_MODEL ** -0.5,
        "b_f": jax.random.uniform(ks[13], (DEPTH, H_FOX), F32, 1.0, 5.0),
        "g_ret_gn": 1.0 + 0.05 * nrm(ks[14], (DEPTH, D_RET), F32),
        "g_fox_q": 1.0 + 0.05 * nrm(ks[15], (DEPTH, HEAD_DIM), F32),
        "g_fox_k": 1.0 + 0.05 * nrm(ks[16], (DEPTH, HEAD_DIM), F32),
        "w_out": nrm(ks[17], (DEPTH, D_MIX, D_MODEL), F32) * D_MIX ** -0.5,
        "w_pe": nrm(ks[18], (DEPTH, PLE_DIM, D_MODEL), F32) * PLE_DIM ** -0.5,
        "g_pe": 1.0 + 0.05 * nrm(ks[19], (DEPTH, D_MODEL), F32),
        "w_pg": nrm(ks[20], (DEPTH, D_MODEL, D_MODEL), F32) * D_MODEL ** -0.5,
    }


def reference(x_prompt, x_sample, state_ret, cache_sb_k, cache_sb_v, cache_fox_k, cache_fox_v,
              cache_fox_logf, page_table, p_prompt, p_sample,
              g_norm, w_in, b_f, g_ret_gn, g_fox_q, g_fox_k, w_out, w_pe, g_pe, w_pg):
    bp, tp = x_prompt.shape[0], x_prompt.shape[1]
    ts = x_sample.shape[1]
    past_len = page_table.shape[1] * PAGE_SIZE
    pos_p = jnp.arange(tp, dtype=jnp.int32)
    pos_s = past_len + jnp.arange(ts, dtype=jnp.int32)
    dt = x_prompt.dtype
    zero_ret = jnp.zeros((bp, H_RET, HEAD_DIM, HEAD_DIM), F32)
    empty_sb = jnp.zeros((bp, 0, H_SB, HEAD_DIM), dt)
    empty_fox = jnp.zeros((bp, 0, H_FOX, HEAD_DIM), dt)
    empty_logf = jnp.zeros((bp, 0, H_FOX), F32)

    y_p, y_s = x_prompt, x_sample
    rs_p, rs_s, skp, svp, sks, svs = [], [], [], [], [], []
    fkp, fvp, flp, fks, fvs, fls = [], [], [], [], [], []
    for i in range(DEPTH):
        w = (w_in[i], b_f[i], g_ret_gn[i], g_fox_q[i], g_fox_k[i], w_out[i])
        m, s_new, k1, v1, k2, v2, l2 = mixer(rms_norm(y_p, g_norm[i]), pos_p, zero_ret,
                                             empty_sb, empty_sb, empty_fox, empty_fox, empty_logf, *w)
        y_p = per_layer_embed(y_p + m, p_prompt[i], w_pe[i], g_pe[i], w_pg[i])
        rs_p.append(s_new); skp.append(k1); svp.append(v1); fkp.append(k2); fvp.append(v2); flp.append(l2)
        m, s_new, k1, v1, k2, v2, l2 = mixer(
            rms_norm(y_s, g_norm[i]), pos_s, state_ret[i],
            gather_pages(cache_sb_k[i], page_table), gather_pages(cache_sb_v[i], page_table),
            gather_pages(cache_fox_k[i], page_table), gather_pages(cache_fox_v[i], page_table),
            gather_pages(cache_fox_logf[i], page_table), *w)
        y_s = per_layer_embed(y_s + m, p_sample[i], w_pe[i], g_pe[i], w_pg[i])
        rs_s.append(s_new); sks.append(k1); svs.append(v1); fks.append(k2); fvs.append(v2); fls.append(l2)

    st = lambda lst: jnp.stack(lst, axis=0)
    return (y_p, y_s, st(rs_p), st(rs_s), st(skp), st(svp), st(sks), st(svs),
            st(fkp), st(fvp), st(flp), st(fks), st(fvs), st(fls))
```

```python
import functools

import jax
import jax.numpy as jnp
import numpy as np
from jax import lax
from jax.experimental import pallas as pl
from jax.experimental.pallas import tpu as pltpu

F32 = jnp.float32
BF16 = jnp.bfloat16

HEAD_DIM = 64
H_RET, H_SB, H_FOX = 6, 5, 5
D_RET, D_SB, D_FOX = H_RET * HEAD_DIM, H_SB * HEAD_DIM, H_FOX * HEAD_DIM
PAGE_SIZE = 128
RET_CHUNK = 128
ATT_BLOCK = 128
LANES = 128
SUBLANES = 8
ROPE_BASE = 10000.0
EPS = 1e-6
QK_SCALE = HEAD_DIM ** -0.5
SB_DEAD = -104.0
NEG_BIG = -1e30
PIECE = 384
VMEM_LIMIT = 48 * 1024 * 1024


def _dot(a, b):
    return jnp.dot(a, b, preferred_element_type=F32)


def _dot_nt(a, b):
    return lax.dot_general(a, b, (((1,), (1,)), ((), ())), preferred_element_type=F32)


def _dot_tn(a, b):
    return lax.dot_general(a, b, (((0,), (0,)), ((), ())), preferred_element_type=F32)


def _split2(x):
    hi = x.astype(BF16)
    lo = (x - hi.astype(F32)).astype(BF16)
    return hi, lo


def _split3(x):
    p1 = x.astype(BF16)
    r1 = x - p1.astype(F32)
    p2 = r1.astype(BF16)
    p3 = (r1 - p2.astype(F32)).astype(BF16)
    return p1, p2, p3


def _dot_exact2(x, m):
    hi, lo = _split2(x)
    return _dot(hi, m) + _dot(lo, m)


def _dot_exact3(x, m):
    p1, p2, p3 = _split3(x)
    return _dot(p1, m) + _dot(p2, m) + _dot(p3, m)


def _softplus(z):
    return jnp.maximum(z, 0.0) + jnp.log1p(jnp.exp(-jnp.abs(z)))


def _log_sigmoid(x):
    return jnp.minimum(x, 0.0) - jnp.log1p(jnp.exp(-jnp.abs(x)))


def _silu(g):
    return g * jax.nn.sigmoid(g)


def _rotary_rows(x, cos, sin_signed):
    lane = lax.broadcasted_iota(jnp.int32, (x.shape[0], LANES), 1)
    first_half = (lane % HEAD_DIM) < (HEAD_DIM // 2)
    out = []
    for c in range(x.shape[1] // LANES):
        xs = x[:, c * LANES:(c + 1) * LANES]
        partner = jnp.where(first_half,
                            pltpu.roll(xs, LANES - HEAD_DIM // 2, axis=1),
                            pltpu.roll(xs, HEAD_DIM // 2, axis=1))
        out.append(xs * cos + partner * sin_signed)
    return jnp.concatenate(out, axis=1)


def _inproj_prompt_kernel(x_ref, g_ref, cos_ref, sin_ref, cost_ref, sint_ref, bf_ref, gk_ref,
                          wrow_ref, wkr_ref, wks_ref, wvs_ref, wkf_ref, wvf_ref, wfl_ref,
                          qr_ref, vr_ref, gr_ref, qs_ref, gs_ref, qf_ref, gf_ref,
                          krt_ref, kst_ref, vst_ref, kft_ref, vft_ref, lft_ref):
    x = x_ref[...]
    ms = jnp.mean(x * x, axis=-1, keepdims=True)
    h = (x * lax.rsqrt(ms + EPS) * g_ref[...]).astype(BF16)

    row = _dot(h, wrow_ref[...])
    qr_ref[...] = _rotary_rows(row[:, 0:D_RET], cos_ref[...], sin_ref[...])
    vr_ref[...] = row[:, D_RET:2 * D_RET]
    gr_ref[...] = row[:, 2 * D_RET:3 * D_RET]
    o = 3 * D_RET
    qs_ref[...] = row[:, o:o + D_SB]
    gs_ref[...] = row[:, o + D_SB:o + 2 * D_SB]
    qf_ref[...] = row[:, o + 2 * D_SB:o + 3 * D_SB]
    gf_ref[...] = row[:, o + 3 * D_SB:o + 4 * D_SB]

    half = HEAD_DIM // 2
    krt = _dot_nt(wkr_ref[...], h)
    cost, sint = cost_ref[...], sint_ref[...]
    for hd in range(H_RET):
        x1 = krt[hd * HEAD_DIM:hd * HEAD_DIM + half]
        x2 = krt[hd * HEAD_DIM + half:(hd + 1) * HEAD_DIM]
        krt_ref[hd * HEAD_DIM:hd * HEAD_DIM + half, :] = (x1 * cost - x2 * sint) * QK_SCALE
        krt_ref[hd * HEAD_DIM + half:(hd + 1) * HEAD_DIM, :] = (x1 * sint + x2 * cost) * QK_SCALE

    kst_ref[...] = _dot_nt(wks_ref[...], h)
    vst_ref[...] = _dot_nt(wvs_ref[...], h)
    vft_ref[...] = _dot_nt(wvf_ref[...], h)
    kft = _dot_nt(wkf_ref[...], h)
    gk = gk_ref[...]
    for hd in range(H_FOX):
        xh = kft[hd * HEAD_DIM:(hd + 1) * HEAD_DIM]
        msh = jnp.mean(xh * xh, axis=0, keepdims=True)
        kft_ref[hd * HEAD_DIM:(hd + 1) * HEAD_DIM, :] = xh * lax.rsqrt(msh + EPS) * gk
    lft_ref[...] = _log_sigmoid(_dot_nt(wfl_ref[...], h) + bf_ref[...])


def _inproj_prompt(x, g_norm, tabs, b_f8, gk_col, w, tm):
    bsz, t, d = x.shape
    n_row = 3 * D_RET + 4 * D_SB
    grid = (bsz, t // tm)
    row_spec = lambda n: pl.BlockSpec((None, tm, n), lambda b, i: (b, i, 0))
    col_spec = lambda n: pl.BlockSpec((None, n, tm), lambda b, i: (b, 0, i))
    full = lambda a: pl.BlockSpec(a.shape, lambda b, i: (0,) * a.ndim)
    cos, sin, cost, sint = tabs
    in_specs = [
        row_spec(d), full(g_norm),
        pl.BlockSpec((tm, LANES), lambda b, i: (i, 0)), pl.BlockSpec((tm, LANES), lambda b, i: (i, 0)),
        pl.BlockSpec((HEAD_DIM // 2, tm), lambda b, i: (0, i)), pl.BlockSpec((HEAD_DIM // 2, tm), lambda b, i: (0, i)),
        full(b_f8), full(gk_col),
    ] + [full(a) for a in w]
    rs = lambda n: jax.ShapeDtypeStruct((bsz, t, n), F32)
    cs = lambda n: jax.ShapeDtypeStruct((bsz, n, t), F32)
    out_shape = [rs(D_RET), rs(D_RET), rs(D_RET), rs(D_SB), rs(D_SB), rs(D_FOX), rs(D_FOX),
                 cs(D_RET), cs(D_SB), cs(D_SB), cs(D_FOX), cs(D_FOX), cs(SUBLANES)]
    out_specs = [row_spec(D_RET)] * 3 + [row_spec(D_SB)] * 4 + \
                [col_spec(D_RET)] + [col_spec(D_SB)] * 4 + [col_spec(SUBLANES)]
    return pl.pallas_call(
        _inproj_prompt_kernel, grid=grid, in_specs=in_specs, out_specs=out_specs, out_shape=out_shape,
        compiler_params=pltpu.CompilerParams(dimension_semantics=("parallel", "parallel"),
                                             vmem_limit_bytes=VMEM_LIMIT),
        name="inproj_prompt",
    )(x, g_norm, cos, sin, cost, sint, b_f8, gk_col, *w)


def _ret_prompt_kernel(qr_ref, krt_ref, vr_ref, gr_ref, dmask_ref, qdec_ref, kdec_ref, cdec_ref, ggn_ref,
                       o_ref, sfin_ref, s_scr):
    c = pl.program_id(1)

    @pl.when(c == 0)
    def _():
        s_scr[...] = jnp.zeros_like(s_scr)

    for hd in range(H_RET):
        sl = slice(hd * HEAD_DIM, (hd + 1) * HEAD_DIM)
        q = qr_ref[:, sl].astype(BF16)
        kt = krt_ref[sl, :]
        v = vr_ref[:, sl].astype(BF16)
        sc = _dot(q, kt.astype(BF16)) * dmask_ref[hd]
        inner = _dot(sc.astype(BF16), v)
        s_old = s_scr[hd]
        cross = _dot(q, s_old.astype(BF16)) * qdec_ref[hd]
        o = inner + cross
        s_scr[hd] = s_old * cdec_ref[hd] + _dot((kt * kdec_ref[hd]).astype(BF16), v)
        mu = jnp.mean(o, axis=-1, keepdims=True)
        dev = o - mu
        var = jnp.mean(dev * dev, axis=-1, keepdims=True)
        y = dev * lax.rsqrt(var + EPS) * ggn_ref[:, sl]
        o_ref[:, sl] = y * _silu(gr_ref[:, sl])

    @pl.when(c == pl.num_programs(1) - 1)
    def _():
        sfin_ref[...] = s_scr[...]


def _ret_prompt(qr, krt, vr, gr, dec, ggn):
    bsz, t, _ = qr.shape
    blk = RET_CHUNK
    dmask, qdec, kdec, cdec = dec
    row = pl.BlockSpec((None, blk, D_RET), lambda b, c: (b, c, 0))
    full = lambda a: pl.BlockSpec(a.shape, lambda b, c: (0,) * a.ndim)
    return pl.pallas_call(
        _ret_prompt_kernel, grid=(bsz, t // blk),
        in_specs=[row, pl.BlockSpec((None, D_RET, blk), lambda b, c: (b, 0, c)), row, row,
                  full(dmask), full(qdec), full(kdec), full(cdec), full(ggn)],
        out_specs=[row, pl.BlockSpec((None, H_RET, HEAD_DIM, HEAD_DIM), lambda b, c: (b, 0, 0, 0))],
        out_shape=[jax.ShapeDtypeStruct((bsz, t, D_RET), F32),
                   jax.ShapeDtypeStruct((bsz, H_RET, HEAD_DIM, HEAD_DIM), F32)],
        scratch_shapes=[pltpu.VMEM((H_RET, HEAD_DIM, HEAD_DIM), F32)],
        compiler_params=pltpu.CompilerParams(dimension_semantics=("parallel", "arbitrary"),
                                             vmem_limit_bytes=VMEM_LIMIT),
        name="ret_prompt",
    )(qr, krt, vr, gr, dmask, qdec, kdec, cdec, ggn)


def _sb_block(q, kt, vt, lmat, carry, acc, mask):
    z = _dot(q, kt.astype(BF16))
    sp = _softplus(z)
    ln1m = -sp
    ls = z - sp
    if mask is not None:
        ln1m = jnp.where(mask, ln1m, 0.0)
    excl = _dot_exact2(ln1m, lmat) + carry
    a = jnp.exp(ls + excl)
    if mask is not None:
        a = jnp.where(mask, a, 0.0)
    acc = acc + _dot_nt(a.astype(BF16), vt.astype(BF16))
    carry = carry + jnp.sum(ln1m, axis=-1, keepdims=True)
    return carry, acc


def _sb_prompt_kernel(q_ref, kt_ref, vt_ref, g_ref, lmat_ref, o_ref):
    i = pl.program_id(1)
    tq = q_ref.shape[0]
    rowi = lax.broadcasted_iota(jnp.int32, (tq, ATT_BLOCK), 0)
    coli = lax.broadcasted_iota(jnp.int32, (tq, ATT_BLOCK), 1)
    diag_mask = coli < rowi
    lmat = lmat_ref[...]
    for hd in range(H_SB):
        sl = slice(hd * HEAD_DIM, (hd + 1) * HEAD_DIM)
        q = (q_ref[:, sl] * QK_SCALE).astype(BF16)

        def load(j, sl=sl):
            off = pl.multiple_of(j * ATT_BLOCK, ATT_BLOCK)
            return kt_ref[sl, pl.ds(off, ATT_BLOCK)], vt_ref[sl, pl.ds(off, ATT_BLOCK)]

        kt, vt = load(i)
        carry, acc = _sb_block(q, kt, vt, lmat, jnp.zeros((tq, 1), F32),
                               jnp.zeros((tq, HEAD_DIM), F32), diag_mask)

        def cond(st):
            j, _, _, cmax = st
            return jnp.logical_and(j >= 0, cmax > SB_DEAD)

        def body(st, q=q, load=load):
            j, carry, acc, _ = st
            kt, vt = load(j)
            carry, acc = _sb_block(q, kt, vt, lmat, carry, acc, None)
            return j - 1, carry, acc, jnp.max(carry)

        _, _, acc, _ = lax.while_loop(cond, body, (i - 1, carry, acc, jnp.max(carry)))
        o_ref[:, sl] = acc * _silu(g_ref[:, sl])


def _sb_prompt(qs, kst, vst, gs, lmat):
    bsz, t, _ = qs.shape
    tq = ATT_BLOCK
    row = pl.BlockSpec((None, tq, D_SB), lambda b, i: (b, i, 0))
    seq = pl.BlockSpec((None, D_SB, t), lambda b, i: (b, 0, 0))
    return pl.pallas_call(
        _sb_prompt_kernel, grid=(bsz, t // tq),
        in_specs=[row, seq, seq, row, pl.BlockSpec(lmat.shape, lambda b, i: (0, 0))],
        out_specs=row, out_shape=jax.ShapeDtypeStruct((bsz, t, D_SB), F32),
        compiler_params=pltpu.CompilerParams(dimension_semantics=("parallel", "arbitrary"),
                                             vmem_limit_bytes=VMEM_LIMIT),
        name="sb_prompt",
    )(qs, kst, vst, gs, lmat)


def _fox_prompt_kernel(q_ref, kt_ref, vt_ref, lft_ref, g_ref, gq_ref, seg_ref, umat_ref, o_ref, c_scr):
    i = pl.program_id(1)
    tq = q_ref.shape[0]
    t = kt_ref.shape[1]

    @pl.when(i == 0)
    def _():
        umat = umat_ref[...]
        carry = jnp.zeros((SUBLANES, 1), F32)
        for c in range(t // ATT_BLOCK):
            cs = _dot_exact3(lft_ref[:, c * ATT_BLOCK:(c + 1) * ATT_BLOCK], umat) + carry
            c_scr[:, c * ATT_BLOCK:(c + 1) * ATT_BLOCK] = cs
            carry = cs[:, ATT_BLOCK - 1:ATT_BLOCK]

    qraw = q_ref[...]
    ssq = _dot_exact2(qraw * qraw, seg_ref[...])
    qn = qraw * lax.rsqrt(ssq * (1.0 / HEAD_DIM) + EPS) * gq_ref[...] * QK_SCALE
    rowi = lax.broadcasted_iota(jnp.int32, (tq, ATT_BLOCK), 0)
    coli = lax.broadcasted_iota(jnp.int32, (tq, ATT_BLOCK), 1)
    diag_mask = coli <= rowi
    for hd in range(H_FOX):
        sl = slice(hd * HEAD_DIM, (hd + 1) * HEAD_DIM)
        q = qn[:, sl].astype(BF16)

        def block(j, m, l, acc, mask, q=q, sl=sl, hd=hd):
            off = pl.multiple_of(j * ATT_BLOCK, ATT_BLOCK)
            kt = kt_ref[sl, pl.ds(off, ATT_BLOCK)]
            vt = vt_ref[sl, pl.ds(off, ATT_BLOCK)]
            s = _dot(q, kt.astype(BF16)) - c_scr[hd:hd + 1, pl.ds(off, ATT_BLOCK)]
            if mask is not None:
                s = jnp.where(mask, s, NEG_BIG)
            m_new = jnp.maximum(m, jnp.max(s, axis=-1, keepdims=True))
            alpha = jnp.exp(m - m_new)
            p = jnp.exp(s - m_new)
            l = alpha * l + jnp.sum(p, axis=-1, keepdims=True)
            acc = alpha * acc + _dot_nt(p.astype(BF16), vt.astype(BF16))
            return m_new, l, acc

        def body(j, st, block=block):
            return block(j, *st, None)

        st = (jnp.full((tq, 1), NEG_BIG, F32), jnp.zeros((tq, 1), F32), jnp.zeros((tq, HEAD_DIM), F32))
        st = lax.fori_loop(0, i, body, st)
        _, l, acc = block(i, *st, diag_mask)
        o_ref[:, sl] = acc / l * _silu(g_ref[:, sl])


def _fox_prompt(qf, kft, vft, lft, gf, gq_row, seg, umat):
    bsz, t, _ = qf.shape
    tq = ATT_BLOCK
    row = pl.BlockSpec((None, tq, D_FOX), lambda b, i: (b, i, 0))
    seq = pl.BlockSpec((None, D_FOX, t), lambda b, i: (b, 0, 0))
    full = lambda a: pl.BlockSpec(a.shape, lambda b, i: (0,) * a.ndim)
    return pl.pallas_call(
        _fox_prompt_kernel, grid=(bsz, t // tq),
        in_specs=[row, seq, seq, pl.BlockSpec((None, SUBLANES, t), lambda b, i: (b, 0, 0)), row,
                  full(gq_row), full(seg), full(umat)],
        out_specs=row, out_shape=jax.ShapeDtypeStruct((bsz, t, D_FOX), F32),
        scratch_shapes=[pltpu.VMEM((SUBLANES, t), F32)],
        compiler_params=pltpu.CompilerParams(dimension_semantics=("parallel", "arbitrary"),
                                             vmem_limit_bytes=VMEM_LIMIT),
        name="fox_prompt",
    )(qf, kft, vft, lft, gf, gq_row, seg, umat)


def _outproj_kernel(y_ref, or_ref, os_ref, of_ref, p_ref, wor_ref, wos_ref, wof_ref, wpe_ref, gpe_ref, wpg_ref,
                    o_ref):
    m = (_dot(or_ref[...].astype(BF16), wor_ref[...]) + _dot(os_ref[...].astype(BF16), wos_ref[...])
         + _dot(of_ref[...].astype(BF16), wof_ref[...]))
    y1 = y_ref[...] + m
    ms = jnp.mean(y1 * y1, axis=-1, keepdims=True)
    n = (y1 * lax.rsqrt(ms + EPS) * gpe_ref[...]).astype(BF16)
    gate = jax.nn.sigmoid(_dot(n, wpg_ref[...]))
    pe = _dot(p_ref[...].astype(BF16), wpe_ref[...])
    o_ref[...] = y1 + pe * gate


def _outproj(y, o_r, o_s, o_f, p, w, tm):
    n, d = y.shape
    row = lambda a: pl.BlockSpec((tm, a.shape[1]), lambda i: (i, 0))
    full = lambda a: pl.BlockSpec(a.shape, lambda i: (0,) * a.ndim)
    return pl.pallas_call(
        _outproj_kernel, grid=(n // tm,),
        in_specs=[row(y), row(o_r), row(o_s), row(o_f), row(p)] + [full(a) for a in w],
        out_specs=row(y), out_shape=jax.ShapeDtypeStruct((n, d), F32),
        compiler_params=pltpu.CompilerParams(dimension_semantics=("parallel",), vmem_limit_bytes=VMEM_LIMIT),
        name="outproj",
    )(y, o_r, o_s, o_f, p, *w)


def _inproj_sample_kernel(x_ref, g_ref, w_ref, wfl_ref, o_ref, flt_ref):
    x = x_ref[...]
    ms = jnp.mean(x * x, axis=-1, keepdims=True)
    h = (x * lax.rsqrt(ms + EPS) * g_ref[...]).astype(BF16)
    o_ref[...] = _dot(h, w_ref[...])
    flt_ref[...] = _dot_nt(wfl_ref[...], h)


def _inproj_sample(x, g_norm, w_pieces, wfl):
    n, d = x.shape
    ncol = w_pieces.shape[1]
    full = lambda a: pl.BlockSpec(a.shape, lambda i: (0,) * a.ndim)
    return pl.pallas_call(
        _inproj_sample_kernel, grid=(1,),
        in_specs=[full(x), full(g_norm), full(w_pieces), full(wfl)],
        out_specs=[pl.BlockSpec((n, ncol), lambda i: (0, 0)), pl.BlockSpec((SUBLANES, n), lambda i: (0, 0))],
        out_shape=[jax.ShapeDtypeStruct((n, ncol), F32), jax.ShapeDtypeStruct((SUBLANES, n), F32)],
        compiler_params=pltpu.CompilerParams(dimension_semantics=("arbitrary",), vmem_limit_bytes=VMEM_LIMIT),
        name="inproj_sample",
    )(x, g_norm, w_pieces, wfl)


def _ret_sample_kernel(q_ref, k_ref, v_ref, g_ref, s0_ref, cos_ref, sin_ref, dmask_ref, qdec_ref, kdec_ref,
                       cdec_ref, ggn_ref, o_ref, s_ref):
    cos, sin = cos_ref[...], sin_ref[...]
    qr = _rotary_rows(q_ref[...], cos, sin)
    kr = _rotary_rows(k_ref[...], cos, sin) * QK_SCALE
    for hd in range(H_RET):
        sl = slice(hd * HEAD_DIM, (hd + 1) * HEAD_DIM)
        q = qr[:, sl].astype(BF16)
        k = kr[:, sl]
        v = v_ref[:, sl].astype(BF16)
        sc = _dot_nt(q, k.astype(BF16)) * dmask_ref[hd]
        inner = _dot(sc.astype(BF16), v)
        s_old = s0_ref[hd]
        cross = _dot(q, s_old.astype(BF16)) * qdec_ref[hd]
        o = inner + cross
        s_ref[hd] = s_old * cdec_ref[hd] + _dot_tn((k * kdec_ref[hd]).astype(BF16), v)
        mu = jnp.mean(o, axis=-1, keepdims=True)
        dev = o - mu
        var = jnp.mean(dev * dev, axis=-1, keepdims=True)
        y = dev * lax.rsqrt(var + EPS) * ggn_ref[:, sl]
        o_ref[:, sl] = y * _silu(g_ref[:, sl])


def _ret_sample(proj, s0, tabs, dec, ggn, ts):
    n = proj.shape[0]
    bsz = n // ts
    cos, sin = tabs
    dmask, qdec, kdec, cdec = dec
    piece = lambda k: pl.BlockSpec((ts, PIECE), lambda b, k=k: (b, k))
    full = lambda a: pl.BlockSpec(a.shape, lambda b: (0,) * a.ndim)
    st = pl.BlockSpec((None, H_RET, HEAD_DIM, HEAD_DIM), lambda b: (b, 0, 0, 0))
    return pl.pallas_call(
        _ret_sample_kernel, grid=(bsz,),
        in_specs=[piece(0), piece(1), piece(2), piece(3), st, full(cos), full(sin),
                  full(dmask), full(qdec), full(kdec), full(cdec), full(ggn)],
        out_specs=[pl.BlockSpec((ts, D_RET), lambda b: (b, 0)), st],
        out_shape=[jax.ShapeDtypeStruct((n, D_RET), F32),
                   jax.ShapeDtypeStruct((bsz, H_RET, HEAD_DIM, HEAD_DIM), F32)],
        compiler_params=pltpu.CompilerParams(dimension_semantics=("parallel",), vmem_limit_bytes=VMEM_LIMIT),
        name="ret_sample",
    )(proj, proj, proj, proj, s0, cos, sin, dmask, qdec, kdec, cdec, ggn)


def _decode_kernel(pt_ref, qs_ref, ks_ref, vs_ref, gs_ref, qf_ref, kf_ref, vf_ref, gf_ref, flt_ref,
                   bf_ref, gq_ref, gk_ref, seg_ref, lmat_ref, *rest, n_grp, ts):
    pages = rest[:5 * n_grp]
    os_ref, of_ref, kfn_ref, lfn_ref = rest[5 * n_grp:5 * n_grp + 4]
    (qbs_scr, qbf_scr, accs_scr, cars_scr, accf_scr, mf_scr, lf_scr, rcar_scr) = rest[5 * n_grp + 4:]
    b = pl.program_id(0)
    s = pl.program_id(1)
    rows = H_SB * ts
    lmat = lmat_ref[...]
    rowi = lax.broadcasted_iota(jnp.int32, (rows, PAGE_SIZE), 0)
    coli = lax.broadcasted_iota(jnp.int32, (rows, PAGE_SIZE), 1)
    tpos = rowi % ts

    def head_rows(x):
        r = lax.broadcasted_iota(jnp.int32, (rows, D_SB), 0)
        c = lax.broadcasted_iota(jnp.int32, (rows, D_SB), 1)
        tiled = jnp.concatenate([x] * H_SB, axis=0)
        return jnp.where(r // ts == c // HEAD_DIM, tiled, 0.0)

    def head_cols(x):
        c = lax.broadcasted_iota(jnp.int32, (ts, D_SB), 1)
        out = jnp.zeros((ts, D_SB), F32)
        for hd in range(H_SB):
            out = out + jnp.where(c // HEAD_DIM == hd, x[hd * ts:(hd + 1) * ts], 0.0)
        return out

    def rep_heads(x):
        return jnp.concatenate([jnp.broadcast_to(x[hd:hd + 1], (ts, x.shape[1])) for hd in range(H_SB)], axis=0)

    def sb_update(z, v_bf, v_nt, mask):
        sp = _softplus(z)
        ln1m = -sp
        ls = z - sp
        if mask is not None:
            ln1m = jnp.where(mask, ln1m, 0.0)
        excl = _dot_exact2(ln1m, lmat) + cars_scr[...]
        a = jnp.exp(ls + excl)
        if mask is not None:
            a = jnp.where(mask, a, 0.0)
        a = a.astype(BF16)
        accs_scr[...] += _dot_nt(a, v_bf) if v_nt else _dot(a, v_bf)
        cars_scr[...] += jnp.sum(ln1m, axis=-1, keepdims=True)

    def fox_update(sc, bias, v_bf, v_nt, mask):
        sc = sc + bias
        if mask is not None:
            sc = jnp.where(mask, sc, NEG_BIG)
        m_old = mf_scr[...]
        m_new = jnp.maximum(m_old, jnp.max(sc, axis=-1, keepdims=True))
        alpha = jnp.exp(m_old - m_new)
        p = jnp.exp(sc - m_new)
        lf_scr[...] = alpha * lf_scr[...] + jnp.sum(p, axis=-1, keepdims=True)
        p = p.astype(BF16)
        accf_scr[...] = alpha * accf_scr[...] + (_dot_nt(p, v_bf) if v_nt else _dot(p, v_bf))
        mf_scr[...] = m_new

    @pl.when(s == 0)
    def _():
        pad = jnp.zeros((PAGE_SIZE - ts, D_SB), F32)
        qbs = head_rows(qs_ref[:, :D_SB] * QK_SCALE).astype(BF16)
        qbs_scr[...] = qbs
        accs_scr[...] = jnp.zeros_like(accs_scr)
        cars_scr[...] = jnp.zeros_like(cars_scr)
        k_new = jnp.concatenate([ks_ref[:, :D_SB], pad], axis=0).astype(BF16)
        v_new = jnp.concatenate([vs_ref[:, :D_SB], pad], axis=0).astype(BF16)
        sb_update(_dot_nt(qbs, k_new), v_new, False, coli < tpos)
        seg = seg_ref[...]
        qraw = qf_ref[:, :D_FOX]
        qn = qraw * lax.rsqrt(_dot_exact2(qraw * qraw, seg) * (1.0 / HEAD_DIM) + EPS) * gq_ref[...] * QK_SCALE
        qbf = head_rows(qn).astype(BF16)
        qbf_scr[...] = qbf
        kraw = kf_ref[:, :D_FOX]
        kn = kraw * lax.rsqrt(_dot_exact2(kraw * kraw, seg) * (1.0 / HEAD_DIM) + EPS) * gk_ref[...]
        kfn_ref[...] = kn
        lf_new = _log_sigmoid(flt_ref[...] + bf_ref[...])
        lfn_ref[...] = lf_new
        lf_pad = jnp.concatenate([lf_new, jnp.zeros((SUBLANES, PAGE_SIZE - ts), F32)], axis=1)
        lf_rows = rep_heads(lf_pad)
        cn = jnp.sum(lf_rows, axis=-1, keepdims=True) - _dot_exact3(lf_rows, lmat)
        accf_scr[...] = jnp.zeros_like(accf_scr)
        lf_scr[...] = jnp.zeros_like(lf_scr)
        mf_scr[...] = jnp.full_like(mf_scr, NEG_BIG)
        rcar_scr[...] = jnp.zeros_like(rcar_scr)
        kn_pad = jnp.concatenate([kn, pad], axis=0).astype(BF16)
        vf_new = jnp.concatenate([vf_ref[:, :D_FOX], pad], axis=0).astype(BF16)
        fox_update(_dot_nt(qbf, kn_pad), -cn, vf_new, False, coli <= tpos)

    n_steps = pl.num_programs(1)
    for gi in range(n_grp - 1, -1, -1):
        skt_ref, svt_ref, fkt_ref, fvt_ref, lfp_ref = pages[5 * gi:5 * gi + 5]
        sb_update(_dot(qbs_scr[...], skt_ref[...].astype(BF16)), svt_ref[...].astype(BF16), True, None)
        page = pt_ref[b, (n_steps - 1 - s) * n_grp + gi]
        lf_pg = lfp_ref[:, pl.ds(page % SUBLANES, 1), :]
        lf_rows = jnp.concatenate([jnp.broadcast_to(lf_pg[hd], (ts, PAGE_SIZE)) for hd in range(H_FOX)], axis=0)
        bias = _dot_exact3(lf_rows, lmat) + rcar_scr[...]
        fox_update(_dot(qbf_scr[...], fkt_ref[...].astype(BF16)), bias, fvt_ref[...].astype(BF16), True, None)
        rcar_scr[...] += jnp.sum(lf_rows, axis=-1, keepdims=True)

    @pl.when(s == n_steps - 1)
    def _():
        os_ref[...] = head_cols(accs_scr[...]) * _silu(gs_ref[:, :D_SB])
        of_ref[...] = head_cols(accf_scr[...] / lf_scr[...]) * _silu(gf_ref[:, :D_FOX])


def _decode(proj, flt, pools, lf_pool, page_table, layer, b_f8, gq_row, gk_row, seg, lmat, ts, n_grp):
    n = proj.shape[0]
    bsz, n_pages = page_table.shape
    n_steps = n_pages // n_grp
    rows = H_SB * ts
    piece = lambda k: pl.BlockSpec((ts, PIECE), lambda b, s, pt, k=k: (b, k))
    full = lambda a: pl.BlockSpec(a.shape, lambda b, s, pt: (0,) * a.ndim)

    def page_spec(gi):
        return pl.BlockSpec((None, None, D_SB, PAGE_SIZE),
                            lambda b, s, pt, gi=gi: (layer, pt[b, (n_steps - 1 - s) * n_grp + gi], 0, 0))

    def lf_spec(gi):
        return pl.BlockSpec((None, H_FOX, SUBLANES, PAGE_SIZE),
                            lambda b, s, pt, gi=gi: (layer, 0, pt[b, (n_steps - 1 - s) * n_grp + gi] // SUBLANES, 0))

    page_specs, page_args = [], []
    for gi in range(n_grp):
        page_specs += [page_spec(gi)] * 4 + [lf_spec(gi)]
        page_args += list(pools) + [lf_pool]
    out_row = pl.BlockSpec((ts, D_SB), lambda b, s, pt: (b, 0))
    grid_spec = pltpu.PrefetchScalarGridSpec(
        num_scalar_prefetch=1, grid=(bsz, n_steps),
        in_specs=[piece(4), piece(5), piece(6), piece(7), piece(8), piece(9), piece(10), piece(11),
                  pl.BlockSpec((None, SUBLANES, ts), lambda b, s, pt: (b, 0, 0)),
                  full(b_f8), full(gq_row), full(gk_row), full(seg), full(lmat)] + page_specs,
        out_specs=[out_row, out_row, out_row, pl.BlockSpec((None, SUBLANES, ts), lambda b, s, pt: (b, 0, 0))],
        scratch_shapes=[pltpu.VMEM((rows, D_SB), BF16), pltpu.VMEM((rows, D_FOX), BF16),
                        pltpu.VMEM((rows, D_SB), F32), pltpu.VMEM((rows, 1), F32),
                        pltpu.VMEM((rows, D_FOX), F32), pltpu.VMEM((rows, 1), F32),
                        pltpu.VMEM((rows, 1), F32), pltpu.VMEM((rows, 1), F32)],
    )
    return pl.pallas_call(
        functools.partial(_decode_kernel, n_grp=n_grp, ts=ts), grid_spec=grid_spec,
        out_shape=[jax.ShapeDtypeStruct((n, D_SB), F32), jax.ShapeDtypeStruct((n, D_FOX), F32),
                   jax.ShapeDtypeStruct((n, D_FOX), F32), jax.ShapeDtypeStruct((bsz, SUBLANES, ts), F32)],
        compiler_params=pltpu.CompilerParams(dimension_semantics=("parallel", "arbitrary"),
                                             vmem_limit_bytes=VMEM_LIMIT),
        name="decode_attn",
    )(page_table, *([proj] * 8), flt, b_f8, gq_row, gk_row, seg, lmat, *page_args)


def _rotary_tables(pos):
    half = HEAD_DIM // 2
    inv = ROPE_BASE ** (-jnp.arange(half, dtype=F32) / half)
    ang = pos.astype(F32)[:, None] * inv[None, :]
    cos, sin = jnp.cos(ang), jnp.sin(ang)
    reps = LANES // HEAD_DIM
    cos_row = jnp.tile(jnp.concatenate([cos, cos], axis=1), (1, reps))
    sin_row = jnp.tile(jnp.concatenate([-sin, sin], axis=1), (1, reps))
    return cos_row, sin_row, cos.T, sin.T


def _decay_tables(blk):
    log_g = jnp.log1p(-jnp.exp2(-5.0 - jnp.arange(H_RET, dtype=F32)))
    i = jnp.arange(blk, dtype=F32)
    diff = i[:, None] - i[None, :]
    dmask = jnp.where(diff[None] >= 0, jnp.exp(jnp.maximum(diff, 0.0)[None] * log_g[:, None, None]), 0.0)
    q_dec = jnp.exp((i + 1.0)[None, :] * log_g[:, None])
    k_dec = jnp.exp((blk - 1.0 - i)[None, :] * log_g[:, None])
    c_dec = jnp.exp(blk * log_g)
    return dmask, q_dec, k_dec, c_dec


def kernel(x_prompt, x_sample, state_ret, cache_sb_k, cache_sb_v, cache_fox_k, cache_fox_v, cache_fox_logf,
           page_table, p_prompt, p_sample, g_norm, w_in, b_f, g_ret_gn, g_fox_q, g_fox_k, w_out, w_pe, g_pe, w_pg):
    bp, tp, d = x_prompt.shape
    bs, ts, _ = x_sample.shape
    depth = w_in.shape[0]
    n_pages = page_table.shape[1]
    past_len = n_pages * PAGE_SIZE
    d_ple = p_prompt.shape[-1]
    assert tp % ATT_BLOCK == 0 and tp % RET_CHUNK == 0 and ts == SUBLANES

    cos_p, sin_p, cost_p, sint_p = _rotary_tables(jnp.arange(tp, dtype=jnp.int32))
    cos_s, sin_s, _, _ = _rotary_tables(past_len + jnp.arange(ts, dtype=jnp.int32))
    dm_p, qd_p, kd_p, cd_p = _decay_tables(RET_CHUNK)
    dec_p = (dm_p, qd_p[:, :, None], kd_p[:, None, :], cd_p[:, None, None])
    dm_s, qd_s, kd_s, cd_s = _decay_tables(ts)
    dec_s = (dm_s, qd_s[:, :, None], kd_s[:, :, None], cd_s[:, None, None])
    ar = jnp.arange(ATT_BLOCK)
    lmat = (ar[:, None] > ar[None, :]).astype(BF16)
    umat = (ar[:, None] <= ar[None, :]).astype(BF16)
    lane_head = jnp.arange(D_FOX) // HEAD_DIM
    seg = (lane_head[:, None] == lane_head[None, :]).astype(BF16)

    def pool_t(c):
        c = jnp.transpose(c, (0, 1, 3, 4, 2))
        return c.reshape(c.shape[0], c.shape[1], c.shape[2] * c.shape[3], c.shape[4])
    pools = tuple(pool_t(c) for c in (cache_sb_k, cache_sb_v, cache_fox_k, cache_fox_v))
    lf_pool = jnp.transpose(cache_fox_logf, (0, 3, 1, 2))

    n_grp = 4 if n_pages % 4 == 0 else 1
    tm_in = 256 if tp % 256 == 0 else ATT_BLOCK
    n_p, n_s = bp * tp, bs * ts
    tm_out = 512 if n_p % 512 == 0 else ATT_BLOCK

    y_p = x_prompt
    y_s = x_sample.reshape(n_s, d)
    outs = {k: [] for k in ("rs_p", "rs_s", "skp", "svp", "sks", "svs", "fkp", "fvp", "flp", "fks", "fvs", "fls")}
    for i in range(depth):
        w = w_in[i]
        splits = np.cumsum([D_RET] * 4 + [D_SB] * 4 + [D_FOX] * 4).tolist()
        (w_qr, w_kr, w_vr, w_gr, w_qs, w_ks, w_vs, w_gs, w_qf, w_kf, w_vf, w_gf, w_fl) = jnp.split(w, splits, axis=1)
        w_row = jnp.concatenate([w_qr, w_vr, w_gr, w_qs, w_gs, w_qf, w_gf], axis=1).astype(BF16)
        w_flt = jnp.pad(w_fl.T, ((0, SUBLANES - H_FOX), (0, 0))).astype(BF16)
        w_t = [a.T.astype(BF16) for a in (w_kr, w_ks, w_vs, w_kf, w_vf)] + [w_flt]
        padc = lambda a: jnp.pad(a, ((0, 0), (0, PIECE - a.shape[1])))
        w_pieces = jnp.concatenate([padc(a) for a in (w_qr, w_kr, w_vr, w_gr, w_qs, w_ks, w_vs, w_gs,
                                                      w_qf, w_kf, w_vf, w_gf)], axis=1).astype(BF16)
        gn = g_norm[i][None, :]
        b_f8 = jnp.pad(b_f[i], (0, SUBLANES - H_FOX))[:, None]
        gq_row = jnp.tile(g_fox_q[i], H_FOX)[None, :]
        gk_row = jnp.tile(g_fox_k[i], H_FOX)[None, :]
        gk_col = g_fox_k[i][:, None]
        ggn = g_ret_gn[i][None, :]
        w_o = w_out[i].astype(BF16)
        w_op = (w_o[:D_RET], w_o[D_RET:D_RET + D_SB], w_o[D_RET + D_SB:],
                w_pe[i].astype(BF16), g_pe[i][None, :], w_pg[i].astype(BF16))

        (qr, vr, gr, qs, gs, qf, gf, krt, kst, vst, kft, vft, lft) = _inproj_prompt(
            y_p, gn, (cos_p, sin_p, cost_p, sint_p), b_f8, gk_col, [w_row] + w_t, tm_in)
        o_r, s_fin = _ret_prompt(qr, krt, vr, gr, dec_p, ggn)
        o_s = _sb_prompt(qs, kst, vst, gs, lmat)
        o_f = _fox_prompt(qf, kft, vft, lft, gf, gq_row, seg, umat)
        y_p = _outproj(y_p.reshape(n_p, d), o_r.reshape(n_p, D_RET), o_s.reshape(n_p, D_SB),
                       o_f.reshape(n_p, D_FOX), p_prompt[i].reshape(n_p, d_ple), w_op, tm_out).reshape(bp, tp, d)
        outs["rs_p"].append(s_fin)
        outs["skp"].append(kst); outs["svp"].append(vst)
        outs["fkp"].append(kft); outs["fvp"].append(vft); outs["flp"].append(lft[:, :H_FOX])

        proj, flt = _inproj_sample(y_s, gn, w_pieces, w_flt)
        flt_b = jnp.transpose(flt.reshape(SUBLANES, bs, ts), (1, 0, 2))
        o_r, s_new = _ret_sample(proj, state_ret[i], (cos_s, sin_s), dec_s, ggn, ts)
        o_s, o_f, kf_new, lf_new = _decode(proj, flt_b, tuple(p for p in pools), lf_pool, page_table, i,
                                           b_f8, gq_row, gk_row, seg, lmat, ts, n_grp)
        y_s = _outproj(y_s, o_r, o_s, o_f, p_sample[i].reshape(n_s, d_ple), w_op, n_s)
        outs["rs_s"].append(s_new)
        piece = lambda k, wd: proj[:, k * PIECE:k * PIECE + wd]
        outs["sks"].append(piece(5, D_SB)); outs["svs"].append(piece(6, D_SB))
        outs["fks"].append(kf_new); outs["fvs"].append(piece(10, D_FOX))
        outs["fls"].append(lf_new[:, :H_FOX])

    st = lambda k: jnp.stack(outs[k], axis=0)
    kv_p = lambda k, h: jnp.transpose(st(k).reshape(depth, bp, h, HEAD_DIM, tp), (0, 1, 4, 2, 3))
    kv_s = lambda k, h: st(k).reshape(depth, bs, ts, h, HEAD_DIM)
    return (y_p, y_s.reshape(bs, ts, d), st("rs_p"), st("rs_s"),
            kv_p("skp", H_SB), kv_p("svp", H_SB), kv_s("sks", H_SB), kv_s("svs", H_SB),
            kv_p("fkp", H_FOX), kv_p("fvp", H_FOX), jnp.transpose(st("flp"), (0, 1, 3, 2)),
            kv_s("fks", H_FOX), kv_s("fvs", H_FOX), jnp.transpose(st("fls"), (0, 1, 3, 2)))
```

```python
import functools

import jax
import jax.numpy as jnp
import numpy as np
from jax import lax
from jax.experimental import pallas as pl
from jax.experimental.pallas import tpu as pltpu

F32 = jnp.float32
BF16 = jnp.bfloat16

HEAD_DIM = 64
H_RET, H_SB, H_FOX = 6, 5, 5
D_RET, D_SB, D_FOX = H_RET * HEAD_DIM, H_SB * HEAD_DIM, H_FOX * HEAD_DIM
PAGE_SIZE = 128
RET_CHUNK = 128
ATT_BLOCK = 128
LANES = 128
SUBLANES = 8
ROPE_BASE = 10000.0
EPS = 1e-6
QK_SCALE = HEAD_DIM ** -0.5
SB_DEAD = -104.0
NEG_BIG = -1e30
PIECE = 384
VMEM_LIMIT = 48 * 1024 * 1024


def _dot(a, b):
    return jnp.dot(a, b, preferred_element_type=F32)


def _dot_nt(a, b):
    return lax.dot_general(a, b, (((1,), (1,)), ((), ())), preferred_element_type=F32)


def _dot_tn(a, b):
    return lax.dot_general(a, b, (((0,), (0,)), ((), ())), preferred_element_type=F32)


def _split2(x):
    hi = x.astype(BF16)
    lo = (x - hi.astype(F32)).astype(BF16)
    return hi, lo


def _split3(x):
    p1 = x.astype(BF16)
    r1 = x - p1.astype(F32)
    p2 = r1.astype(BF16)
    p3 = (r1 - p2.astype(F32)).astype(BF16)
    return p1, p2, p3


def _dot_exact2(x, m):
    hi, lo = _split2(x)
    return _dot(hi, m) + _dot(lo, m)


def _dot_exact3(x, m):
    p1, p2, p3 = _split3(x)
    return _dot(p1, m) + _dot(p2, m) + _dot(p3, m)


def _softplus(z):
    return jnp.maximum(z, 0.0) + jnp.log1p(jnp.exp(-jnp.abs(z)))


def _log_sigmoid(x):
    return jnp.minimum(x, 0.0) - jnp.log1p(jnp.exp(-jnp.abs(x)))


def _silu(g):
    return g * jax.nn.sigmoid(g)


def _rotary_rows(x, cos, sin_signed):
    lane = lax.broadcasted_iota(jnp.int32, (x.shape[0], LANES), 1)
    first_half = (lane % HEAD_DIM) < (HEAD_DIM // 2)
    out = []
    for c in range(x.shape[1] // LANES):
        xs = x[:, c * LANES:(c + 1) * LANES]
        partner = jnp.where(first_half,
                            pltpu.roll(xs, LANES - HEAD_DIM // 2, axis=1),
                            pltpu.roll(xs, HEAD_DIM // 2, axis=1))
        out.append(xs * cos + partner * sin_signed)
    return jnp.concatenate(out, axis=1)


def _inproj_prompt_kernel(x_ref, g_ref, cos_ref, sin_ref, cost_ref, sint_ref, bf_ref, gk_ref,
                          wrow_ref, wkr_ref, wks_ref, wvs_ref, wkf_ref, wvf_ref, wfl_ref,
                          qr_ref, vr_ref, gr_ref, qs_ref, gs_ref, qf_ref, gf_ref,
                          krt_ref, kst_ref, vst_ref, kft_ref, vft_ref, lft_ref):
    x = x_ref[...]
    ms = jnp.mean(x * x, axis=-1, keepdims=True)
    h = (x * lax.rsqrt(ms + EPS) * g_ref[...]).astype(BF16)

    row = _dot(h, wrow_ref[...])
    qr_ref[...] = _rotary_rows(row[:, 0:D_RET], cos_ref[...], sin_ref[...])
    vr_ref[...] = row[:, D_RET:2 * D_RET]
    gr_ref[...] = row[:, 2 * D_RET:3 * D_RET]
    o = 3 * D_RET
    qs_ref[...] = row[:, o:o + D_SB]
    gs_ref[...] = row[:, o + D_SB:o + 2 * D_SB]
    qf_ref[...] = row[:, o + 2 * D_SB:o + 3 * D_SB]
    gf_ref[...] = row[:, o + 3 * D_SB:o + 4 * D_SB]

    half = HEAD_DIM // 2
    krt = _dot_nt(wkr_ref[...], h)
    cost, sint = cost_ref[...], sint_ref[...]
    for hd in range(H_RET):
        x1 = krt[hd * HEAD_DIM:hd * HEAD_DIM + half]
        x2 = krt[hd * HEAD_DIM + half:(hd + 1) * HEAD_DIM]
        krt_ref[hd * HEAD_DIM:hd * HEAD_DIM + half, :] = (x1 * cost - x2 * sint) * QK_SCALE
        krt_ref[hd * HEAD_DIM + half:(hd + 1) * HEAD_DIM, :] = (x1 * sint + x2 * cost) * QK_SCALE

    kst_ref[...] = _dot_nt(wks_ref[...], h)
    vst_ref[...] = _dot_nt(wvs_ref[...], h)
    vft_ref[...] = _dot_nt(wvf_ref[...], h)
    kft = _dot_nt(wkf_ref[...], h)
    gk = gk_ref[...]
    for hd in range(H_FOX):
        xh = kft[hd * HEAD_DIM:(hd + 1) * HEAD_DIM]
        msh = jnp.mean(xh * xh, axis=0, keepdims=True)
        kft_ref[hd * HEAD_DIM:(hd + 1) * HEAD_DIM, :] = xh * lax.rsqrt(msh + EPS) * gk
    lft_ref[...] = _log_sigmoid(_dot_nt(wfl_ref[...], h) + bf_ref[...])


def _inproj_prompt(x, g_norm, tabs, b_f8, gk_col, w, tm):
    bsz, t, d = x.shape
    n_row = 3 * D_RET + 4 * D_SB
    grid = (bsz, t // tm)
    row_spec = lambda n: pl.BlockSpec((None, tm, n), lambda b, i: (b, i, 0))
    col_spec = lambda n: pl.BlockSpec((None, n, tm), lambda b, i: (b, 0, i))
    full = lambda a: pl.BlockSpec(a.shape, lambda b, i: (0,) * a.ndim)
    cos, sin, cost, sint = tabs
    in_specs = [
        row_spec(d), full(g_norm),
        pl.BlockSpec((tm, LANES), lambda b, i: (i, 0)), pl.BlockSpec((tm, LANES), lambda b, i: (i, 0)),
        pl.BlockSpec((HEAD_DIM // 2, tm), lambda b, i: (0, i)), pl.BlockSpec((HEAD_DIM // 2, tm), lambda b, i: (0, i)),
        full(b_f8), full(gk_col),
    ] + [full(a) for a in w]
    rs = lambda n: jax.ShapeDtypeStruct((bsz, t, n), F32)
    cs = lambda n: jax.ShapeDtypeStruct((bsz, n, t), F32)
    out_shape = [rs(D_RET), rs(D_RET), rs(D_RET), rs(D_SB), rs(D_SB), rs(D_FOX), rs(D_FOX),
                 cs(D_RET), cs(D_SB), cs(D_SB), cs(D_FOX), cs(D_FOX), cs(SUBLANES)]
    out_specs = [row_spec(D_RET)] * 3 + [row_spec(D_SB)] * 4 + \
                [col_spec(D_RET)] + [col_spec(D_SB)] * 4 + [col_spec(SUBLANES)]
    return pl.pallas_call(
        _inproj_prompt_kernel, grid=grid, in_specs=in_specs, out_specs=out_specs, out_shape=out_shape,
        compiler_params=pltpu.CompilerParams(dimension_semantics=("parallel", "parallel"),
                                             vmem_limit_bytes=VMEM_LIMIT),
        name="inproj_prompt",
    )(x, g_norm, cos, sin, cost, sint, b_f8, gk_col, *w)


def _ret_prompt_kernel(qr_ref, krt_ref, vr_ref, gr_ref, dmask_ref, qdec_ref, kdec_ref, cdec_ref, ggn_ref,
                       o_ref, sfin_ref, s_scr):
    c = pl.program_id(1)

    @pl.when(c == 0)
    def _():
        s_scr[...] = jnp.zeros_like(s_scr)

    for hd in range(H_RET):
        sl = slice(hd * HEAD_DIM, (hd + 1) * HEAD_DIM)
        q = qr_ref[:, sl].astype(BF16)
        kt = krt_ref[sl, :]
        v = vr_ref[:, sl].astype(BF16)
        sc = _dot(q, kt.astype(BF16)) * dmask_ref[hd]
        inner = _dot(sc.astype(BF16), v)
        s_old = s_scr[hd]
        cross = _dot(q, s_old.astype(BF16)) * qdec_ref[hd]
        o = inner + cross
        s_scr[hd] = s_old * cdec_ref[hd] + _dot((kt * kdec_ref[hd]).astype(BF16), v)
        mu = jnp.mean(o, axis=-1, keepdims=True)
        dev = o - mu
        var = jnp.mean(dev * dev, axis=-1, keepdims=True)
        y = dev * lax.rsqrt(var + EPS) * ggn_ref[:, sl]
        o_ref[:, sl] = y * _silu(gr_ref[:, sl])

    @pl.when(c == pl.num_programs(1) - 1)
    def _():
        sfin_ref[...] = s_scr[...]


def _ret_prompt(qr, krt, vr, gr, dec, ggn):
    bsz, t, _ = qr.shape
    blk = RET_CHUNK
    dmask, qdec, kdec, cdec = dec
    row = pl.BlockSpec((None, blk, D_RET), lambda b, c: (b, c, 0))
    full = lambda a: pl.BlockSpec(a.shape, lambda b, c: (0,) * a.ndim)
    return pl.pallas_call(
        _ret_prompt_kernel, grid=(bsz, t // blk),
        in_specs=[row, pl.BlockSpec((None, D_RET, blk), lambda b, c: (b, 0, c)), row, row,
                  full(dmask), full(qdec), full(kdec), full(cdec), full(ggn)],
        out_specs=[row, pl.BlockSpec((None, H_RET, HEAD_DIM, HEAD_DIM), lambda b, c: (b, 0, 0, 0))],
        out_shape=[jax.ShapeDtypeStruct((bsz, t, D_RET), F32),
                   jax.ShapeDtypeStruct((bsz, H_RET, HEAD_DIM, HEAD_DIM), F32)],
        scratch_shapes=[pltpu.VMEM((H_RET, HEAD_DIM, HEAD_DIM), F32)],
        compiler_params=pltpu.CompilerParams(dimension_semantics=("parallel", "arbitrary"),
                                             vmem_limit_bytes=VMEM_LIMIT),
        name="ret_prompt",
    )(qr, krt, vr, gr, dmask, qdec, kdec, cdec, ggn)


def _sb_prompt_kernel(q_ref, kt_ref, vt_ref, g_ref, lmat_ref, o_ref, q_scr, car_scr, acc_scr):
    i = pl.program_id(1)
    tq = q_ref.shape[0]
    rowi = lax.broadcasted_iota(jnp.int32, (tq, tq), 0)
    coli = lax.broadcasted_iota(jnp.int32, (tq, tq), 1)
    diag_mask = coli < rowi
    for hd in range(H_SB):
        sl = slice(hd * HEAD_DIM, (hd + 1) * HEAD_DIM)
        q_scr[hd] = (q_ref[:, sl] * QK_SCALE).astype(BF16)
    car_scr[...] = jnp.zeros_like(car_scr)
    acc_scr[...] = jnp.zeros_like(acc_scr)

    def block(j, mask):
        off = pl.multiple_of(j * tq, tq)
        lmat = lmat_ref[...]
        cmax = None
        for hd in range(H_SB):
            sl = slice(hd * HEAD_DIM, (hd + 1) * HEAD_DIM)
            kt = kt_ref[sl, pl.ds(off, tq)].astype(BF16)
            vt = vt_ref[sl, pl.ds(off, tq)].astype(BF16)
            z = _dot(q_scr[hd], kt)
            sp = _softplus(z)
            ln1m = -sp
            ls = z - sp
            if mask is not None:
                ln1m = jnp.where(mask, ln1m, 0.0)
            carry = car_scr[hd]
            later = _dot_exact2(ln1m, lmat)
            a = jnp.exp(ls + later + carry)
            if mask is not None:
                a = jnp.where(mask, a, 0.0)
            acc_scr[hd] += _dot_nt(a.astype(BF16), vt)
            carry = carry + (later[:, 0:1] + ln1m[:, 0:1])
            car_scr[hd] = carry
            hmax = jnp.max(carry)
            cmax = hmax if cmax is None else jnp.maximum(cmax, hmax)
        return cmax

    cmax = block(i, diag_mask)

    def cond(st):
        j, cmax = st
        return jnp.logical_and(j >= 0, cmax > SB_DEAD)

    def body(st):
        j, _ = st
        return j - 1, block(j, None)

    lax.while_loop(cond, body, (i - 1, cmax))
    for hd in range(H_SB):
        sl = slice(hd * HEAD_DIM, (hd + 1) * HEAD_DIM)
        o_ref[:, sl] = acc_scr[hd] * _silu(g_ref[:, sl])


def _sb_prompt(qs, kst, vst, gs, lmat, tq):
    bsz, t, _ = qs.shape
    row = pl.BlockSpec((None, tq, D_SB), lambda b, i: (b, i, 0))
    seq = pl.BlockSpec((None, D_SB, t), lambda b, i: (b, 0, 0))
    return pl.pallas_call(
        _sb_prompt_kernel, grid=(bsz, t // tq),
        in_specs=[row, seq, seq, row, pl.BlockSpec(lmat.shape, lambda b, i: (0, 0))],
        out_specs=row, out_shape=jax.ShapeDtypeStruct((bsz, t, D_SB), F32),
        scratch_shapes=[pltpu.VMEM((H_SB, tq, HEAD_DIM), BF16), pltpu.VMEM((H_SB, tq, 1), F32),
                        pltpu.VMEM((H_SB, tq, HEAD_DIM), F32)],
        compiler_params=pltpu.CompilerParams(dimension_semantics=("parallel", "arbitrary"),
                                             vmem_limit_bytes=VMEM_LIMIT),
        name="sb_prompt",
    )(qs, kst, vst, gs, lmat)


def _fox_prompt_kernel(q_ref, kt_ref, vt_ref, lft_ref, g_ref, gq_ref, seg_ref, umat_ref, o_ref,
                       c_scr, q_scr, m_scr, acc_scr):
    i = pl.program_id(1)
    tq = q_ref.shape[0]
    t = kt_ref.shape[1]
    cw = umat_ref.shape[0]

    @pl.when(i == 0)
    def _():
        umat = umat_ref[...]
        carry = jnp.zeros((SUBLANES, 1), F32)
        for c in range(t // cw):
            cs = _dot_exact3(lft_ref[:, c * cw:(c + 1) * cw], umat) + carry
            c_scr[:, c * cw:(c + 1) * cw] = cs
            carry = cs[:, cw - 1:cw]

    qraw = q_ref[...]
    ssq = _dot_exact2(qraw * qraw, seg_ref[...])
    qn = qraw * lax.rsqrt(ssq * (1.0 / HEAD_DIM) + EPS) * gq_ref[...] * QK_SCALE
    for hd in range(H_FOX):
        q_scr[hd] = qn[:, hd * HEAD_DIM:(hd + 1) * HEAD_DIM].astype(BF16)
    m_scr[...] = jnp.full_like(m_scr, NEG_BIG)
    acc_scr[...] = jnp.zeros_like(acc_scr)
    rowi = lax.broadcasted_iota(jnp.int32, (tq, tq), 0)
    coli = lax.broadcasted_iota(jnp.int32, (tq, tq), 1)
    diag_mask = coli <= rowi

    ones_rows = jnp.ones((HEAD_DIM, tq), BF16)

    def block(j, mask):
        off = pl.multiple_of(j * tq, tq)
        for hd in range(H_FOX):
            sl = slice(hd * HEAD_DIM, (hd + 1) * HEAD_DIM)
            kt = kt_ref[sl, pl.ds(off, tq)].astype(BF16)
            vt = jnp.concatenate([vt_ref[sl, pl.ds(off, tq)].astype(BF16), ones_rows], axis=0)
            s = _dot(q_scr[hd], kt) - c_scr[hd:hd + 1, pl.ds(off, tq)]
            if mask is not None:
                s = jnp.where(mask, s, NEG_BIG)
            m_old = m_scr[hd]
            m_new = jnp.maximum(m_old, jnp.max(s, axis=-1, keepdims=True))
            p = jnp.exp(s - m_new)
            acc_scr[hd] = jnp.exp(m_old - m_new) * acc_scr[hd] + _dot_nt(p.astype(BF16), vt)
            m_scr[hd] = m_new

    def body(j, carry):
        block(j, None)
        return carry

    lax.fori_loop(0, i, body, 0)
    block(i, diag_mask)
    for hd in range(H_FOX):
        sl = slice(hd * HEAD_DIM, (hd + 1) * HEAD_DIM)
        acc = acc_scr[hd]
        o_ref[:, sl] = acc[:, :HEAD_DIM] / acc[:, HEAD_DIM:HEAD_DIM + 1] * _silu(g_ref[:, sl])


def _fox_prompt(qf, kft, vft, lft, gf, gq_row, seg, umat, tq):
    bsz, t, _ = qf.shape
    row = pl.BlockSpec((None, tq, D_FOX), lambda b, i: (b, i, 0))
    seq = pl.BlockSpec((None, D_FOX, t), lambda b, i: (b, 0, 0))
    full = lambda a: pl.BlockSpec(a.shape, lambda b, i: (0,) * a.ndim)
    return pl.pallas_call(
        _fox_prompt_kernel, grid=(bsz, t // tq),
        in_specs=[row, seq, seq, pl.BlockSpec((None, SUBLANES, t), lambda b, i: (b, 0, 0)), row,
                  full(gq_row), full(seg), full(umat)],
        out_specs=row, out_shape=jax.ShapeDtypeStruct((bsz, t, D_FOX), F32),
        scratch_shapes=[pltpu.VMEM((SUBLANES, t), F32), pltpu.VMEM((H_FOX, tq, HEAD_DIM), BF16),
                        pltpu.VMEM((H_FOX, tq, 1), F32), pltpu.VMEM((H_FOX, tq, 2 * HEAD_DIM), F32)],
        compiler_params=pltpu.CompilerParams(dimension_semantics=("parallel", "arbitrary"),
                                             vmem_limit_bytes=VMEM_LIMIT),
        name="fox_prompt",
    )(qf, kft, vft, lft, gf, gq_row, seg, umat)


def _outproj_kernel(y_ref, or_ref, os_ref, of_ref, p_ref, wor_ref, wos_ref, wof_ref, wpe_ref, gpe_ref, wpg_ref,
                    o_ref):
    m = (_dot(or_ref[...].astype(BF16), wor_ref[...]) + _dot(os_ref[...].astype(BF16), wos_ref[...])
         + _dot(of_ref[...].astype(BF16), wof_ref[...]))
    y1 = y_ref[...] + m
    ms = jnp.mean(y1 * y1, axis=-1, keepdims=True)
    n = (y1 * lax.rsqrt(ms + EPS) * gpe_ref[...]).astype(BF16)
    gate = jax.nn.sigmoid(_dot(n, wpg_ref[...]))
    pe = _dot(p_ref[...].astype(BF16), wpe_ref[...])
    o_ref[...] = y1 + pe * gate


def _outproj(y, o_r, o_s, o_f, p, w, tm):
    n, d = y.shape
    row = lambda a: pl.BlockSpec((tm, a.shape[1]), lambda i: (i, 0))
    full = lambda a: pl.BlockSpec(a.shape, lambda i: (0,) * a.ndim)
    return pl.pallas_call(
        _outproj_kernel, grid=(n // tm,),
        in_specs=[row(y), row(o_r), row(o_s), row(o_f), row(p)] + [full(a) for a in w],
        out_specs=row(y), out_shape=jax.ShapeDtypeStruct((n, d), F32),
        compiler_params=pltpu.CompilerParams(dimension_semantics=("parallel",), vmem_limit_bytes=VMEM_LIMIT),
        name="outproj",
    )(y, o_r, o_s, o_f, p, *w)


def _inproj_sample_kernel(x_ref, g_ref, w_ref, wfl_ref, o_ref, flt_ref):
    x = x_ref[...]
    ms = jnp.mean(x * x, axis=-1, keepdims=True)
    h = (x * lax.rsqrt(ms + EPS) * g_ref[...]).astype(BF16)
    o_ref[...] = _dot(h, w_ref[...])
    flt_ref[...] = _dot_nt(wfl_ref[...], h)


def _inproj_sample(x, g_norm, w_pieces, wfl):
    n, d = x.shape
    ncol = w_pieces.shape[1]
    full = lambda a: pl.BlockSpec(a.shape, lambda i: (0,) * a.ndim)
    return pl.pallas_call(
        _inproj_sample_kernel, grid=(1,),
        in_specs=[full(x), full(g_norm), full(w_pieces), full(wfl)],
        out_specs=[pl.BlockSpec((n, ncol), lambda i: (0, 0)), pl.BlockSpec((SUBLANES, n), lambda i: (0, 0))],
        out_shape=[jax.ShapeDtypeStruct((n, ncol), F32), jax.ShapeDtypeStruct((SUBLANES, n), F32)],
        compiler_params=pltpu.CompilerParams(dimension_semantics=("arbitrary",), vmem_limit_bytes=VMEM_LIMIT),
        name="inproj_sample",
    )(x, g_norm, w_pieces, wfl)


def _ret_sample_kernel(q_ref, k_ref, v_ref, g_ref, s0_ref, cos_ref, sin_ref, dmask_ref, qdec_ref, kdec_ref,
                       cdec_ref, ggn_ref, o_ref, s_ref):
    cos, sin = cos_ref[...], sin_ref[...]
    qr = _rotary_rows(q_ref[...], cos, sin)
    kr = _rotary_rows(k_ref[...], cos, sin) * QK_SCALE
    for hd in range(H_RET):
        sl = slice(hd * HEAD_DIM, (hd + 1) * HEAD_DIM)
        q = qr[:, sl].astype(BF16)
        k = kr[:, sl]
        v = v_ref[:, sl].astype(BF16)
        sc = _dot_nt(q, k.astype(BF16)) * dmask_ref[hd]
        inner = _dot(sc.astype(BF16), v)
        s_old = s0_ref[hd]
        cross = _dot(q, s_old.astype(BF16)) * qdec_ref[hd]
        o = inner + cross
        s_ref[hd] = s_old * cdec_ref[hd] + _dot_tn((k * kdec_ref[hd]).astype(BF16), v)
        mu = jnp.mean(o, axis=-1, keepdims=True)
        dev = o - mu
        var = jnp.mean(dev * dev, axis=-1, keepdims=True)
        y = dev * lax.rsqrt(var + EPS) * ggn_ref[:, sl]
        o_ref[:, sl] = y * _silu(g_ref[:, sl])


def _ret_sample(proj, s0, tabs, dec, ggn, ts):
    n = proj.shape[0]
    bsz = n // ts
    cos, sin = tabs
    dmask, qdec, kdec, cdec = dec
    piece = lambda k: pl.BlockSpec((ts, PIECE), lambda b, k=k: (b, k))
    full = lambda a: pl.BlockSpec(a.shape, lambda b: (0,) * a.ndim)
    st = pl.BlockSpec((None, H_RET, HEAD_DIM, HEAD_DIM), lambda b: (b, 0, 0, 0))
    return pl.pallas_call(
        _ret_sample_kernel, grid=(bsz,),
        in_specs=[piece(0), piece(1), piece(2), piece(3), st, full(cos), full(sin),
                  full(dmask), full(qdec), full(kdec), full(cdec), full(ggn)],
        out_specs=[pl.BlockSpec((ts, D_RET), lambda b: (b, 0)), st],
        out_shape=[jax.ShapeDtypeStruct((n, D_RET), F32),
                   jax.ShapeDtypeStruct((bsz, H_RET, HEAD_DIM, HEAD_DIM), F32)],
        compiler_params=pltpu.CompilerParams(dimension_semantics=("parallel",), vmem_limit_bytes=VMEM_LIMIT),
        name="ret_sample",
    )(proj, proj, proj, proj, s0, cos, sin, dmask, qdec, kdec, cdec, ggn)


def _decode_kernel(pt_ref, qs_ref, ks_ref, vs_ref, gs_ref, qf_ref, kf_ref, vf_ref, gf_ref, flt_ref,
                   bf_ref, gq_ref, gk_ref, seg_ref, lmat_ref, *rest, n_grp, ts):
    pages = rest[:5 * n_grp]
    os_ref, of_ref, kfn_ref, lfn_ref = rest[5 * n_grp:5 * n_grp + 4]
    (qbs_scr, qbf_scr, accs_scr, cars_scr, accf_scr, mf_scr, lf_scr, rcar_scr) = rest[5 * n_grp + 4:]
    b = pl.program_id(0)
    s = pl.program_id(1)
    n_steps = pl.num_programs(1)
    rows = H_SB * ts
    lmat = lmat_ref[...]

    def head_rows(x):
        r = lax.broadcasted_iota(jnp.int32, (rows, D_SB), 0)
        c = lax.broadcasted_iota(jnp.int32, (rows, D_SB), 1)
        tiled = jnp.concatenate([x] * H_SB, axis=0)
        return jnp.where(r // ts == c // HEAD_DIM, tiled, 0.0)

    def head_cols(x):
        c = lax.broadcasted_iota(jnp.int32, (ts, D_SB), 1)
        out = jnp.zeros((ts, D_SB), F32)
        for hd in range(H_SB):
            out = out + jnp.where(c // HEAD_DIM == hd, x[hd * ts:(hd + 1) * ts], 0.0)
        return out

    def rep_heads(x):
        return jnp.concatenate([jnp.broadcast_to(x[hd:hd + 1], (ts, x.shape[1])) for hd in range(H_SB)], axis=0)

    def sb_blocks(zs, v_dots):
        car = cars_scr[...]
        acc = accs_scr[...]
        for (z, mask), v_dot in zip(zs, v_dots):
            sp = _softplus(z)
            ln1m = -sp
            ls = z - sp
            if mask is not None:
                ln1m = jnp.where(mask, ln1m, 0.0)
            a = jnp.exp(ls + _dot_exact2(ln1m, lmat) + car)
            if mask is not None:
                a = jnp.where(mask, a, 0.0)
            acc = acc + v_dot(a.astype(BF16))
            car = car + jnp.sum(ln1m, axis=-1, keepdims=True)
        accs_scr[...] = acc
        cars_scr[...] = car

    def fox_blocks(scs, v_dots):
        scs = [sc if mask is None else jnp.where(mask, sc, NEG_BIG) for sc, mask in scs]
        m_old = mf_scr[...]
        m_new = m_old
        for sc in scs:
            m_new = jnp.maximum(m_new, jnp.max(sc, axis=-1, keepdims=True))
        alpha = jnp.exp(m_old - m_new)
        l = alpha * lf_scr[...]
        acc = alpha * accf_scr[...]
        for sc, v_dot in zip(scs, v_dots):
            p = jnp.exp(sc - m_new)
            l = l + jnp.sum(p, axis=-1, keepdims=True)
            acc = acc + v_dot(p.astype(BF16))
        lf_scr[...] = l
        accf_scr[...] = acc
        mf_scr[...] = m_new

    @pl.when(s == 0)
    def _():
        rowi = lax.broadcasted_iota(jnp.int32, (rows, PAGE_SIZE), 0)
        coli = lax.broadcasted_iota(jnp.int32, (rows, PAGE_SIZE), 1)
        tpos = rowi % ts
        pad = jnp.zeros((PAGE_SIZE - ts, D_SB), F32)
        qbs = head_rows(qs_ref[:, :D_SB] * QK_SCALE).astype(BF16)
        qbs_scr[...] = qbs
        accs_scr[...] = jnp.zeros_like(accs_scr)
        cars_scr[...] = jnp.zeros_like(cars_scr)
        k_new = jnp.concatenate([ks_ref[:, :D_SB], pad], axis=0).astype(BF16)
        v_new = jnp.concatenate([vs_ref[:, :D_SB], pad], axis=0).astype(BF16)
        sb_blocks([(_dot_nt(qbs, k_new), coli < tpos)], [lambda a: _dot(a, v_new)])
        seg = seg_ref[...]
        qraw = qf_ref[:, :D_FOX]
        qn = qraw * lax.rsqrt(_dot_exact2(qraw * qraw, seg) * (1.0 / HEAD_DIM) + EPS) * gq_ref[...] * QK_SCALE
        qbf = head_rows(qn).astype(BF16)
        qbf_scr[...] = qbf
        kraw = kf_ref[:, :D_FOX]
        kn = kraw * lax.rsqrt(_dot_exact2(kraw * kraw, seg) * (1.0 / HEAD_DIM) + EPS) * gk_ref[...]
        kfn_ref[...] = kn
        lf_new = _log_sigmoid(flt_ref[...] + bf_ref[...])
        lfn_ref[...] = lf_new
        lf_pad = jnp.concatenate([lf_new, jnp.zeros((SUBLANES, PAGE_SIZE - ts), F32)], axis=1)
        lf_rows = rep_heads(lf_pad)
        cn = jnp.sum(lf_rows, axis=-1, keepdims=True) - _dot_exact3(lf_rows, lmat)
        accf_scr[...] = jnp.zeros_like(accf_scr)
        lf_scr[...] = jnp.zeros_like(lf_scr)
        mf_scr[...] = jnp.full_like(mf_scr, NEG_BIG)
        rcar_scr[...] = jnp.zeros_like(rcar_scr)
        kn_pad = jnp.concatenate([kn, pad], axis=0).astype(BF16)
        vf_new = jnp.concatenate([vf_ref[:, :D_FOX], pad], axis=0).astype(BF16)
        fox_blocks([(_dot_nt(qbf, kn_pad) - cn, coli <= tpos)], [lambda p: _dot(p, vf_new)])

    order = list(range(n_grp - 1, -1, -1))
    qbf = qbf_scr[...]
    rcar = rcar_scr[...]
    scs, f_dots = [], []
    for gi in order:
        fkt_ref, fvt_ref, lfp_ref = pages[5 * gi + 2:5 * gi + 5]
        page = pt_ref[b, (n_steps - 1 - s) * n_grp + gi]
        lf_pg = lfp_ref[:, pl.ds(page % SUBLANES, 1), :]
        lf_rows = jnp.concatenate([jnp.broadcast_to(lf_pg[hd], (ts, PAGE_SIZE)) for hd in range(H_FOX)], axis=0)
        bias = _dot_exact3(lf_rows, lmat) + rcar
        rcar = rcar + jnp.sum(lf_rows, axis=-1, keepdims=True)
        scs.append((_dot(qbf, fkt_ref[...].astype(BF16)) + bias, None))
        f_dots.append(lambda p, r=fvt_ref: _dot_nt(p, r[...].astype(BF16)))
    rcar_scr[...] = rcar
    fox_blocks(scs, f_dots)

    @pl.when(jnp.max(cars_scr[...]) > SB_DEAD)
    def _():
        qbs = qbs_scr[...]
        zs, s_dots = [], []
        for gi in order:
            skt_ref, svt_ref = pages[5 * gi:5 * gi + 2]
            zs.append((_dot(qbs, skt_ref[...].astype(BF16)), None))
            s_dots.append(lambda a, r=svt_ref: _dot_nt(a, r[...].astype(BF16)))
        sb_blocks(zs, s_dots)

    @pl.when(s == n_steps - 1)
    def _():
        os_ref[...] = head_cols(accs_scr[...]) * _silu(gs_ref[:, :D_SB])
        of_ref[...] = head_cols(accf_scr[...] / lf_scr[...]) * _silu(gf_ref[:, :D_FOX])


def _decode(proj, flt, pools, lf_pool, page_table, layer, b_f8, gq_row, gk_row, seg, lmat, ts, n_grp):
    n = proj.shape[0]
    bsz, n_pages = page_table.shape
    n_steps = n_pages // n_grp
    rows = H_SB * ts
    piece = lambda k: pl.BlockSpec((ts, PIECE), lambda b, s, pt, k=k: (b, k))
    full = lambda a: pl.BlockSpec(a.shape, lambda b, s, pt: (0,) * a.ndim)

    def page_spec(gi):
        return pl.BlockSpec((None, None, D_SB, PAGE_SIZE),
                            lambda b, s, pt, gi=gi: (layer, pt[b, (n_steps - 1 - s) * n_grp + gi], 0, 0))

    def lf_spec(gi):
        return pl.BlockSpec((None, H_FOX, SUBLANES, PAGE_SIZE),
                            lambda b, s, pt, gi=gi: (layer, 0, pt[b, (n_steps - 1 - s) * n_grp + gi] // SUBLANES, 0))

    page_specs, page_args = [], []
    for gi in range(n_grp):
        page_specs += [page_spec(gi)] * 4 + [lf_spec(gi)]
        page_args += list(pools) + [lf_pool]
    out_row = pl.BlockSpec((ts, D_SB), lambda b, s, pt: (b, 0))
    grid_spec = pltpu.PrefetchScalarGridSpec(
        num_scalar_prefetch=1, grid=(bsz, n_steps),
        in_specs=[piece(4), piece(5), piece(6), piece(7), piece(8), piece(9), piece(10), piece(11),
                  pl.BlockSpec((None, SUBLANES, ts), lambda b, s, pt: (b, 0, 0)),
                  full(b_f8), full(gq_row), full(gk_row), full(seg), full(lmat)] + page_specs,
        out_specs=[out_row, out_row, out_row, pl.BlockSpec((None, SUBLANES, ts), lambda b, s, pt: (b, 0, 0))],
        scratch_shapes=[pltpu.VMEM((rows, D_SB), BF16), pltpu.VMEM((rows, D_FOX), BF16),
                        pltpu.VMEM((rows, D_SB), F32), pltpu.VMEM((rows, 1), F32),
                        pltpu.VMEM((rows, D_FOX), F32), pltpu.VMEM((rows, 1), F32),
                        pltpu.VMEM((rows, 1), F32), pltpu.VMEM((rows, 1), F32)],
    )
    return pl.pallas_call(
        functools.partial(_decode_kernel, n_grp=n_grp, ts=ts), grid_spec=grid_spec,
        out_shape=[jax.ShapeDtypeStruct((n, D_SB), F32), jax.ShapeDtypeStruct((n, D_FOX), F32),
                   jax.ShapeDtypeStruct((n, D_FOX), F32), jax.ShapeDtypeStruct((bsz, SUBLANES, ts), F32)],
        compiler_params=pltpu.CompilerParams(dimension_semantics=("parallel", "arbitrary"),
                                             vmem_limit_bytes=VMEM_LIMIT),
        name="decode_attn",
    )(page_table, *([proj] * 8), flt, b_f8, gq_row, gk_row, seg, lmat, *page_args)


def _rotary_tables(pos):
    half = HEAD_DIM // 2
    inv = ROPE_BASE ** (-jnp.arange(half, dtype=F32) / half)
    ang = pos.astype(F32)[:, None] * inv[None, :]
    cos, sin = jnp.cos(ang), jnp.sin(ang)
    reps = LANES // HEAD_DIM
    cos_row = jnp.tile(jnp.concatenate([cos, cos], axis=1), (1, reps))
    sin_row = jnp.tile(jnp.concatenate([-sin, sin], axis=1), (1, reps))
    return cos_row, sin_row, cos.T, sin.T


def _decay_tables(blk):
    log_g = jnp.log1p(-jnp.exp2(-5.0 - jnp.arange(H_RET, dtype=F32)))
    i = jnp.arange(blk, dtype=F32)
    diff = i[:, None] - i[None, :]
    dmask = jnp.where(diff[None] >= 0, jnp.exp(jnp.maximum(diff, 0.0)[None] * log_g[:, None, None]), 0.0)
    q_dec = jnp.exp((i + 1.0)[None, :] * log_g[:, None])
    k_dec = jnp.exp((blk - 1.0 - i)[None, :] * log_g[:, None])
    c_dec = jnp.exp(blk * log_g)
    return dmask, q_dec, k_dec, c_dec


def kernel(x_prompt, x_sample, state_ret, cache_sb_k, cache_sb_v, cache_fox_k, cache_fox_v, cache_fox_logf,
           page_table, p_prompt, p_sample, g_norm, w_in, b_f, g_ret_gn, g_fox_q, g_fox_k, w_out, w_pe, g_pe, w_pg):
    bp, tp, d = x_prompt.shape
    bs, ts, _ = x_sample.shape
    depth = w_in.shape[0]
    n_pages = page_table.shape[1]
    past_len = n_pages * PAGE_SIZE
    d_ple = p_prompt.shape[-1]
    assert tp % ATT_BLOCK == 0 and tp % RET_CHUNK == 0 and ts == SUBLANES

    cos_p, sin_p, cost_p, sint_p = _rotary_tables(jnp.arange(tp, dtype=jnp.int32))
    cos_s, sin_s, _, _ = _rotary_tables(past_len + jnp.arange(ts, dtype=jnp.int32))
    dm_p, qd_p, kd_p, cd_p = _decay_tables(RET_CHUNK)
    dec_p = (dm_p, qd_p[:, :, None], kd_p[:, None, :], cd_p[:, None, None])
    dm_s, qd_s, kd_s, cd_s = _decay_tables(ts)
    dec_s = (dm_s, qd_s[:, :, None], kd_s[:, :, None], cd_s[:, None, None])
    tq_att = 2 * ATT_BLOCK if tp % (2 * ATT_BLOCK) == 0 else ATT_BLOCK
    later = lambda n: (jnp.arange(n)[:, None] > jnp.arange(n)[None, :]).astype(BF16)
    lmat_page, lmat_att = later(PAGE_SIZE), later(tq_att)
    ar = jnp.arange(tq_att)
    umat = (ar[:, None] <= ar[None, :]).astype(BF16)
    lane_head = jnp.arange(D_FOX) // HEAD_DIM
    seg = (lane_head[:, None] == lane_head[None, :]).astype(BF16)

    def pool_t(c):
        c = jnp.transpose(c, (0, 1, 3, 4, 2))
        return c.reshape(c.shape[0], c.shape[1], c.shape[2] * c.shape[3], c.shape[4])
    pools = tuple(pool_t(c) for c in (cache_sb_k, cache_sb_v, cache_fox_k, cache_fox_v))
    lf_pool = jnp.transpose(cache_fox_logf, (0, 3, 1, 2))

    n_grp = next(g for g in (8, 4, 2, 1) if n_pages % g == 0)
    tm_in = 256 if tp % 256 == 0 else ATT_BLOCK
    n_p, n_s = bp * tp, bs * ts
    tm_out = 512 if n_p % 512 == 0 else ATT_BLOCK

    y_p = x_prompt
    y_s = x_sample.reshape(n_s, d)
    outs = {k: [] for k in ("rs_p", "rs_s", "skp", "svp", "sks", "svs", "fkp", "fvp", "flp", "fks", "fvs", "fls")}
    for i in range(depth):
        w = w_in[i]
        splits = np.cumsum([D_RET] * 4 + [D_SB] * 4 + [D_FOX] * 4).tolist()
        (w_qr, w_kr, w_vr, w_gr, w_qs, w_ks, w_vs, w_gs, w_qf, w_kf, w_vf, w_gf, w_fl) = jnp.split(w, splits, axis=1)
        w_row = jnp.concatenate([w_qr, w_vr, w_gr, w_qs, w_gs, w_qf, w_gf], axis=1).astype(BF16)
        w_flt = jnp.pad(w_fl.T, ((0, SUBLANES - H_FOX), (0, 0))).astype(BF16)
        w_t = [a.T.astype(BF16) for a in (w_kr, w_ks, w_vs, w_kf, w_vf)] + [w_flt]
        padc = lambda a: jnp.pad(a, ((0, 0), (0, PIECE - a.shape[1])))
        w_pieces = jnp.concatenate([padc(a) for a in (w_qr, w_kr, w_vr, w_gr, w_qs, w_ks, w_vs, w_gs,
                                                      w_qf, w_kf, w_vf, w_gf)], axis=1).astype(BF16)
        gn = g_norm[i][None, :]
        b_f8 = jnp.pad(b_f[i], (0, SUBLANES - H_FOX))[:, None]
        gq_row = jnp.tile(g_fox_q[i], H_FOX)[None, :]
        gk_row = jnp.tile(g_fox_k[i], H_FOX)[None, :]
        gk_col = g_fox_k[i][:, None]
        ggn = g_ret_gn[i][None, :]
        w_o = w_out[i].astype(BF16)
        w_op = (w_o[:D_RET], w_o[D_RET:D_RET + D_SB], w_o[D_RET + D_SB:],
                w_pe[i].astype(BF16), g_pe[i][None, :], w_pg[i].astype(BF16))

        (qr, vr, gr, qs, gs, qf, gf, krt, kst, vst, kft, vft, lft) = _inproj_prompt(
            y_p, gn, (cos_p, sin_p, cost_p, sint_p), b_f8, gk_col, [w_row] + w_t, tm_in)
        o_r, s_fin = _ret_prompt(qr, krt, vr, gr, dec_p, ggn)
        o_s = _sb_prompt(qs, kst, vst, gs, lmat_att, tq_att)
        o_f = _fox_prompt(qf, kft, vft, lft, gf, gq_row, seg, umat, tq_att)
        y_p = _outproj(y_p.reshape(n_p, d), o_r.reshape(n_p, D_RET), o_s.reshape(n_p, D_SB),
                       o_f.reshape(n_p, D_FOX), p_prompt[i].reshape(n_p, d_ple), w_op, tm_out).reshape(bp, tp, d)
        outs["rs_p"].append(s_fin)
        outs["skp"].append(kst); outs["svp"].append(vst)
        outs["fkp"].append(kft); outs["fvp"].append(vft); outs["flp"].append(lft[:, :H_FOX])

        proj, flt = _inproj_sample(y_s, gn, w_pieces, w_flt)
        flt_b = jnp.transpose(flt.reshape(SUBLANES, bs, ts), (1, 0, 2))
        o_r, s_new = _ret_sample(proj, state_ret[i], (cos_s, sin_s), dec_s, ggn, ts)
        o_s, o_f, kf_new, lf_new = _decode(proj, flt_b, tuple(p for p in pools), lf_pool, page_table, i,
                                           b_f8, gq_row, gk_row, seg, lmat_page, ts, n_grp)
        y_s = _outproj(y_s, o_r, o_s, o_f, p_sample[i].reshape(n_s, d_ple), w_op, n_s)
        outs["rs_s"].append(s_new)
        piece = lambda k, wd: proj[:, k * PIECE:k * PIECE + wd]
        outs["sks"].append(piece(5, D_SB)); outs["svs"].append(piece(6, D_SB))
        outs["fks"].append(kf_new); outs["fvs"].append(piece(10, D_FOX))
        outs["fls"].append(lf_new[:, :H_FOX])

    st = lambda k: jnp.stack(outs[k], axis=0)
    kv_p = lambda k, h: jnp.transpose(st(k).reshape(depth, bp, h, HEAD_DIM, tp), (0, 1, 4, 2, 3))
    kv_s = lambda k, h: st(k).reshape(depth, bs, ts, h, HEAD_DIM)
    return (y_p, y_s.reshape(bs, ts, d), st("rs_p"), st("rs_s"),
            kv_p("skp", H_SB), kv_p("svp", H_SB), kv_s("sks", H_SB), kv_s("svs", H_SB),
            kv_p("fkp", H_FOX), kv_p("fvp", H_FOX), jnp.transpose(st("flp"), (0, 1, 3, 2)),
            kv_s("fks", H_FOX), kv_s("fvs", H_FOX), jnp.transpose(st("fls"), (0, 1, 3, 2)))
```

```python
import functools

import jax
import jax.numpy as jnp
import numpy as np
from jax import lax
from jax.experimental import pallas as pl
from jax.experimental.pallas import tpu as pltpu

F32 = jnp.float32
BF16 = jnp.bfloat16

HEAD_DIM = 64
H_RET, H_SB, H_FOX = 6, 5, 5
D_RET, D_SB, D_FOX = H_RET * HEAD_DIM, H_SB * HEAD_DIM, H_FOX * HEAD_DIM
PAGE_SIZE = 128
RET_CHUNK = 128
ATT_BLOCK = 128
ROW_CHUNK = 32
LANES = 128
SUBLANES = 8
ROPE_BASE = 10000.0
EPS = 1e-6
QK_SCALE = HEAD_DIM ** -0.5
SB_DEAD = -104.0
NEG_BIG = -1e30
PIECE = 384
VMEM_LIMIT = 48 * 1024 * 1024


def _dot(a, b):
    return jnp.dot(a, b, preferred_element_type=F32)


def _dot_nt(a, b):
    return lax.dot_general(a, b, (((1,), (1,)), ((), ())), preferred_element_type=F32)


def _dot_tn(a, b):
    return lax.dot_general(a, b, (((0,), (0,)), ((), ())), preferred_element_type=F32)


def _split2(x):
    hi = x.astype(BF16)
    lo = (x - hi.astype(F32)).astype(BF16)
    return hi, lo


def _split3(x):
    p1 = x.astype(BF16)
    r1 = x - p1.astype(F32)
    p2 = r1.astype(BF16)
    p3 = (r1 - p2.astype(F32)).astype(BF16)
    return p1, p2, p3


def _dot_exact2(x, m):
    hi, lo = _split2(x)
    return _dot(hi, m) + _dot(lo, m)


def _dot_exact3(x, m):
    p1, p2, p3 = _split3(x)
    return _dot(p1, m) + _dot(p2, m) + _dot(p3, m)


def _softplus(z):
    return jnp.maximum(z, 0.0) + jnp.log(1.0 + jnp.exp(-jnp.abs(z)))


def _log_sigmoid(x):
    return jnp.minimum(x, 0.0) - jnp.log1p(jnp.exp(-jnp.abs(x)))


def _silu(g):
    return g * jax.nn.sigmoid(g)


def _rotary_rows(x, cos, sin_signed):
    lane = lax.broadcasted_iota(jnp.int32, (x.shape[0], LANES), 1)
    first_half = (lane % HEAD_DIM) < (HEAD_DIM // 2)
    out = []
    for c in range(x.shape[1] // LANES):
        xs = x[:, c * LANES:(c + 1) * LANES]
        partner = jnp.where(first_half,
                            pltpu.roll(xs, LANES - HEAD_DIM // 2, axis=1),
                            pltpu.roll(xs, HEAD_DIM // 2, axis=1))
        out.append(xs * cos + partner * sin_signed)
    return jnp.concatenate(out, axis=1)


def _inproj_prompt_kernel(x_ref, g_ref, cos_ref, sin_ref, cost_ref, sint_ref, bf_ref, gk_ref,
                          wrow_ref, wkr_ref, wks_ref, wvs_ref, wkf_ref, wvf_ref, wfl_ref,
                          qr_ref, vr_ref, gr_ref, qs_ref, gs_ref, qf_ref, gf_ref,
                          krt_ref, kst_ref, vst_ref, kft_ref, vft_ref, lft_ref):
    x = x_ref[...]
    ms = jnp.mean(x * x, axis=-1, keepdims=True)
    h = (x * lax.rsqrt(ms + EPS) * g_ref[...]).astype(BF16)

    row = _dot(h, wrow_ref[...])
    qr_ref[...] = _rotary_rows(row[:, 0:D_RET], cos_ref[...], sin_ref[...])
    vr_ref[...] = row[:, D_RET:2 * D_RET]
    gr_ref[...] = row[:, 2 * D_RET:3 * D_RET]
    o = 3 * D_RET
    qs_ref[...] = row[:, o:o + D_SB]
    gs_ref[...] = row[:, o + D_SB:o + 2 * D_SB]
    qf_ref[...] = row[:, o + 2 * D_SB:o + 3 * D_SB]
    gf_ref[...] = row[:, o + 3 * D_SB:o + 4 * D_SB]

    half = HEAD_DIM // 2
    krt = _dot_nt(wkr_ref[...], h)
    cost, sint = cost_ref[...], sint_ref[...]
    for hd in range(H_RET):
        x1 = krt[hd * HEAD_DIM:hd * HEAD_DIM + half]
        x2 = krt[hd * HEAD_DIM + half:(hd + 1) * HEAD_DIM]
        krt_ref[hd * HEAD_DIM:hd * HEAD_DIM + half, :] = (x1 * cost - x2 * sint) * QK_SCALE
        krt_ref[hd * HEAD_DIM + half:(hd + 1) * HEAD_DIM, :] = (x1 * sint + x2 * cost) * QK_SCALE

    kst_ref[...] = _dot_nt(wks_ref[...], h)
    vst_ref[...] = _dot_nt(wvs_ref[...], h)
    vft_ref[...] = _dot_nt(wvf_ref[...], h)
    kft = _dot_nt(wkf_ref[...], h)
    gk = gk_ref[...]
    for hd in range(H_FOX):
        xh = kft[hd * HEAD_DIM:(hd + 1) * HEAD_DIM]
        msh = jnp.mean(xh * xh, axis=0, keepdims=True)
        kft_ref[hd * HEAD_DIM:(hd + 1) * HEAD_DIM, :] = xh * lax.rsqrt(msh + EPS) * gk
    lft_ref[...] = _log_sigmoid(_dot_nt(wfl_ref[...], h) + bf_ref[...])


def _inproj_prompt(x, g_norm, tabs, b_f8, gk_col, w, tm):
    bsz, t, d = x.shape
    n_row = 3 * D_RET + 4 * D_SB
    grid = (bsz, t // tm)
    row_spec = lambda n: pl.BlockSpec((None, tm, n), lambda b, i: (b, i, 0))
    col_spec = lambda n: pl.BlockSpec((None, n, tm), lambda b, i: (b, 0, i))
    full = lambda a: pl.BlockSpec(a.shape, lambda b, i: (0,) * a.ndim)
    cos, sin, cost, sint = tabs
    in_specs = [
        row_spec(d), full(g_norm),
        pl.BlockSpec((tm, LANES), lambda b, i: (i, 0)), pl.BlockSpec((tm, LANES), lambda b, i: (i, 0)),
        pl.BlockSpec((HEAD_DIM // 2, tm), lambda b, i: (0, i)), pl.BlockSpec((HEAD_DIM // 2, tm), lambda b, i: (0, i)),
        full(b_f8), full(gk_col),
    ] + [full(a) for a in w]
    rs = lambda n: jax.ShapeDtypeStruct((bsz, t, n), F32)
    cs = lambda n: jax.ShapeDtypeStruct((bsz, n, t), F32)
    out_shape = [rs(D_RET), rs(D_RET), rs(D_RET), rs(D_SB), rs(D_SB), rs(D_FOX), rs(D_FOX),
                 cs(D_RET), cs(D_SB), cs(D_SB), cs(D_FOX), cs(D_FOX), cs(SUBLANES)]
    out_specs = [row_spec(D_RET)] * 3 + [row_spec(D_SB)] * 4 + \
                [col_spec(D_RET)] + [col_spec(D_SB)] * 4 + [col_spec(SUBLANES)]
    return pl.pallas_call(
        _inproj_prompt_kernel, grid=grid, in_specs=in_specs, out_specs=out_specs, out_shape=out_shape,
        compiler_params=pltpu.CompilerParams(dimension_semantics=("parallel", "parallel"),
                                             vmem_limit_bytes=VMEM_LIMIT),
        name="inproj_prompt",
    )(x, g_norm, cos, sin, cost, sint, b_f8, gk_col, *w)


def _ret_prompt_kernel(qr_ref, krt_ref, vr_ref, gr_ref, dmask_ref, qdec_ref, kdec_ref, cdec_ref, ggn_ref,
                       o_ref, sfin_ref, s_scr):
    c = pl.program_id(1)

    @pl.when(c == 0)
    def _():
        s_scr[...] = jnp.zeros_like(s_scr)

    for hd in range(H_RET):
        sl = slice(hd * HEAD_DIM, (hd + 1) * HEAD_DIM)
        q = qr_ref[:, sl].astype(BF16)
        kt = krt_ref[sl, :]
        v = vr_ref[:, sl].astype(BF16)
        sc = _dot(q, kt.astype(BF16)) * dmask_ref[hd]
        inner = _dot(sc.astype(BF16), v)
        s_old = s_scr[hd]
        cross = _dot(q, s_old.astype(BF16)) * qdec_ref[hd]
        o = inner + cross
        s_scr[hd] = s_old * cdec_ref[hd] + _dot((kt * kdec_ref[hd]).astype(BF16), v)
        mu = jnp.mean(o, axis=-1, keepdims=True)
        dev = o - mu
        var = jnp.mean(dev * dev, axis=-1, keepdims=True)
        y = dev * lax.rsqrt(var + EPS) * ggn_ref[:, sl]
        o_ref[:, sl] = y * _silu(gr_ref[:, sl])

    @pl.when(c == pl.num_programs(1) - 1)
    def _():
        sfin_ref[...] = s_scr[...]


def _ret_prompt(qr, krt, vr, gr, dec, ggn):
    bsz, t, _ = qr.shape
    blk = RET_CHUNK
    dmask, qdec, kdec, cdec = dec
    row = pl.BlockSpec((None, blk, D_RET), lambda b, c: (b, c, 0))
    full = lambda a: pl.BlockSpec(a.shape, lambda b, c: (0,) * a.ndim)
    return pl.pallas_call(
        _ret_prompt_kernel, grid=(bsz, t // blk),
        in_specs=[row, pl.BlockSpec((None, D_RET, blk), lambda b, c: (b, 0, c)), row, row,
                  full(dmask), full(qdec), full(kdec), full(cdec), full(ggn)],
        out_specs=[row, pl.BlockSpec((None, H_RET, HEAD_DIM, HEAD_DIM), lambda b, c: (b, 0, 0, 0))],
        out_shape=[jax.ShapeDtypeStruct((bsz, t, D_RET), F32),
                   jax.ShapeDtypeStruct((bsz, H_RET, HEAD_DIM, HEAD_DIM), F32)],
        scratch_shapes=[pltpu.VMEM((H_RET, HEAD_DIM, HEAD_DIM), F32)],
        compiler_params=pltpu.CompilerParams(dimension_semantics=("parallel", "arbitrary"),
                                             vmem_limit_bytes=VMEM_LIMIT),
        name="ret_prompt",
    )(qr, krt, vr, gr, dmask, qdec, kdec, cdec, ggn)


def _sb_prompt_kernel(q_ref, kt_ref, vt_ref, g_ref, lmat_ref, o_ref,
                      q_scr, car_scr, acc_scr, z_scr, ls_scr, hi_scr, lo_scr, a_scr, tot_scr):
    i = pl.program_id(1)
    tq = q_ref.shape[0]
    reps = tq // LANES
    for hd in range(H_SB):
        sl = slice(hd * HEAD_DIM, (hd + 1) * HEAD_DIM)
        q_scr[hd] = (q_ref[:, sl] * QK_SCALE).astype(BF16)
    car_scr[...] = jnp.zeros_like(car_scr)
    acc_scr[...] = jnp.zeros_like(acc_scr)

    def block(j, diag):
        off = pl.multiple_of(j * tq, tq)
        lmat = lmat_ref[...]
        for hd in range(H_SB):
            sl = slice(hd * HEAD_DIM, (hd + 1) * HEAD_DIM)
            z_scr[hd] = _dot(q_scr[hd], kt_ref[sl, pl.ds(off, tq)].astype(BF16))
        chunk_masks = []
        for r in range(0, tq, ROW_CHUNK):
            if diag:
                rowi = lax.broadcasted_iota(jnp.int32, (ROW_CHUNK, tq), 0) + r
                coli = lax.broadcasted_iota(jnp.int32, (ROW_CHUNK, tq), 1)
                chunk_masks.append(coli < rowi)
            else:
                chunk_masks.append(None)
        for hd in range(H_SB):
            for ci, r in enumerate(range(0, tq, ROW_CHUNK)):
                rows = slice(r, r + ROW_CHUNK)
                z = z_scr[hd, rows, :]
                sp = _softplus(z)
                ls_scr[hd, rows, :] = z - sp
                if diag:
                    sp = jnp.where(chunk_masks[ci], sp, 0.0)
                hi, lo = _split2(sp)
                hi_scr[hd, rows, :] = hi
                lo_scr[hd, rows, :] = lo
                tot = jnp.sum(sp, axis=-1, keepdims=True)
                tot_scr[hd, rows, :] = jnp.broadcast_to(tot, (ROW_CHUNK, LANES))
        for hd in range(H_SB):
            z_scr[hd] = _dot(hi_scr[hd], lmat) + _dot(lo_scr[hd], lmat)
        for hd in range(H_SB):
            for ci, r in enumerate(range(0, tq, ROW_CHUNK)):
                rows = slice(r, r + ROW_CHUNK)
                carry = car_scr[hd, rows, :]
                a = jnp.exp(ls_scr[hd, rows, :] - z_scr[hd, rows, :] + jnp.tile(carry, (1, reps)))
                if diag:
                    a = jnp.where(chunk_masks[ci], a, 0.0)
                a_scr[hd, rows, :] = a.astype(BF16)
                car_scr[hd, rows, :] = carry - tot_scr[hd, rows, :]
        for hd in range(H_SB):
            sl = slice(hd * HEAD_DIM, (hd + 1) * HEAD_DIM)
            acc_scr[hd] += _dot_nt(a_scr[hd], vt_ref[sl, pl.ds(off, tq)].astype(BF16))
        return jnp.max(car_scr[...])

    cmax = block(i, True)

    def cond(st):
        j, cmax = st
        return jnp.logical_and(j >= 0, cmax > SB_DEAD)

    def body(st):
        j, _ = st
        return j - 1, block(j, False)

    lax.while_loop(cond, body, (i - 1, cmax))
    for hd in range(H_SB):
        sl = slice(hd * HEAD_DIM, (hd + 1) * HEAD_DIM)
        o_ref[:, sl] = acc_scr[hd] * _silu(g_ref[:, sl])


def _sb_prompt(qs, kst, vst, gs, lmat, tq):
    bsz, t, _ = qs.shape
    row = pl.BlockSpec((None, tq, D_SB), lambda b, i: (b, i, 0))
    seq = pl.BlockSpec((None, D_SB, t), lambda b, i: (b, 0, 0))
    return pl.pallas_call(
        _sb_prompt_kernel, grid=(bsz, t // tq),
        in_specs=[row, seq, seq, row, pl.BlockSpec(lmat.shape, lambda b, i: (0, 0))],
        out_specs=row, out_shape=jax.ShapeDtypeStruct((bsz, t, D_SB), F32),
        scratch_shapes=[pltpu.VMEM((H_SB, tq, HEAD_DIM), BF16), pltpu.VMEM((H_SB, tq, LANES), F32),
                        pltpu.VMEM((H_SB, tq, HEAD_DIM), F32),
                        pltpu.VMEM((H_SB, tq, tq), F32), pltpu.VMEM((H_SB, tq, tq), F32),
                        pltpu.VMEM((H_SB, tq, tq), BF16), pltpu.VMEM((H_SB, tq, tq), BF16),
                        pltpu.VMEM((H_SB, tq, tq), BF16), pltpu.VMEM((H_SB, tq, LANES), F32)],
        compiler_params=pltpu.CompilerParams(dimension_semantics=("parallel", "arbitrary"),
                                             vmem_limit_bytes=VMEM_LIMIT),
        name="sb_prompt",
    )(qs, kst, vst, gs, lmat)


def _fox_prompt_kernel(q_ref, kt_ref, vt_ref, lft_ref, g_ref, gq_ref, seg_ref, umat_ref, o_ref,
                       c_scr, q_scr, m_scr, acc_scr, s_scr, p_scr, a_scr):
    i = pl.program_id(1)
    tq = q_ref.shape[0]
    t = kt_ref.shape[1]
    cw = umat_ref.shape[0]

    @pl.when(i == 0)
    def _():
        umat = umat_ref[...]
        carry = jnp.zeros((SUBLANES, 1), F32)
        for c in range(t // cw):
            cs = _dot_exact3(lft_ref[:, c * cw:(c + 1) * cw], umat) + carry
            c_scr[:, c * cw:(c + 1) * cw] = cs
            carry = cs[:, cw - 1:cw]

    qraw = q_ref[...]
    ssq = _dot_exact2(qraw * qraw, seg_ref[...])
    qn = qraw * lax.rsqrt(ssq * (1.0 / HEAD_DIM) + EPS) * gq_ref[...] * QK_SCALE
    for hd in range(H_FOX):
        q_scr[hd] = qn[:, hd * HEAD_DIM:(hd + 1) * HEAD_DIM].astype(BF16)
    m_scr[...] = jnp.full_like(m_scr, NEG_BIG)
    acc_scr[...] = jnp.zeros_like(acc_scr)
    rowi = lax.broadcasted_iota(jnp.int32, (tq, tq), 0)
    coli = lax.broadcasted_iota(jnp.int32, (tq, tq), 1)
    diag_mask = coli <= rowi

    ones_rows = jnp.ones((HEAD_DIM, tq), BF16)

    def block(j, mask):
        off = pl.multiple_of(j * tq, tq)
        for hd in range(H_FOX):
            sl = slice(hd * HEAD_DIM, (hd + 1) * HEAD_DIM)
            kt = kt_ref[sl, pl.ds(off, tq)].astype(BF16)
            s = _dot(q_scr[hd], kt) - c_scr[hd:hd + 1, pl.ds(off, tq)]
            if mask is not None:
                s = jnp.where(mask, s, NEG_BIG)
            s_scr[hd] = s
        for hd in range(H_FOX):
            for r in range(0, tq, ROW_CHUNK):
                rows = slice(r, r + ROW_CHUNK)
                s = s_scr[hd, rows, :]
                m_old = m_scr[hd, rows, :]
                mx = jnp.max(s, axis=-1, keepdims=True)
                m_new = jnp.maximum(m_old, jnp.broadcast_to(mx, (ROW_CHUNK, LANES)))
                p_scr[hd, rows, :] = jnp.exp(s - jnp.tile(m_new, (1, tq // LANES))).astype(BF16)
                a_scr[hd, rows, :] = jnp.exp(m_old - m_new)
                m_scr[hd, rows, :] = m_new
        for hd in range(H_FOX):
            sl = slice(hd * HEAD_DIM, (hd + 1) * HEAD_DIM)
            vt = jnp.concatenate([vt_ref[sl, pl.ds(off, tq)].astype(BF16), ones_rows], axis=0)
            acc_scr[hd] = a_scr[hd] * acc_scr[hd] + _dot_nt(p_scr[hd], vt)

    def body(j, carry):
        block(j, None)
        return carry

    lax.fori_loop(0, i, body, 0)
    block(i, diag_mask)
    for hd in range(H_FOX):
        sl = slice(hd * HEAD_DIM, (hd + 1) * HEAD_DIM)
        acc = acc_scr[hd]
        o_ref[:, sl] = acc[:, :HEAD_DIM] / acc[:, HEAD_DIM:HEAD_DIM + 1] * _silu(g_ref[:, sl])


def _fox_prompt(qf, kft, vft, lft, gf, gq_row, seg, umat, tq):
    bsz, t, _ = qf.shape
    row = pl.BlockSpec((None, tq, D_FOX), lambda b, i: (b, i, 0))
    seq = pl.BlockSpec((None, D_FOX, t), lambda b, i: (b, 0, 0))
    full = lambda a: pl.BlockSpec(a.shape, lambda b, i: (0,) * a.ndim)
    return pl.pallas_call(
        _fox_prompt_kernel, grid=(bsz, t // tq),
        in_specs=[row, seq, seq, pl.BlockSpec((None, SUBLANES, t), lambda b, i: (b, 0, 0)), row,
                  full(gq_row), full(seg), full(umat)],
        out_specs=row, out_shape=jax.ShapeDtypeStruct((bsz, t, D_FOX), F32),
        scratch_shapes=[pltpu.VMEM((SUBLANES, t), F32), pltpu.VMEM((H_FOX, tq, HEAD_DIM), BF16),
                        pltpu.VMEM((H_FOX, tq, LANES), F32), pltpu.VMEM((H_FOX, tq, 2 * HEAD_DIM), F32),
                        pltpu.VMEM((H_FOX, tq, tq), F32), pltpu.VMEM((H_FOX, tq, tq), BF16),
                        pltpu.VMEM((H_FOX, tq, LANES), F32)],
        compiler_params=pltpu.CompilerParams(dimension_semantics=("parallel", "arbitrary"),
                                             vmem_limit_bytes=VMEM_LIMIT),
        name="fox_prompt",
    )(qf, kft, vft, lft, gf, gq_row, seg, umat)


def _outproj_kernel(y_ref, or_ref, os_ref, of_ref, p_ref, wor_ref, wos_ref, wof_ref, wpe_ref, gpe_ref, wpg_ref,
                    o_ref):
    m = (_dot(or_ref[...].astype(BF16), wor_ref[...]) + _dot(os_ref[...].astype(BF16), wos_ref[...])
         + _dot(of_ref[...].astype(BF16), wof_ref[...]))
    y1 = y_ref[...] + m
    ms = jnp.mean(y1 * y1, axis=-1, keepdims=True)
    n = (y1 * lax.rsqrt(ms + EPS) * gpe_ref[...]).astype(BF16)
    gate = jax.nn.sigmoid(_dot(n, wpg_ref[...]))
    pe = _dot(p_ref[...].astype(BF16), wpe_ref[...])
    o_ref[...] = y1 + pe * gate


def _outproj(y, o_r, o_s, o_f, p, w, tm):
    n, d = y.shape
    row = lambda a: pl.BlockSpec((tm, a.shape[1]), lambda i: (i, 0))
    full = lambda a: pl.BlockSpec(a.shape, lambda i: (0,) * a.ndim)
    return pl.pallas_call(
        _outproj_kernel, grid=(n // tm,),
        in_specs=[row(y), row(o_r), row(o_s), row(o_f), row(p)] + [full(a) for a in w],
        out_specs=row(y), out_shape=jax.ShapeDtypeStruct((n, d), F32),
        compiler_params=pltpu.CompilerParams(dimension_semantics=("parallel",), vmem_limit_bytes=VMEM_LIMIT),
        name="outproj",
    )(y, o_r, o_s, o_f, p, *w)


def _inproj_sample_kernel(x_ref, g_ref, w_ref, wfl_ref, o_ref, flt_ref):
    x = x_ref[...]
    ms = jnp.mean(x * x, axis=-1, keepdims=True)
    h = (x * lax.rsqrt(ms + EPS) * g_ref[...]).astype(BF16)
    o_ref[...] = _dot(h, w_ref[...])
    flt_ref[...] = _dot_nt(wfl_ref[...], h)


def _inproj_sample(x, g_norm, w_pieces, wfl):
    n, d = x.shape
    ncol = w_pieces.shape[1]
    full = lambda a: pl.BlockSpec(a.shape, lambda i: (0,) * a.ndim)
    return pl.pallas_call(
        _inproj_sample_kernel, grid=(1,),
        in_specs=[full(x), full(g_norm), full(w_pieces), full(wfl)],
        out_specs=[pl.BlockSpec((n, ncol), lambda i: (0, 0)), pl.BlockSpec((SUBLANES, n), lambda i: (0, 0))],
        out_shape=[jax.ShapeDtypeStruct((n, ncol), F32), jax.ShapeDtypeStruct((SUBLANES, n), F32)],
        compiler_params=pltpu.CompilerParams(dimension_semantics=("arbitrary",), vmem_limit_bytes=VMEM_LIMIT),
        name="inproj_sample",
    )(x, g_norm, w_pieces, wfl)


def _ret_sample_kernel(q_ref, k_ref, v_ref, g_ref, s0_ref, cos_ref, sin_ref, dmask_ref, qdec_ref, kdec_ref,
                       cdec_ref, ggn_ref, o_ref, s_ref):
    cos, sin = cos_ref[...], sin_ref[...]
    qr = _rotary_rows(q_ref[...], cos, sin)
    kr = _rotary_rows(k_ref[...], cos, sin) * QK_SCALE
    for hd in range(H_RET):
        sl = slice(hd * HEAD_DIM, (hd + 1) * HEAD_DIM)
        q = qr[:, sl].astype(BF16)
        k = kr[:, sl]
        v = v_ref[:, sl].astype(BF16)
        sc = _dot_nt(q, k.astype(BF16)) * dmask_ref[hd]
        inner = _dot(sc.astype(BF16), v)
        s_old = s0_ref[hd]
        cross = _dot(q, s_old.astype(BF16)) * qdec_ref[hd]
        o = inner + cross
        s_ref[hd] = s_old * cdec_ref[hd] + _dot_tn((k * kdec_ref[hd]).astype(BF16), v)
        mu = jnp.mean(o, axis=-1, keepdims=True)
        dev = o - mu
        var = jnp.mean(dev * dev, axis=-1, keepdims=True)
        y = dev * lax.rsqrt(var + EPS) * ggn_ref[:, sl]
        o_ref[:, sl] = y * _silu(g_ref[:, sl])


def _ret_sample(proj, s0, tabs, dec, ggn, ts):
    n = proj.shape[0]
    bsz = n // ts
    cos, sin = tabs
    dmask, qdec, kdec, cdec = dec
    piece = lambda k: pl.BlockSpec((ts, PIECE), lambda b, k=k: (b, k))
    full = lambda a: pl.BlockSpec(a.shape, lambda b: (0,) * a.ndim)
    st = pl.BlockSpec((None, H_RET, HEAD_DIM, HEAD_DIM), lambda b: (b, 0, 0, 0))
    return pl.pallas_call(
        _ret_sample_kernel, grid=(bsz,),
        in_specs=[piece(0), piece(1), piece(2), piece(3), st, full(cos), full(sin),
                  full(dmask), full(qdec), full(kdec), full(cdec), full(ggn)],
        out_specs=[pl.BlockSpec((ts, D_RET), lambda b: (b, 0)), st],
        out_shape=[jax.ShapeDtypeStruct((n, D_RET), F32),
                   jax.ShapeDtypeStruct((bsz, H_RET, HEAD_DIM, HEAD_DIM), F32)],
        compiler_params=pltpu.CompilerParams(dimension_semantics=("parallel",), vmem_limit_bytes=VMEM_LIMIT),
        name="ret_sample",
    )(proj, proj, proj, proj, s0, cos, sin, dmask, qdec, kdec, cdec, ggn)


def _head_rows(x, ts):
    rows = H_SB * ts
    r = lax.broadcasted_iota(jnp.int32, (rows, D_SB), 0)
    c = lax.broadcasted_iota(jnp.int32, (rows, D_SB), 1)
    tiled = jnp.concatenate([x] * H_SB, axis=0)
    return jnp.where(r // ts == c // HEAD_DIM, tiled, 0.0)


def _head_cols(x, ts):
    c = lax.broadcasted_iota(jnp.int32, (ts, D_SB), 1)
    out = jnp.zeros((ts, D_SB), F32)
    for hd in range(H_SB):
        out = out + jnp.where(c // HEAD_DIM == hd, x[hd * ts:(hd + 1) * ts], 0.0)
    return out


def _new_token_masks(ts, strict):
    rows = H_SB * ts
    tpos = lax.broadcasted_iota(jnp.int32, (rows, PAGE_SIZE), 0) % ts
    coli = lax.broadcasted_iota(jnp.int32, (rows, PAGE_SIZE), 1)
    return coli < tpos if strict else coli <= tpos


def _sb_blocks(zs, v_dots, lmat, acc, car):
    for (z, mask), v_dot in zip(zs, v_dots):
        sp = _softplus(z)
        ls = z - sp
        if mask is not None:
            sp = jnp.where(mask, sp, 0.0)
        a = jnp.exp(ls - _dot_exact2(sp, lmat) + car)
        if mask is not None:
            a = jnp.where(mask, a, 0.0)
        acc = acc + v_dot(a.astype(BF16))
        car = car - jnp.sum(sp, axis=-1, keepdims=True)
    return acc, car


def _decode_sb_kernel(pt_ref, qs_ref, ks_ref, vs_ref, gs_ref, lmat_ref, *rest, n_grp, ts, head):
    if head:
        pages = rest[:2 * n_grp]
        os_ref, acco_ref, caro_ref = rest[2 * n_grp:]
    else:
        acci_ref, cari_ref = rest[:2]
        pages = rest[2:2 + 2 * n_grp]
        os_ref, qbs_scr, acc_scr, car_scr = rest[2 + 2 * n_grp:]
    lmat = lmat_ref[...]
    order = list(range(n_grp - 1, -1, -1))

    def page_blocks(qbs):
        zs, dots = [], []
        for gi in order:
            kt_ref, vt_ref = pages[2 * gi:2 * gi + 2]
            zs.append((_dot(qbs, kt_ref[...].astype(BF16)), None))
            dots.append(lambda a, r=vt_ref: _dot_nt(a, r[...].astype(BF16)))
        return zs, dots

    qbs_new = _head_rows(qs_ref[:, :D_SB] * QK_SCALE, ts).astype(BF16)
    if head:
        pad = jnp.zeros((PAGE_SIZE - ts, D_SB), F32)
        k_new = jnp.concatenate([ks_ref[:, :D_SB], pad], axis=0).astype(BF16)
        v_new = jnp.concatenate([vs_ref[:, :D_SB], pad], axis=0).astype(BF16)
        zs, dots = page_blocks(qbs_new)
        acc, car = _sb_blocks([(_dot_nt(qbs_new, k_new), _new_token_masks(ts, True))] + zs,
                              [lambda a: _dot(a, v_new)] + dots, lmat,
                              jnp.zeros((H_SB * ts, D_SB), F32), jnp.zeros((H_SB * ts, 1), F32))
        acco_ref[...] = acc
        caro_ref[...] = jnp.broadcast_to(car, caro_ref.shape)
        os_ref[...] = _head_cols(acc, ts) * _silu(gs_ref[:, :D_SB])
    else:
        s = pl.program_id(1)

        @pl.when(s == 0)
        def _():
            qbs_scr[...] = qbs_new
            acc_scr[...] = acci_ref[...]
            car_scr[...] = cari_ref[:, 0:1]

        @pl.when(jnp.max(car_scr[...]) > SB_DEAD)
        def _():
            zs, dots = page_blocks(qbs_scr[...])
            acc, car = _sb_blocks(zs, dots, lmat, acc_scr[...], car_scr[...])
            acc_scr[...] = acc
            car_scr[...] = car

        @pl.when(s == pl.num_programs(1) - 1)
        def _():
            os_ref[...] = _head_cols(acc_scr[...], ts) * _silu(gs_ref[:, :D_SB])


def _decode_sb(proj, pools, page_table, layer, lmat, ts, n_head, n_grp, state=None):
    n = proj.shape[0]
    bsz, n_pages = page_table.shape
    rows = H_SB * ts
    head = state is None
    per_step = n_head if head else n_grp
    n_steps = 1 if head else (n_pages - n_head) // n_grp
    last = n_pages - 1 if head else n_pages - n_head - 1
    piece = lambda k: pl.BlockSpec((ts, PIECE), lambda b, s, pt, k=k: (b, k))

    def page_spec(gi):
        return pl.BlockSpec((None, None, D_SB, PAGE_SIZE),
                            lambda b, s, pt, gi=gi: (layer, pt[b, last - s * per_step - (per_step - 1 - gi)], 0, 0))

    page_specs = [page_spec(gi) for gi in range(per_step) for _ in range(2)]
    page_args = [p for _ in range(per_step) for p in pools]
    out_row = pl.BlockSpec((ts, D_SB), lambda b, s, pt: (b, 0))
    st_acc = pl.BlockSpec((rows, D_SB), lambda b, s, pt: (b, 0))
    st_car = pl.BlockSpec((rows, LANES), lambda b, s, pt: (b, 0))
    in_specs = [piece(4), piece(5), piece(6), piece(7), pl.BlockSpec(lmat.shape, lambda b, s, pt: (0, 0))]
    args = [proj] * 4 + [lmat]
    if head:
        out_specs = [out_row, st_acc, st_car]
        out_shape = [jax.ShapeDtypeStruct((n, D_SB), F32), jax.ShapeDtypeStruct((bsz * rows, D_SB), F32),
                     jax.ShapeDtypeStruct((bsz * rows, LANES), F32)]
        scratch = []
    else:
        in_specs += [st_acc, st_car]
        args += list(state)
        out_specs = out_row
        out_shape = jax.ShapeDtypeStruct((n, D_SB), F32)
        scratch = [pltpu.VMEM((rows, D_SB), BF16), pltpu.VMEM((rows, D_SB), F32), pltpu.VMEM((rows, 1), F32)]
    grid_spec = pltpu.PrefetchScalarGridSpec(
        num_scalar_prefetch=1, grid=(bsz, n_steps), in_specs=in_specs + page_specs, out_specs=out_specs,
        scratch_shapes=scratch)
    return pl.pallas_call(
        functools.partial(_decode_sb_kernel, n_grp=per_step, ts=ts, head=head), grid_spec=grid_spec,
        out_shape=out_shape,
        compiler_params=pltpu.CompilerParams(dimension_semantics=("parallel", "arbitrary"),
                                             vmem_limit_bytes=VMEM_LIMIT),
        name="decode_sb_head" if head else "decode_sb_tail",
    )(page_table, *args, *page_args)


def _decode_fox_kernel(pt_ref, qf_ref, kf_ref, vf_ref, gf_ref, flt_ref, bf_ref, gq_ref, gk_ref, seg_ref, lmat_ref,
                       *rest, n_grp, ts):
    pages = rest[:3 * n_grp]
    of_ref, kfn_ref, lfn_ref = rest[3 * n_grp:3 * n_grp + 3]
    qbf_scr, acc_scr, m_scr, l_scr, rcar_scr = rest[3 * n_grp + 3:]
    b = pl.program_id(0)
    s = pl.program_id(1)
    n_steps = pl.num_programs(1)
    lmat = lmat_ref[...]

    def rep_heads(x):
        return jnp.concatenate([jnp.broadcast_to(x[hd:hd + 1], (ts, x.shape[1])) for hd in range(H_FOX)], axis=0)

    def fox_blocks(scs, v_dots):
        scs = [sc if mask is None else jnp.where(mask, sc, NEG_BIG) for sc, mask in scs]
        m_old = m_scr[...]
        m_new = m_old
        for sc in scs:
            m_new = jnp.maximum(m_new, jnp.max(sc, axis=-1, keepdims=True))
        alpha = jnp.exp(m_old - m_new)
        l = alpha * l_scr[...]
        acc = alpha * acc_scr[...]
        for sc, v_dot in zip(scs, v_dots):
            p = jnp.exp(sc - m_new)
            l = l + jnp.sum(p, axis=-1, keepdims=True)
            acc = acc + v_dot(p.astype(BF16))
        l_scr[...] = l
        acc_scr[...] = acc
        m_scr[...] = m_new

    @pl.when(s == 0)
    def _():
        pad = jnp.zeros((PAGE_SIZE - ts, D_FOX), F32)
        seg = seg_ref[...]
        qraw = qf_ref[:, :D_FOX]
        qn = qraw * lax.rsqrt(_dot_exact2(qraw * qraw, seg) * (1.0 / HEAD_DIM) + EPS) * gq_ref[...] * QK_SCALE
        qbf = _head_rows(qn, ts).astype(BF16)
        qbf_scr[...] = qbf
        kraw = kf_ref[:, :D_FOX]
        kn = kraw * lax.rsqrt(_dot_exact2(kraw * kraw, seg) * (1.0 / HEAD_DIM) + EPS) * gk_ref[...]
        kfn_ref[...] = kn
        lf_new = _log_sigmoid(flt_ref[...] + bf_ref[...])
        lfn_ref[...] = lf_new
        lf_pad = jnp.concatenate([lf_new, jnp.zeros((SUBLANES, PAGE_SIZE - ts), F32)], axis=1)
        lf_rows = rep_heads(lf_pad)
        cn = jnp.sum(lf_rows, axis=-1, keepdims=True) - _dot_exact3(lf_rows, lmat)
        acc_scr[...] = jnp.zeros_like(acc_scr)
        l_scr[...] = jnp.zeros_like(l_scr)
        m_scr[...] = jnp.full_like(m_scr, NEG_BIG)
        rcar_scr[...] = jnp.zeros_like(rcar_scr)
        kn_pad = jnp.concatenate([kn, pad], axis=0).astype(BF16)
        vf_new = jnp.concatenate([vf_ref[:, :D_FOX], pad], axis=0).astype(BF16)
        fox_blocks([(_dot_nt(qbf, kn_pad) - cn, _new_token_masks(ts, False))], [lambda p: _dot(p, vf_new)])

    qbf = qbf_scr[...]
    rcar = rcar_scr[...]
    scs, f_dots = [], []
    for gi in range(n_grp - 1, -1, -1):
        fkt_ref, fvt_ref, lfp_ref = pages[3 * gi:3 * gi + 3]
        page = pt_ref[b, (n_steps - 1 - s) * n_grp + gi]
        lf_pg = lfp_ref[:, pl.ds(page % SUBLANES, 1), :]
        lf_rows = jnp.concatenate([jnp.broadcast_to(lf_pg[hd], (ts, PAGE_SIZE)) for hd in range(H_FOX)], axis=0)
        bias = _dot_exact3(lf_rows, lmat) + rcar
        rcar = rcar + jnp.sum(lf_rows, axis=-1, keepdims=True)
        scs.append((_dot(qbf, fkt_ref[...].astype(BF16)) + bias, None))
        f_dots.append(lambda p, r=fvt_ref: _dot_nt(p, r[...].astype(BF16)))
    rcar_scr[...] = rcar
    fox_blocks(scs, f_dots)

    @pl.when(s == n_steps - 1)
    def _():
        of_ref[...] = _head_cols(acc_scr[...] / l_scr[...], ts) * _silu(gf_ref[:, :D_FOX])


def _decode_fox(proj, flt, pools, lf_pool, page_table, layer, b_f8, gq_row, gk_row, seg, lmat, ts, n_grp):
    n = proj.shape[0]
    bsz, n_pages = page_table.shape
    n_steps = n_pages // n_grp
    rows = H_FOX * ts
    piece = lambda k: pl.BlockSpec((ts, PIECE), lambda b, s, pt, k=k: (b, k))
    full = lambda a: pl.BlockSpec(a.shape, lambda b, s, pt: (0,) * a.ndim)

    def page_spec(gi):
        return pl.BlockSpec((None, None, D_FOX, PAGE_SIZE),
                            lambda b, s, pt, gi=gi: (layer, pt[b, (n_steps - 1 - s) * n_grp + gi], 0, 0))

    def lf_spec(gi):
        return pl.BlockSpec((None, H_FOX, SUBLANES, PAGE_SIZE),
                            lambda b, s, pt, gi=gi: (layer, 0, pt[b, (n_steps - 1 - s) * n_grp + gi] // SUBLANES, 0))

    page_specs, page_args = [], []
    for gi in range(n_grp):
        page_specs += [page_spec(gi)] * 2 + [lf_spec(gi)]
        page_args += list(pools) + [lf_pool]
    out_row = pl.BlockSpec((ts, D_FOX), lambda b, s, pt: (b, 0))
    small = pl.BlockSpec((None, SUBLANES, ts), lambda b, s, pt: (b, 0, 0))
    grid_spec = pltpu.PrefetchScalarGridSpec(
        num_scalar_prefetch=1, grid=(bsz, n_steps),
        in_specs=[piece(8), piece(9), piece(10), piece(11), small,
                  full(b_f8), full(gq_row), full(gk_row), full(seg), full(lmat)] + page_specs,
        out_specs=[out_row, out_row, small],
        scratch_shapes=[pltpu.VMEM((rows, D_FOX), BF16), pltpu.VMEM((rows, D_FOX), F32),
                        pltpu.VMEM((rows, 1), F32), pltpu.VMEM((rows, 1), F32), pltpu.VMEM((rows, 1), F32)],
    )
    return pl.pallas_call(
        functools.partial(_decode_fox_kernel, n_grp=n_grp, ts=ts), grid_spec=grid_spec,
        out_shape=[jax.ShapeDtypeStruct((n, D_FOX), F32), jax.ShapeDtypeStruct((n, D_FOX), F32),
                   jax.ShapeDtypeStruct((bsz, SUBLANES, ts), F32)],
        compiler_params=pltpu.CompilerParams(dimension_semantics=("parallel", "arbitrary"),
                                             vmem_limit_bytes=VMEM_LIMIT),
        name="decode_fox",
    )(page_table, *([proj] * 4), flt, b_f8, gq_row, gk_row, seg, lmat, *page_args)


def _rotary_tables(pos):
    half = HEAD_DIM // 2
    inv = ROPE_BASE ** (-jnp.arange(half, dtype=F32) / half)
    ang = pos.astype(F32)[:, None] * inv[None, :]
    cos, sin = jnp.cos(ang), jnp.sin(ang)
    reps = LANES // HEAD_DIM
    cos_row = jnp.tile(jnp.concatenate([cos, cos], axis=1), (1, reps))
    sin_row = jnp.tile(jnp.concatenate([-sin, sin], axis=1), (1, reps))
    return cos_row, sin_row, cos.T, sin.T


def _decay_tables(blk):
    log_g = jnp.log1p(-jnp.exp2(-5.0 - jnp.arange(H_RET, dtype=F32)))
    i = jnp.arange(blk, dtype=F32)
    diff = i[:, None] - i[None, :]
    dmask = jnp.where(diff[None] >= 0, jnp.exp(jnp.maximum(diff, 0.0)[None] * log_g[:, None, None]), 0.0)
    q_dec = jnp.exp((i + 1.0)[None, :] * log_g[:, None])
    k_dec = jnp.exp((blk - 1.0 - i)[None, :] * log_g[:, None])
    c_dec = jnp.exp(blk * log_g)
    return dmask, q_dec, k_dec, c_dec


def kernel(x_prompt, x_sample, state_ret, cache_sb_k, cache_sb_v, cache_fox_k, cache_fox_v, cache_fox_logf,
           page_table, p_prompt, p_sample, g_norm, w_in, b_f, g_ret_gn, g_fox_q, g_fox_k, w_out, w_pe, g_pe, w_pg):
    bp, tp, d = x_prompt.shape
    bs, ts, _ = x_sample.shape
    depth = w_in.shape[0]
    n_pages = page_table.shape[1]
    past_len = n_pages * PAGE_SIZE
    d_ple = p_prompt.shape[-1]
    assert tp % ATT_BLOCK == 0 and tp % RET_CHUNK == 0 and ts == SUBLANES

    cos_p, sin_p, cost_p, sint_p = _rotary_tables(jnp.arange(tp, dtype=jnp.int32))
    cos_s, sin_s, _, _ = _rotary_tables(past_len + jnp.arange(ts, dtype=jnp.int32))
    dm_p, qd_p, kd_p, cd_p = _decay_tables(RET_CHUNK)
    dec_p = (dm_p, qd_p[:, :, None], kd_p[:, None, :], cd_p[:, None, None])
    dm_s, qd_s, kd_s, cd_s = _decay_tables(ts)
    dec_s = (dm_s, qd_s[:, :, None], kd_s[:, :, None], cd_s[:, None, None])
    tq_att = 2 * ATT_BLOCK if tp % (2 * ATT_BLOCK) == 0 else ATT_BLOCK
    later = lambda n: (jnp.arange(n)[:, None] > jnp.arange(n)[None, :]).astype(BF16)
    lmat_page, lmat_att = later(PAGE_SIZE), later(tq_att)
    ar = jnp.arange(tq_att)
    umat = (ar[:, None] <= ar[None, :]).astype(BF16)
    lane_head = jnp.arange(D_FOX) // HEAD_DIM
    seg = (lane_head[:, None] == lane_head[None, :]).astype(BF16)

    def pool_t(c):
        c = jnp.transpose(c, (0, 1, 3, 4, 2))
        return c.reshape(c.shape[0], c.shape[1], c.shape[2] * c.shape[3], c.shape[4])
    pools = tuple(pool_t(c) for c in (cache_sb_k, cache_sb_v, cache_fox_k, cache_fox_v))
    lf_pool = jnp.transpose(cache_fox_logf, (0, 3, 1, 2))

    n_grp = next(g for g in (8, 4, 2, 1) if n_pages % g == 0)
    n_head = min(4, n_pages)
    n_tail = next(g for g in (8, 7, 6, 5, 4, 3, 2, 1) if (n_pages - n_head) % g == 0)
    tm_in = 256 if tp % 256 == 0 else ATT_BLOCK
    n_p, n_s = bp * tp, bs * ts
    tm_out = 512 if n_p % 512 == 0 else ATT_BLOCK

    y_p = x_prompt
    y_s = x_sample.reshape(n_s, d)
    outs = {k: [] for k in ("rs_p", "rs_s", "skp", "svp", "sks", "svs", "fkp", "fvp", "flp", "fks", "fvs", "fls")}
    for i in range(depth):
        w = w_in[i]
        splits = np.cumsum([D_RET] * 4 + [D_SB] * 4 + [D_FOX] * 4).tolist()
        (w_qr, w_kr, w_vr, w_gr, w_qs, w_ks, w_vs, w_gs, w_qf, w_kf, w_vf, w_gf, w_fl) = jnp.split(w, splits, axis=1)
        w_row = jnp.concatenate([w_qr, w_vr, w_gr, w_qs, w_gs, w_qf, w_gf], axis=1).astype(BF16)
        w_flt = jnp.pad(w_fl.T, ((0, SUBLANES - H_FOX), (0, 0))).astype(BF16)
        w_t = [a.T.astype(BF16) for a in (w_kr, w_ks, w_vs, w_kf, w_vf)] + [w_flt]
        padc = lambda a: jnp.pad(a, ((0, 0), (0, PIECE - a.shape[1])))
        w_pieces = jnp.concatenate([padc(a) for a in (w_qr, w_kr, w_vr, w_gr, w_qs, w_ks, w_vs, w_gs,
                                                      w_qf, w_kf, w_vf, w_gf)], axis=1).astype(BF16)
        gn = g_norm[i][None, :]
        b_f8 = jnp.pad(b_f[i], (0, SUBLANES - H_FOX))[:, None]
        gq_row = jnp.tile(g_fox_q[i], H_FOX)[None, :]
        gk_row = jnp.tile(g_fox_k[i], H_FOX)[None, :]
        gk_col = g_fox_k[i][:, None]
        ggn = g_ret_gn[i][None, :]
        w_o = w_out[i].astype(BF16)
        w_op = (w_o[:D_RET], w_o[D_RET:D_RET + D_SB], w_o[D_RET + D_SB:],
                w_pe[i].astype(BF16), g_pe[i][None, :], w_pg[i].astype(BF16))

        (qr, vr, gr, qs, gs, qf, gf, krt, kst, vst, kft, vft, lft) = _inproj_prompt(
            y_p, gn, (cos_p, sin_p, cost_p, sint_p), b_f8, gk_col, [w_row] + w_t, tm_in)
        o_r, s_fin = _ret_prompt(qr, krt, vr, gr, dec_p, ggn)
        o_s = _sb_prompt(qs, kst, vst, gs, lmat_att, tq_att)
        o_f = _fox_prompt(qf, kft, vft, lft, gf, gq_row, seg, umat, tq_att)
        y_p = _outproj(y_p.reshape(n_p, d), o_r.reshape(n_p, D_RET), o_s.reshape(n_p, D_SB),
                       o_f.reshape(n_p, D_FOX), p_prompt[i].reshape(n_p, d_ple), w_op, tm_out).reshape(bp, tp, d)
        outs["rs_p"].append(s_fin)
        outs["skp"].append(kst); outs["svp"].append(vst)
        outs["fkp"].append(kft); outs["fvp"].append(vft); outs["flp"].append(lft[:, :H_FOX])

        proj, flt = _inproj_sample(y_s, gn, w_pieces, w_flt)
        flt_b = jnp.transpose(flt.reshape(SUBLANES, bs, ts), (1, 0, 2))
        o_r, s_new = _ret_sample(proj, state_ret[i], (cos_s, sin_s), dec_s, ggn, ts)
        o_f, kf_new, lf_new = _decode_fox(proj, flt_b, pools[2:], lf_pool, page_table, i,
                                          b_f8, gq_row, gk_row, seg, lmat_page, ts, n_grp)
        o_s, sb_acc, sb_car = _decode_sb(proj, pools[:2], page_table, i, lmat_page, ts, n_head, n_tail)
        if n_pages > n_head:
            o_s = lax.cond(jnp.max(sb_car) > SB_DEAD,
                           lambda st: _decode_sb(proj, pools[:2], page_table, i, lmat_page, ts, n_head, n_tail, st),
                           lambda st: o_s, (sb_acc, sb_car))
        y_s = _outproj(y_s, o_r, o_s, o_f, p_sample[i].reshape(n_s, d_ple), w_op, n_s)
        outs["rs_s"].append(s_new)
        piece = lambda k, wd: proj[:, k * PIECE:k * PIECE + wd]
        outs["sks"].append(piece(5, D_SB)); outs["svs"].append(piece(6, D_SB))
        outs["fks"].append(kf_new); outs["fvs"].append(piece(10, D_FOX))
        outs["fls"].append(lf_new[:, :H_FOX])

    st = lambda k: jnp.stack(outs[k], axis=0)
    kv_p = lambda k, h: jnp.transpose(st(k).reshape(depth, bp, h, HEAD_DIM, tp), (0, 1, 4, 2, 3))
    kv_s = lambda k, h: st(k).reshape(depth, bs, ts, h, HEAD_DIM)
    return (y_p, y_s.reshape(bs, ts, d), st("rs_p"), st("rs_s"),
            kv_p("skp", H_SB), kv_p("svp", H_SB), kv_s("sks", H_SB), kv_s("svs", H_SB),
            kv_p("fkp", H_FOX), kv_p("fvp", H_FOX), jnp.transpose(st("flp"), (0, 1, 3, 2)),
            kv_s("fks", H_FOX), kv_s("fvs", H_FOX), jnp.transpose(st("fls"), (0, 1, 3, 2)))
```

```python
import functools

import jax
import jax.numpy as jnp
import numpy as np
from jax import lax
from jax.experimental import pallas as pl
from jax.experimental.pallas import tpu as pltpu

F32 = jnp.float32
BF16 = jnp.bfloat16

HEAD_DIM = 64
H_RET, H_SB, H_FOX = 6, 5, 5
D_RET, D_SB, D_FOX = H_RET * HEAD_DIM, H_SB * HEAD_DIM, H_FOX * HEAD_DIM
PAGE_SIZE = 128
RET_CHUNK = 128
ATT_BLOCK = 128
ROW_CHUNK = 32
LANES = 128
SUBLANES = 8
ROPE_BASE = 10000.0
EPS = 1e-6
QK_SCALE = HEAD_DIM ** -0.5
SB_DEAD = -104.0
NEG_BIG = -1e30
PIECE = 384
VMEM_LIMIT = 48 * 1024 * 1024


def _dot(a, b):
    return jnp.dot(a, b, preferred_element_type=F32)


def _dot_nt(a, b):
    return lax.dot_general(a, b, (((1,), (1,)), ((), ())), preferred_element_type=F32)


def _dot_tn(a, b):
    return lax.dot_general(a, b, (((0,), (0,)), ((), ())), preferred_element_type=F32)


def _split2(x):
    hi = x.astype(BF16)
    lo = (x - hi.astype(F32)).astype(BF16)
    return hi, lo


def _split3(x):
    p1 = x.astype(BF16)
    r1 = x - p1.astype(F32)
    p2 = r1.astype(BF16)
    p3 = (r1 - p2.astype(F32)).astype(BF16)
    return p1, p2, p3


def _dot_exact2(x, m):
    hi, lo = _split2(x)
    return _dot(hi, m) + _dot(lo, m)


def _dot_exact3(x, m):
    p1, p2, p3 = _split3(x)
    return _dot(p1, m) + _dot(p2, m) + _dot(p3, m)


def _softplus(z):
    return jnp.maximum(z, 0.0) + jnp.log(1.0 + jnp.exp(-jnp.abs(z)))


def _log_sigmoid(x):
    return jnp.minimum(x, 0.0) - jnp.log1p(jnp.exp(-jnp.abs(x)))


def _silu(g):
    return g * jax.nn.sigmoid(g)


def _rotary_rows(x, cos, sin_signed):
    lane = lax.broadcasted_iota(jnp.int32, (x.shape[0], LANES), 1)
    first_half = (lane % HEAD_DIM) < (HEAD_DIM // 2)
    out = []
    for c in range(x.shape[1] // LANES):
        xs = x[:, c * LANES:(c + 1) * LANES]
        partner = jnp.where(first_half,
                            pltpu.roll(xs, LANES - HEAD_DIM // 2, axis=1),
                            pltpu.roll(xs, HEAD_DIM // 2, axis=1))
        out.append(xs * cos + partner * sin_signed)
    return jnp.concatenate(out, axis=1)


def _inproj_prompt_kernel(x_ref, g_ref, cos_ref, sin_ref, cost_ref, sint_ref, bf_ref, gk_ref,
                          wrow_ref, wkr_ref, wks_ref, wvs_ref, wkf_ref, wvf_ref, wfl_ref, *rest, first_of):
    (qr_ref, vr_ref, gr_ref, qs_ref, gs_ref, qf_ref, gf_ref,
     krt_ref, kst_ref, vst_ref, kft_ref, vft_ref, lft_ref) = rest[-13:]
    if first_of is not None:
        for ref in (kst_ref, vst_ref, kft_ref, vft_ref):
            if first_of > 1:
                ref[1:] = jnp.zeros((first_of - 1,) + ref.shape[1:], F32)
        kst_ref, vst_ref, kft_ref, vft_ref = (r.at[0] for r in (kst_ref, vst_ref, kft_ref, vft_ref))
    x = x_ref[...]
    ms = jnp.mean(x * x, axis=-1, keepdims=True)
    h = (x * lax.rsqrt(ms + EPS) * g_ref[...]).astype(BF16)

    row = _dot(h, wrow_ref[...])
    qr_ref[...] = _rotary_rows(row[:, 0:D_RET], cos_ref[...], sin_ref[...])
    vr_ref[...] = row[:, D_RET:2 * D_RET]
    gr_ref[...] = _silu(row[:, 2 * D_RET:3 * D_RET])
    o = 3 * D_RET
    qs_ref[...] = row[:, o:o + D_SB]
    gs_ref[...] = _silu(row[:, o + D_SB:o + 2 * D_SB])
    qf_ref[...] = row[:, o + 2 * D_SB:o + 3 * D_SB]
    gf_ref[...] = _silu(row[:, o + 3 * D_SB:o + 4 * D_SB])

    half = HEAD_DIM // 2
    krt = _dot_nt(wkr_ref[...], h)
    cost, sint = cost_ref[...], sint_ref[...]
    for hd in range(H_RET):
        x1 = krt[hd * HEAD_DIM:hd * HEAD_DIM + half]
        x2 = krt[hd * HEAD_DIM + half:(hd + 1) * HEAD_DIM]
        krt_ref[hd * HEAD_DIM:hd * HEAD_DIM + half, :] = (x1 * cost - x2 * sint) * QK_SCALE
        krt_ref[hd * HEAD_DIM + half:(hd + 1) * HEAD_DIM, :] = (x1 * sint + x2 * cost) * QK_SCALE

    kst_ref[...] = _dot_nt(wks_ref[...], h)
    vst_ref[...] = _dot_nt(wvs_ref[...], h)
    vft_ref[...] = _dot_nt(wvf_ref[...], h)
    kft = _dot_nt(wkf_ref[...], h)
    gk = gk_ref[...]
    for hd in range(H_FOX):
        xh = kft[hd * HEAD_DIM:(hd + 1) * HEAD_DIM]
        msh = jnp.mean(xh * xh, axis=0, keepdims=True)
        kft_ref[hd * HEAD_DIM:(hd + 1) * HEAD_DIM, :] = xh * lax.rsqrt(msh + EPS) * gk
    lft_ref[...] = _log_sigmoid(_dot_nt(wfl_ref[...], h) + bf_ref[...])


def _inproj_prompt(x, g_norm, tabs, b_f8, gk_col, w, tm, layer, depth, stacked):
    bsz, t, d = x.shape
    n_row = 3 * D_RET + 4 * D_SB
    grid = (bsz, t // tm)
    row_spec = lambda n: pl.BlockSpec((None, tm, n), lambda b, i: (b, i, 0))
    col_spec = lambda n: pl.BlockSpec((None, n, tm), lambda b, i: (b, 0, i))
    full = lambda a: pl.BlockSpec(a.shape, lambda b, i: (0,) * a.ndim)
    cos, sin, cost, sint = tabs
    in_specs = [
        row_spec(d), full(g_norm),
        pl.BlockSpec((tm, LANES), lambda b, i: (i, 0)), pl.BlockSpec((tm, LANES), lambda b, i: (i, 0)),
        pl.BlockSpec((HEAD_DIM // 2, tm), lambda b, i: (0, i)), pl.BlockSpec((HEAD_DIM // 2, tm), lambda b, i: (0, i)),
        full(b_f8), full(gk_col),
    ] + [full(a) for a in w]
    rs = lambda n: jax.ShapeDtypeStruct((bsz, t, n), F32)
    cs = lambda n: jax.ShapeDtypeStruct((bsz, n, t), F32)
    stk = jax.ShapeDtypeStruct((depth, bsz, D_SB, t), F32)
    aliases, extra = {}, []
    if stacked is None:
        stk_spec = pl.BlockSpec((depth, None, D_SB, tm), lambda b, i: (0, b, 0, i))
    else:
        stk_spec = pl.BlockSpec((None, None, D_SB, tm), lambda b, i: (layer, b, 0, i))
        extra = list(stacked)
        in_specs += [pl.BlockSpec(memory_space=pl.ANY)] * len(extra)
        first_in, first_out = len(in_specs) - len(extra), 8
        aliases = {first_in + k: first_out + k for k in range(len(extra))}
    out_shape = [rs(D_RET), rs(D_RET), rs(D_RET), rs(D_SB), rs(D_SB), rs(D_FOX), rs(D_FOX),
                 cs(D_RET), stk, stk, stk, stk, cs(SUBLANES)]
    out_specs = [row_spec(D_RET)] * 3 + [row_spec(D_SB)] * 4 + \
                [col_spec(D_RET)] + [stk_spec] * 4 + [col_spec(SUBLANES)]
    return pl.pallas_call(
        functools.partial(_inproj_prompt_kernel, first_of=depth if stacked is None else None),
        grid=grid, in_specs=in_specs, out_specs=out_specs, out_shape=out_shape,
        input_output_aliases=aliases,
        compiler_params=pltpu.CompilerParams(dimension_semantics=("parallel", "parallel"),
                                             vmem_limit_bytes=VMEM_LIMIT),
        name="inproj_prompt",
    )(x, g_norm, cos, sin, cost, sint, b_f8, gk_col, *w, *extra)


def _ret_prompt_kernel(qr_ref, krt_ref, vr_ref, gr_ref, dmask_ref, qdec_ref, kdec_ref, cdec_ref, ggn_ref,
                       o_ref, sfin_ref, s_scr):
    c = pl.program_id(1)

    @pl.when(c == 0)
    def _():
        s_scr[...] = jnp.zeros_like(s_scr)

    for hd in range(H_RET):
        sl = slice(hd * HEAD_DIM, (hd + 1) * HEAD_DIM)
        q = qr_ref[:, sl].astype(BF16)
        kt = krt_ref[sl, :]
        v = vr_ref[:, sl].astype(BF16)
        sc = _dot(q, kt.astype(BF16)) * dmask_ref[hd]
        inner = _dot(sc.astype(BF16), v)
        s_old = s_scr[hd]
        cross = _dot(q, s_old.astype(BF16)) * qdec_ref[hd]
        o = inner + cross
        s_scr[hd] = s_old * cdec_ref[hd] + _dot((kt * kdec_ref[hd]).astype(BF16), v)
        mu = jnp.mean(o, axis=-1, keepdims=True)
        dev = o - mu
        var = jnp.mean(dev * dev, axis=-1, keepdims=True)
        o_ref[:, sl] = dev * lax.rsqrt(var + EPS) * ggn_ref[:, sl] * gr_ref[:, sl]

    @pl.when(c == pl.num_programs(1) - 1)
    def _():
        sfin_ref[...] = s_scr[...]


def _ret_prompt(qr, krt, vr, gr, dec, ggn):
    bsz, t, _ = qr.shape
    blk = RET_CHUNK
    dmask, qdec, kdec, cdec = dec
    row = pl.BlockSpec((None, blk, D_RET), lambda b, c: (b, c, 0))
    full = lambda a: pl.BlockSpec(a.shape, lambda b, c: (0,) * a.ndim)
    return pl.pallas_call(
        _ret_prompt_kernel, grid=(bsz, t // blk),
        in_specs=[row, pl.BlockSpec((None, D_RET, blk), lambda b, c: (b, 0, c)), row, row,
                  full(dmask), full(qdec), full(kdec), full(cdec), full(ggn)],
        out_specs=[row, pl.BlockSpec((None, H_RET, HEAD_DIM, HEAD_DIM), lambda b, c: (b, 0, 0, 0))],
        out_shape=[jax.ShapeDtypeStruct((bsz, t, D_RET), F32),
                   jax.ShapeDtypeStruct((bsz, H_RET, HEAD_DIM, HEAD_DIM), F32)],
        scratch_shapes=[pltpu.VMEM((H_RET, HEAD_DIM, HEAD_DIM), F32)],
        compiler_params=pltpu.CompilerParams(dimension_semantics=("parallel", "arbitrary"),
                                             vmem_limit_bytes=VMEM_LIMIT),
        name="ret_prompt",
    )(qr, krt, vr, gr, dmask, qdec, kdec, cdec, ggn)


def _sb_prompt_kernel(q_ref, kt_ref, vt_ref, g_ref, lmat_ref, o_ref,
                      q_scr, car_scr, acc_scr, z_scr, ls_scr, hi_scr, lo_scr, a_scr, tot_scr):
    i = pl.program_id(1)
    tq = q_ref.shape[0]
    reps = tq // LANES
    for hd in range(H_SB):
        sl = slice(hd * HEAD_DIM, (hd + 1) * HEAD_DIM)
        q_scr[hd] = (q_ref[:, sl] * QK_SCALE).astype(BF16)
    car_scr[...] = jnp.zeros_like(car_scr)
    acc_scr[...] = jnp.zeros_like(acc_scr)

    def block(j, diag):
        off = pl.multiple_of(j * tq, tq)
        lmat = lmat_ref[...]
        for hd in range(H_SB):
            sl = slice(hd * HEAD_DIM, (hd + 1) * HEAD_DIM)
            z_scr[hd] = _dot(q_scr[hd], kt_ref[sl, pl.ds(off, tq)].astype(BF16))
        chunk_masks = []
        for r in range(0, tq, ROW_CHUNK):
            if diag:
                rowi = lax.broadcasted_iota(jnp.int32, (ROW_CHUNK, tq), 0) + r
                coli = lax.broadcasted_iota(jnp.int32, (ROW_CHUNK, tq), 1)
                chunk_masks.append(coli < rowi)
            else:
                chunk_masks.append(None)
        for hd in range(H_SB):
            for ci, r in enumerate(range(0, tq, ROW_CHUNK)):
                rows = slice(r, r + ROW_CHUNK)
                z = z_scr[hd, rows, :]
                sp = _softplus(z)
                ls_scr[hd, rows, :] = z - sp
                if diag:
                    sp = jnp.where(chunk_masks[ci], sp, 0.0)
                hi, lo = _split2(sp)
                hi_scr[hd, rows, :] = hi
                lo_scr[hd, rows, :] = lo
                tot = jnp.sum(sp, axis=-1, keepdims=True)
                tot_scr[hd, rows, :] = jnp.broadcast_to(tot, (ROW_CHUNK, LANES))
        for hd in range(H_SB):
            z_scr[hd] = _dot(hi_scr[hd], lmat) + _dot(lo_scr[hd], lmat)
        for hd in range(H_SB):
            for ci, r in enumerate(range(0, tq, ROW_CHUNK)):
                rows = slice(r, r + ROW_CHUNK)
                carry = car_scr[hd, rows, :]
                a = jnp.exp(ls_scr[hd, rows, :] - z_scr[hd, rows, :] + jnp.tile(carry, (1, reps)))
                if diag:
                    a = jnp.where(chunk_masks[ci], a, 0.0)
                a_scr[hd, rows, :] = a.astype(BF16)
                car_scr[hd, rows, :] = carry - tot_scr[hd, rows, :]
        for hd in range(H_SB):
            sl = slice(hd * HEAD_DIM, (hd + 1) * HEAD_DIM)
            acc_scr[hd] += _dot_nt(a_scr[hd], vt_ref[sl, pl.ds(off, tq)].astype(BF16))
        return jnp.max(car_scr[...])

    cmax = block(i, True)

    def cond(st):
        j, cmax = st
        return jnp.logical_and(j >= 0, cmax > SB_DEAD)

    def body(st):
        j, _ = st
        return j - 1, block(j, False)

    lax.while_loop(cond, body, (i - 1, cmax))
    for hd in range(H_SB):
        sl = slice(hd * HEAD_DIM, (hd + 1) * HEAD_DIM)
        o_ref[:, sl] = acc_scr[hd] * g_ref[:, sl]


def _sb_prompt(qs, kst, vst, gs, lmat, tq, layer):
    bsz, t, _ = qs.shape
    row = pl.BlockSpec((None, tq, D_SB), lambda b, i: (b, i, 0))
    seq = pl.BlockSpec((None, None, D_SB, t), lambda b, i: (layer, b, 0, 0))
    return pl.pallas_call(
        _sb_prompt_kernel, grid=(bsz, t // tq),
        in_specs=[row, seq, seq, row, pl.BlockSpec(lmat.shape, lambda b, i: (0, 0))],
        out_specs=row, out_shape=jax.ShapeDtypeStruct((bsz, t, D_SB), F32),
        scratch_shapes=[pltpu.VMEM((H_SB, tq, HEAD_DIM), BF16), pltpu.VMEM((H_SB, tq, LANES), F32),
                        pltpu.VMEM((H_SB, tq, HEAD_DIM), F32),
                        pltpu.VMEM((H_SB, tq, tq), F32), pltpu.VMEM((H_SB, tq, tq), F32),
                        pltpu.VMEM((H_SB, tq, tq), BF16), pltpu.VMEM((H_SB, tq, tq), BF16),
                        pltpu.VMEM((H_SB, tq, tq), BF16), pltpu.VMEM((H_SB, tq, LANES), F32)],
        compiler_params=pltpu.CompilerParams(dimension_semantics=("parallel", "arbitrary"),
                                             vmem_limit_bytes=VMEM_LIMIT),
        name="sb_prompt",
    )(qs, kst, vst, gs, lmat)


def _fox_prompt_kernel(q_ref, kt_ref, vt_ref, lft_ref, g_ref, gq_ref, seg_ref, umat_ref, o_ref,
                       c_scr, q_scr, m_scr, acc_scr, s_scr, p_scr, a_scr):
    i = pl.program_id(1)
    tq = q_ref.shape[0]
    t = kt_ref.shape[1]
    cw = umat_ref.shape[0]

    @pl.when(i == 0)
    def _():
        umat = umat_ref[...]
        carry = jnp.zeros((SUBLANES, 1), F32)
        for c in range(t // cw):
            cs = _dot_exact3(lft_ref[:, c * cw:(c + 1) * cw], umat) + carry
            c_scr[:, c * cw:(c + 1) * cw] = cs
            carry = cs[:, cw - 1:cw]

    qraw = q_ref[...]
    ssq = _dot_exact2(qraw * qraw, seg_ref[...])
    qn = qraw * lax.rsqrt(ssq * (1.0 / HEAD_DIM) + EPS) * gq_ref[...] * QK_SCALE
    for hd in range(H_FOX):
        q_scr[hd] = qn[:, hd * HEAD_DIM:(hd + 1) * HEAD_DIM].astype(BF16)
    m_scr[...] = jnp.full_like(m_scr, NEG_BIG)
    acc_scr[...] = jnp.zeros_like(acc_scr)
    rowi = lax.broadcasted_iota(jnp.int32, (tq, tq), 0)
    coli = lax.broadcasted_iota(jnp.int32, (tq, tq), 1)
    diag_mask = coli <= rowi

    ones_rows = jnp.ones((HEAD_DIM, tq), BF16)

    def block(j, mask):
        off = pl.multiple_of(j * tq, tq)
        for hd in range(H_FOX):
            sl = slice(hd * HEAD_DIM, (hd + 1) * HEAD_DIM)
            kt = kt_ref[sl, pl.ds(off, tq)].astype(BF16)
            s = _dot(q_scr[hd], kt) - c_scr[hd:hd + 1, pl.ds(off, tq)]
            if mask is not None:
                s = jnp.where(mask, s, NEG_BIG)
            s_scr[hd] = s
        for hd in range(H_FOX):
            for r in range(0, tq, ROW_CHUNK):
                rows = slice(r, r + ROW_CHUNK)
                s = s_scr[hd, rows, :]
                m_old = m_scr[hd, rows, :]
                mx = jnp.max(s, axis=-1, keepdims=True)
                m_new = jnp.maximum(m_old, jnp.broadcast_to(mx, (ROW_CHUNK, LANES)))
                p_scr[hd, rows, :] = jnp.exp(s - jnp.tile(m_new, (1, tq // LANES))).astype(BF16)
                a_scr[hd, rows, :] = jnp.exp(m_old - m_new)
                m_scr[hd, rows, :] = m_new
        for hd in range(H_FOX):
            sl = slice(hd * HEAD_DIM, (hd + 1) * HEAD_DIM)
            vt = jnp.concatenate([vt_ref[sl, pl.ds(off, tq)].astype(BF16), ones_rows], axis=0)
            acc_scr[hd] = a_scr[hd] * acc_scr[hd] + _dot_nt(p_scr[hd], vt)

    def body(j, carry):
        block(j, None)
        return carry

    lax.fori_loop(0, i, body, 0)
    block(i, diag_mask)
    for hd in range(H_FOX):
        sl = slice(hd * HEAD_DIM, (hd + 1) * HEAD_DIM)
        acc = acc_scr[hd]
        o_ref[:, sl] = acc[:, :HEAD_DIM] / acc[:, HEAD_DIM:HEAD_DIM + 1] * g_ref[:, sl]


def _fox_prompt(qf, kft, vft, lft, gf, gq_row, seg, umat, tq, layer):
    bsz, t, _ = qf.shape
    row = pl.BlockSpec((None, tq, D_FOX), lambda b, i: (b, i, 0))
    seq = pl.BlockSpec((None, None, D_FOX, t), lambda b, i: (layer, b, 0, 0))
    full = lambda a: pl.BlockSpec(a.shape, lambda b, i: (0,) * a.ndim)
    return pl.pallas_call(
        _fox_prompt_kernel, grid=(bsz, t // tq),
        in_specs=[row, seq, seq, pl.BlockSpec((None, SUBLANES, t), lambda b, i: (b, 0, 0)), row,
                  full(gq_row), full(seg), full(umat)],
        out_specs=row, out_shape=jax.ShapeDtypeStruct((bsz, t, D_FOX), F32),
        scratch_shapes=[pltpu.VMEM((SUBLANES, t), F32), pltpu.VMEM((H_FOX, tq, HEAD_DIM), BF16),
                        pltpu.VMEM((H_FOX, tq, LANES), F32), pltpu.VMEM((H_FOX, tq, 2 * HEAD_DIM), F32),
                        pltpu.VMEM((H_FOX, tq, tq), F32), pltpu.VMEM((H_FOX, tq, tq), BF16),
                        pltpu.VMEM((H_FOX, tq, LANES), F32)],
        compiler_params=pltpu.CompilerParams(dimension_semantics=("parallel", "arbitrary"),
                                             vmem_limit_bytes=VMEM_LIMIT),
        name="fox_prompt",
    )(qf, kft, vft, lft, gf, gq_row, seg, umat)


def _outproj_kernel(y_ref, or_ref, os_ref, of_ref, p_ref, wor_ref, wos_ref, wof_ref, wpe_ref, gpe_ref, wpg_ref,
                    o_ref):
    m = (_dot(or_ref[...].astype(BF16), wor_ref[...]) + _dot(os_ref[...].astype(BF16), wos_ref[...])
         + _dot(of_ref[...].astype(BF16), wof_ref[...]))
    y1 = y_ref[...] + m
    ms = jnp.mean(y1 * y1, axis=-1, keepdims=True)
    n = (y1 * lax.rsqrt(ms + EPS) * gpe_ref[...]).astype(BF16)
    gate = jax.nn.sigmoid(_dot(n, wpg_ref[...]))
    pe = _dot(p_ref[...].astype(BF16), wpe_ref[...])
    o_ref[...] = y1 + pe * gate


def _outproj(y, o_r, o_s, o_f, p, w, tm):
    n, d = y.shape
    row = lambda a: pl.BlockSpec((tm, a.shape[1]), lambda i: (i, 0))
    full = lambda a: pl.BlockSpec(a.shape, lambda i: (0,) * a.ndim)
    return pl.pallas_call(
        _outproj_kernel, grid=(n // tm,),
        in_specs=[row(y), row(o_r), row(o_s), row(o_f), row(p)] + [full(a) for a in w],
        out_specs=row(y), out_shape=jax.ShapeDtypeStruct((n, d), F32),
        compiler_params=pltpu.CompilerParams(dimension_semantics=("parallel",), vmem_limit_bytes=VMEM_LIMIT),
        name="outproj",
    )(y, o_r, o_s, o_f, p, *w)


def _inproj_sample_kernel(x_ref, g_ref, wrow_ref, wkr_ref, wks_ref, wvs_ref, wkf_ref, wvf_ref, wfl_ref,
                          o_ref, flt_ref):
    x = x_ref[...]
    ms = jnp.mean(x * x, axis=-1, keepdims=True)
    h = (x * lax.rsqrt(ms + EPS) * g_ref[...]).astype(BF16)
    row = _dot(h, wrow_ref[...])
    o_ref[...] = jnp.zeros_like(o_ref)

    def put(slot, val):
        o_ref[:, slot * PIECE:slot * PIECE + val.shape[1]] = val

    put(0, row[:, 0:D_RET])
    put(2, row[:, D_RET:2 * D_RET])
    put(3, _silu(row[:, 2 * D_RET:3 * D_RET]))
    o = 3 * D_RET
    put(4, row[:, o:o + D_SB])
    put(7, _silu(row[:, o + D_SB:o + 2 * D_SB]))
    put(8, row[:, o + 2 * D_SB:o + 3 * D_SB])
    put(11, _silu(row[:, o + 3 * D_SB:o + 4 * D_SB]))
    put(1, _dot_nt(h, wkr_ref[...]))
    put(5, _dot_nt(h, wks_ref[...]))
    put(6, _dot_nt(h, wvs_ref[...]))
    put(9, _dot_nt(h, wkf_ref[...]))
    put(10, _dot_nt(h, wvf_ref[...]))
    flt_ref[...] = _dot_nt(wfl_ref[...], h)


def _inproj_sample(x, g_norm, w):
    n, d = x.shape
    ncol = 12 * PIECE
    full = lambda a: pl.BlockSpec(a.shape, lambda i: (0,) * a.ndim)
    return pl.pallas_call(
        _inproj_sample_kernel, grid=(1,),
        in_specs=[full(x), full(g_norm)] + [full(a) for a in w],
        out_specs=[pl.BlockSpec((n, ncol), lambda i: (0, 0)), pl.BlockSpec((SUBLANES, n), lambda i: (0, 0))],
        out_shape=[jax.ShapeDtypeStruct((n, ncol), F32), jax.ShapeDtypeStruct((SUBLANES, n), F32)],
        compiler_params=pltpu.CompilerParams(dimension_semantics=("arbitrary",), vmem_limit_bytes=VMEM_LIMIT),
        name="inproj_sample",
    )(x, g_norm, *w)


def _ret_sample_kernel(q_ref, k_ref, v_ref, g_ref, s0_ref, cos_ref, sin_ref, dmask_ref, qdec_ref, kdec_ref,
                       cdec_ref, ggn_ref, o_ref, s_ref):
    cos, sin = cos_ref[...], sin_ref[...]
    qr = _rotary_rows(q_ref[...], cos, sin)
    kr = _rotary_rows(k_ref[...], cos, sin) * QK_SCALE
    for hd in range(H_RET):
        sl = slice(hd * HEAD_DIM, (hd + 1) * HEAD_DIM)
        q = qr[:, sl].astype(BF16)
        k = kr[:, sl]
        v = v_ref[:, sl].astype(BF16)
        sc = _dot_nt(q, k.astype(BF16)) * dmask_ref[hd]
        inner = _dot(sc.astype(BF16), v)
        s_old = s0_ref[hd]
        cross = _dot(q, s_old.astype(BF16)) * qdec_ref[hd]
        o = inner + cross
        s_ref[hd] = s_old * cdec_ref[hd] + _dot_tn((k * kdec_ref[hd]).astype(BF16), v)
        mu = jnp.mean(o, axis=-1, keepdims=True)
        dev = o - mu
        var = jnp.mean(dev * dev, axis=-1, keepdims=True)
        y = dev * lax.rsqrt(var + EPS) * ggn_ref[:, sl]
        o_ref[:, sl] = y * g_ref[:, sl]


def _ret_sample(proj, s0, tabs, dec, ggn, ts):
    n = proj.shape[0]
    bsz = n // ts
    cos, sin = tabs
    dmask, qdec, kdec, cdec = dec
    piece = lambda k: pl.BlockSpec((ts, PIECE), lambda b, k=k: (b, k))
    full = lambda a: pl.BlockSpec(a.shape, lambda b: (0,) * a.ndim)
    st = pl.BlockSpec((None, H_RET, HEAD_DIM, HEAD_DIM), lambda b: (b, 0, 0, 0))
    return pl.pallas_call(
        _ret_sample_kernel, grid=(bsz,),
        in_specs=[piece(0), piece(1), piece(2), piece(3), st, full(cos), full(sin),
                  full(dmask), full(qdec), full(kdec), full(cdec), full(ggn)],
        out_specs=[pl.BlockSpec((ts, D_RET), lambda b: (b, 0)), st],
        out_shape=[jax.ShapeDtypeStruct((n, D_RET), F32),
                   jax.ShapeDtypeStruct((bsz, H_RET, HEAD_DIM, HEAD_DIM), F32)],
        compiler_params=pltpu.CompilerParams(dimension_semantics=("parallel",), vmem_limit_bytes=VMEM_LIMIT),
        name="ret_sample",
    )(proj, proj, proj, proj, s0, cos, sin, dmask, qdec, kdec, cdec, ggn)


def _head_rows(x, ts):
    rows = H_SB * ts
    r = lax.broadcasted_iota(jnp.int32, (rows, D_SB), 0)
    c = lax.broadcasted_iota(jnp.int32, (rows, D_SB), 1)
    tiled = jnp.concatenate([x] * H_SB, axis=0)
    return jnp.where(r // ts == c // HEAD_DIM, tiled, 0.0)


def _head_cols(x, ts):
    c = lax.broadcasted_iota(jnp.int32, (ts, D_SB), 1)
    out = jnp.zeros((ts, D_SB), F32)
    for hd in range(H_SB):
        out = out + jnp.where(c // HEAD_DIM == hd, x[hd * ts:(hd + 1) * ts], 0.0)
    return out


def _new_token_masks(ts, strict):
    rows = H_SB * ts
    tpos = lax.broadcasted_iota(jnp.int32, (rows, PAGE_SIZE), 0) % ts
    coli = lax.broadcasted_iota(jnp.int32, (rows, PAGE_SIZE), 1)
    return coli < tpos if strict else coli <= tpos


def _sb_blocks(zs, v_dots, lmat, acc, car):
    for (z, mask), v_dot in zip(zs, v_dots):
        sp = _softplus(z)
        ls = z - sp
        if mask is not None:
            sp = jnp.where(mask, sp, 0.0)
        a = jnp.exp(ls - _dot_exact2(sp, lmat) + car)
        if mask is not None:
            a = jnp.where(mask, a, 0.0)
        acc = acc + v_dot(a.astype(BF16))
        car = car - jnp.sum(sp, axis=-1, keepdims=True)
    return acc, car


def _decode_sb_kernel(pt_ref, qs_ref, ks_ref, vs_ref, gs_ref, lmat_ref, *rest, n_grp, ts, head):
    if head:
        pages = rest[:2 * n_grp]
        os_ref, acco_ref, caro_ref = rest[2 * n_grp:]
    else:
        acci_ref, cari_ref = rest[:2]
        pages = rest[2:2 + 2 * n_grp]
        os_ref, qbs_scr, acc_scr, car_scr = rest[2 + 2 * n_grp:]
    lmat = lmat_ref[...]
    order = list(range(n_grp - 1, -1, -1))

    def page_blocks(qbs):
        zs, dots = [], []
        for gi in order:
            kt_ref, vt_ref = pages[2 * gi:2 * gi + 2]
            zs.append((_dot(qbs, kt_ref[...].astype(BF16)), None))
            dots.append(lambda a, r=vt_ref: _dot_nt(a, r[...].astype(BF16)))
        return zs, dots

    qbs_new = _head_rows(qs_ref[:, :D_SB] * QK_SCALE, ts).astype(BF16)
    if head:
        pad = jnp.zeros((PAGE_SIZE - ts, D_SB), F32)
        k_new = jnp.concatenate([ks_ref[:, :D_SB], pad], axis=0).astype(BF16)
        v_new = jnp.concatenate([vs_ref[:, :D_SB], pad], axis=0).astype(BF16)
        zs, dots = page_blocks(qbs_new)
        acc, car = _sb_blocks([(_dot_nt(qbs_new, k_new), _new_token_masks(ts, True))] + zs,
                              [lambda a: _dot(a, v_new)] + dots, lmat,
                              jnp.zeros((H_SB * ts, D_SB), F32), jnp.zeros((H_SB * ts, 1), F32))
        acco_ref[...] = acc
        caro_ref[...] = jnp.broadcast_to(car, caro_ref.shape)
        os_ref[...] = _head_cols(acc, ts) * gs_ref[:, :D_SB]
    else:
        s = pl.program_id(1)

        @pl.when(s == 0)
        def _():
            qbs_scr[...] = qbs_new
            acc_scr[...] = acci_ref[...]
            car_scr[...] = cari_ref[:, 0:1]

        @pl.when(jnp.max(car_scr[...]) > SB_DEAD)
        def _():
            zs, dots = page_blocks(qbs_scr[...])
            acc, car = _sb_blocks(zs, dots, lmat, acc_scr[...], car_scr[...])
            acc_scr[...] = acc
            car_scr[...] = car

        @pl.when(s == pl.num_programs(1) - 1)
        def _():
            os_ref[...] = _head_cols(acc_scr[...], ts) * gs_ref[:, :D_SB]


def _decode_sb(proj, pools, page_table, layer, lmat, ts, n_head, n_grp, state=None):
    n = proj.shape[0]
    bsz, n_pages = page_table.shape
    rows = H_SB * ts
    head = state is None
    per_step = n_head if head else n_grp
    n_steps = 1 if head else (n_pages - n_head) // n_grp
    last = n_pages - 1 if head else n_pages - n_head - 1
    piece = lambda k: pl.BlockSpec((ts, PIECE), lambda b, s, pt, k=k: (b, k))

    def page_spec(gi):
        return pl.BlockSpec((None, None, D_SB, PAGE_SIZE),
                            lambda b, s, pt, gi=gi: (layer, pt[b, last - s * per_step - (per_step - 1 - gi)], 0, 0))

    page_specs = [page_spec(gi) for gi in range(per_step) for _ in range(2)]
    page_args = [p for _ in range(per_step) for p in pools]
    out_row = pl.BlockSpec((ts, D_SB), lambda b, s, pt: (b, 0))
    st_acc = pl.BlockSpec((rows, D_SB), lambda b, s, pt: (b, 0))
    st_car = pl.BlockSpec((rows, LANES), lambda b, s, pt: (b, 0))
    in_specs = [piece(4), piece(5), piece(6), piece(7), pl.BlockSpec(lmat.shape, lambda b, s, pt: (0, 0))]
    args = [proj] * 4 + [lmat]
    if head:
        out_specs = [out_row, st_acc, st_car]
        out_shape = [jax.ShapeDtypeStruct((n, D_SB), F32), jax.ShapeDtypeStruct((bsz * rows, D_SB), F32),
                     jax.ShapeDtypeStruct((bsz * rows, LANES), F32)]
        scratch = []
    else:
        in_specs += [st_acc, st_car]
        args += list(state)
        out_specs = out_row
        out_shape = jax.ShapeDtypeStruct((n, D_SB), F32)
        scratch = [pltpu.VMEM((rows, D_SB), BF16), pltpu.VMEM((rows, D_SB), F32), pltpu.VMEM((rows, 1), F32)]
    grid_spec = pltpu.PrefetchScalarGridSpec(
        num_scalar_prefetch=1, grid=(bsz, n_steps), in_specs=in_specs + page_specs, out_specs=out_specs,
        scratch_shapes=scratch)
    return pl.pallas_call(
        functools.partial(_decode_sb_kernel, n_grp=per_step, ts=ts, head=head), grid_spec=grid_spec,
        out_shape=out_shape,
        compiler_params=pltpu.CompilerParams(dimension_semantics=("parallel", "arbitrary"),
                                             vmem_limit_bytes=VMEM_LIMIT),
        name="decode_sb_head" if head else "decode_sb_tail",
    )(page_table, *args, *page_args)


def _decode_fox_kernel(pt_ref, qf_ref, kf_ref, vf_ref, gf_ref, flt_ref, bf_ref, gq_ref, gk_ref, seg_ref, lmat_ref,
                       *rest, n_grp, ts):
    pages = rest[:3 * n_grp]
    of_ref, kfn_ref, lfn_ref = rest[3 * n_grp:3 * n_grp + 3]
    qbf_scr, acc_scr, m_scr, l_scr, rcar_scr = rest[3 * n_grp + 3:]
    b = pl.program_id(0)
    s = pl.program_id(1)
    n_steps = pl.num_programs(1)
    lmat = lmat_ref[...]

    def rep_heads(x):
        return jnp.concatenate([jnp.broadcast_to(x[hd:hd + 1], (ts, x.shape[1])) for hd in range(H_FOX)], axis=0)

    def fox_blocks(scs, v_dots):
        scs = [sc if mask is None else jnp.where(mask, sc, NEG_BIG) for sc, mask in scs]
        m_old = m_scr[...]
        m_new = m_old
        for sc in scs:
            m_new = jnp.maximum(m_new, jnp.max(sc, axis=-1, keepdims=True))
        alpha = jnp.exp(m_old - m_new)
        l = alpha * l_scr[...]
        acc = alpha * acc_scr[...]
        for sc, v_dot in zip(scs, v_dots):
            p = jnp.exp(sc - m_new)
            l = l + jnp.sum(p, axis=-1, keepdims=True)
            acc = acc + v_dot(p.astype(BF16))
        l_scr[...] = l
        acc_scr[...] = acc
        m_scr[...] = m_new

    @pl.when(s == 0)
    def _():
        pad = jnp.zeros((PAGE_SIZE - ts, D_FOX), F32)
        seg = seg_ref[...]
        qraw = qf_ref[:, :D_FOX]
        qn = qraw * lax.rsqrt(_dot_exact2(qraw * qraw, seg) * (1.0 / HEAD_DIM) + EPS) * gq_ref[...] * QK_SCALE
        qbf = _head_rows(qn, ts).astype(BF16)
        qbf_scr[...] = qbf
        kraw = kf_ref[:, :D_FOX]
        kn = kraw * lax.rsqrt(_dot_exact2(kraw * kraw, seg) * (1.0 / HEAD_DIM) + EPS) * gk_ref[...]
        kfn_ref[...] = kn
        lf_new = _log_sigmoid(flt_ref[...] + bf_ref[...])
        lfn_ref[...] = lf_new
        lf_pad = jnp.concatenate([lf_new, jnp.zeros((SUBLANES, PAGE_SIZE - ts), F32)], axis=1)
        lf_rows = rep_heads(lf_pad)
        cn = jnp.sum(lf_rows, axis=-1, keepdims=True) - _dot_exact3(lf_rows, lmat)
        acc_scr[...] = jnp.zeros_like(acc_scr)
        l_scr[...] = jnp.zeros_like(l_scr)
        m_scr[...] = jnp.full_like(m_scr, NEG_BIG)
        rcar_scr[...] = jnp.zeros_like(rcar_scr)
        kn_pad = jnp.concatenate([kn, pad], axis=0).astype(BF16)
        vf_new = jnp.concatenate([vf_ref[:, :D_FOX], pad], axis=0).astype(BF16)
        fox_blocks([(_dot_nt(qbf, kn_pad) - cn, _new_token_masks(ts, False))], [lambda p: _dot(p, vf_new)])

    order = list(range(n_grp - 1, -1, -1))
    rows = H_FOX * ts
    lf_rows = []
    for gi in order:
        page = pt_ref[b, (n_steps - 1 - s) * n_grp + gi]
        lf_pg = pages[3 * gi + 2][:, pl.ds(page % SUBLANES, 1), :]
        lf_rows.append(jnp.concatenate([jnp.broadcast_to(lf_pg[hd], (ts, PAGE_SIZE)) for hd in range(H_FOX)],
                                       axis=0))
    later = _dot_exact3(jnp.concatenate(lf_rows, axis=0), lmat)
    rcar = rcar_scr[...]
    biases = []
    for k, lr in enumerate(lf_rows):
        biases.append(later[k * rows:(k + 1) * rows] + rcar)
        rcar = rcar + jnp.sum(lr, axis=-1, keepdims=True)
    rcar_scr[...] = rcar
    kt_all = jnp.concatenate([pages[3 * gi][...].astype(BF16) for gi in order], axis=1)
    vt_all = jnp.concatenate([pages[3 * gi + 1][...].astype(BF16) for gi in order], axis=1)
    sc = _dot(qbf_scr[...], kt_all) + jnp.concatenate(biases, axis=1)
    fox_blocks([(sc, None)], [lambda p: _dot_nt(p, vt_all)])

    @pl.when(s == n_steps - 1)
    def _():
        of_ref[...] = _head_cols(acc_scr[...] / l_scr[...], ts) * gf_ref[:, :D_FOX]


def _decode_fox(proj, flt, pools, lf_pool, page_table, layer, b_f8, gq_row, gk_row, seg, lmat, ts, n_grp):
    n = proj.shape[0]
    bsz, n_pages = page_table.shape
    n_steps = n_pages // n_grp
    rows = H_FOX * ts
    piece = lambda k: pl.BlockSpec((ts, PIECE), lambda b, s, pt, k=k: (b, k))
    full = lambda a: pl.BlockSpec(a.shape, lambda b, s, pt: (0,) * a.ndim)

    def page_spec(gi):
        return pl.BlockSpec((None, None, D_FOX, PAGE_SIZE),
                            lambda b, s, pt, gi=gi: (layer, pt[b, (n_steps - 1 - s) * n_grp + gi], 0, 0))

    def lf_spec(gi):
        return pl.BlockSpec((None, H_FOX, SUBLANES, PAGE_SIZE),
                            lambda b, s, pt, gi=gi: (layer, 0, pt[b, (n_steps - 1 - s) * n_grp + gi] // SUBLANES, 0))

    page_specs, page_args = [], []
    for gi in range(n_grp):
        page_specs += [page_spec(gi)] * 2 + [lf_spec(gi)]
        page_args += list(pools) + [lf_pool]
    out_row = pl.BlockSpec((ts, D_FOX), lambda b, s, pt: (b, 0))
    small = pl.BlockSpec((None, SUBLANES, ts), lambda b, s, pt: (b, 0, 0))
    grid_spec = pltpu.PrefetchScalarGridSpec(
        num_scalar_prefetch=1, grid=(bsz, n_steps),
        in_specs=[piece(8), piece(9), piece(10), piece(11), small,
                  full(b_f8), full(gq_row), full(gk_row), full(seg), full(lmat)] + page_specs,
        out_specs=[out_row, out_row, small],
        scratch_shapes=[pltpu.VMEM((rows, D_FOX), BF16), pltpu.VMEM((rows, D_FOX), F32),
                        pltpu.VMEM((rows, 1), F32), pltpu.VMEM((rows, 1), F32), pltpu.VMEM((rows, 1), F32)],
    )
    return pl.pallas_call(
        functools.partial(_decode_fox_kernel, n_grp=n_grp, ts=ts), grid_spec=grid_spec,
        out_shape=[jax.ShapeDtypeStruct((n, D_FOX), F32), jax.ShapeDtypeStruct((n, D_FOX), F32),
                   jax.ShapeDtypeStruct((bsz, SUBLANES, ts), F32)],
        compiler_params=pltpu.CompilerParams(dimension_semantics=("parallel", "arbitrary"),
                                             vmem_limit_bytes=VMEM_LIMIT),
        name="decode_fox",
    )(page_table, *([proj] * 4), flt, b_f8, gq_row, gk_row, seg, lmat, *page_args)


def _rotary_tables(pos):
    half = HEAD_DIM // 2
    inv = ROPE_BASE ** (-jnp.arange(half, dtype=F32) / half)
    ang = pos.astype(F32)[:, None] * inv[None, :]
    cos, sin = jnp.cos(ang), jnp.sin(ang)
    reps = LANES // HEAD_DIM
    cos_row = jnp.tile(jnp.concatenate([cos, cos], axis=1), (1, reps))
    sin_row = jnp.tile(jnp.concatenate([-sin, sin], axis=1), (1, reps))
    return cos_row, sin_row, cos.T, sin.T


def _decay_tables(blk):
    log_g = jnp.log1p(-jnp.exp2(-5.0 - jnp.arange(H_RET, dtype=F32)))
    i = jnp.arange(blk, dtype=F32)
    diff = i[:, None] - i[None, :]
    dmask = jnp.where(diff[None] >= 0, jnp.exp(jnp.maximum(diff, 0.0)[None] * log_g[:, None, None]), 0.0)
    q_dec = jnp.exp((i + 1.0)[None, :] * log_g[:, None])
    k_dec = jnp.exp((blk - 1.0 - i)[None, :] * log_g[:, None])
    c_dec = jnp.exp(blk * log_g)
    return dmask, q_dec, k_dec, c_dec


def kernel(x_prompt, x_sample, state_ret, cache_sb_k, cache_sb_v, cache_fox_k, cache_fox_v, cache_fox_logf,
           page_table, p_prompt, p_sample, g_norm, w_in, b_f, g_ret_gn, g_fox_q, g_fox_k, w_out, w_pe, g_pe, w_pg):
    bp, tp, d = x_prompt.shape
    bs, ts, _ = x_sample.shape
    depth = w_in.shape[0]
    n_pages = page_table.shape[1]
    past_len = n_pages * PAGE_SIZE
    d_ple = p_prompt.shape[-1]
    assert tp % ATT_BLOCK == 0 and tp % RET_CHUNK == 0 and ts == SUBLANES

    cos_p, sin_p, cost_p, sint_p = _rotary_tables(jnp.arange(tp, dtype=jnp.int32))
    cos_s, sin_s, _, _ = _rotary_tables(past_len + jnp.arange(ts, dtype=jnp.int32))
    dm_p, qd_p, kd_p, cd_p = _decay_tables(RET_CHUNK)
    dec_p = (dm_p, qd_p[:, :, None], kd_p[:, None, :], cd_p[:, None, None])
    dm_s, qd_s, kd_s, cd_s = _decay_tables(ts)
    dec_s = (dm_s, qd_s[:, :, None], kd_s[:, :, None], cd_s[:, None, None])
    tq_att = 2 * ATT_BLOCK if tp % (2 * ATT_BLOCK) == 0 else ATT_BLOCK
    later = lambda n: (jnp.arange(n)[:, None] > jnp.arange(n)[None, :]).astype(BF16)
    lmat_page, lmat_att = later(PAGE_SIZE), later(tq_att)
    ar = jnp.arange(tq_att)
    umat = (ar[:, None] <= ar[None, :]).astype(BF16)
    lane_head = jnp.arange(D_FOX) // HEAD_DIM
    seg = (lane_head[:, None] == lane_head[None, :]).astype(BF16)

    def pool_t(c):
        c = jnp.transpose(c, (0, 1, 3, 4, 2))
        return c.reshape(c.shape[0], c.shape[1], c.shape[2] * c.shape[3], c.shape[4])
    pools = tuple(pool_t(c) for c in (cache_sb_k, cache_sb_v, cache_fox_k, cache_fox_v))
    lf_pool = jnp.transpose(cache_fox_logf, (0, 3, 1, 2))

    n_grp = next(g for g in (16, 8, 4, 2, 1) if n_pages % g == 0)
    n_head = min(4, n_pages)
    n_tail = next(g for g in (8, 7, 6, 5, 4, 3, 2, 1) if (n_pages - n_head) % g == 0)
    tm_in = next(m for m in (512, 256, ATT_BLOCK) if tp % m == 0)
    n_p, n_s = bp * tp, bs * ts
    tm_out = 512 if n_p % 512 == 0 else ATT_BLOCK

    y_p = x_prompt
    y_s = x_sample.reshape(n_s, d)
    outs = {k: [] for k in ("rs_p", "rs_s", "sks", "svs", "flp", "fks", "fvs", "fls")}
    kv_stacked = None
    for i in range(depth):
        w = w_in[i]
        splits = np.cumsum([D_RET] * 4 + [D_SB] * 4 + [D_FOX] * 4).tolist()
        (w_qr, w_kr, w_vr, w_gr, w_qs, w_ks, w_vs, w_gs, w_qf, w_kf, w_vf, w_gf, w_fl) = jnp.split(w, splits, axis=1)
        w_row = jnp.concatenate([w_qr, w_vr, w_gr, w_qs, w_gs, w_qf, w_gf], axis=1).astype(BF16)
        w_flt = jnp.pad(w_fl.T, ((0, SUBLANES - H_FOX), (0, 0))).astype(BF16)
        w_t = [a.T.astype(BF16) for a in (w_kr, w_ks, w_vs, w_kf, w_vf)] + [w_flt]
        gn = g_norm[i][None, :]
        b_f8 = jnp.pad(b_f[i], (0, SUBLANES - H_FOX))[:, None]
        gq_row = jnp.tile(g_fox_q[i], H_FOX)[None, :]
        gk_row = jnp.tile(g_fox_k[i], H_FOX)[None, :]
        gk_col = g_fox_k[i][:, None]
        ggn = g_ret_gn[i][None, :]
        w_o = w_out[i].astype(BF16)
        w_op = (w_o[:D_RET], w_o[D_RET:D_RET + D_SB], w_o[D_RET + D_SB:],
                w_pe[i].astype(BF16), g_pe[i][None, :], w_pg[i].astype(BF16))

        (qr, vr, gr, qs, gs, qf, gf, krt, kst, vst, kft, vft, lft) = _inproj_prompt(
            y_p, gn, (cos_p, sin_p, cost_p, sint_p), b_f8, gk_col, [w_row] + w_t, tm_in, i, depth, kv_stacked)
        kv_stacked = (kst, vst, kft, vft)
        o_r, s_fin = _ret_prompt(qr, krt, vr, gr, dec_p, ggn)
        o_s = _sb_prompt(qs, kst, vst, gs, lmat_att, tq_att, i)
        o_f = _fox_prompt(qf, kft, vft, lft, gf, gq_row, seg, umat, tq_att, i)
        y_p = _outproj(y_p.reshape(n_p, d), o_r.reshape(n_p, D_RET), o_s.reshape(n_p, D_SB),
                       o_f.reshape(n_p, D_FOX), p_prompt[i].reshape(n_p, d_ple), w_op, tm_out).reshape(bp, tp, d)
        outs["rs_p"].append(s_fin)
        outs["flp"].append(lft[:, :H_FOX])

        proj, flt = _inproj_sample(y_s, gn, [w_row] + w_t)
        flt_b = jnp.transpose(flt.reshape(SUBLANES, bs, ts), (1, 0, 2))
        o_r, s_new = _ret_sample(proj, state_ret[i], (cos_s, sin_s), dec_s, ggn, ts)
        o_f, kf_new, lf_new = _decode_fox(proj, flt_b, pools[2:], lf_pool, page_table, i,
                                          b_f8, gq_row, gk_row, seg, lmat_page, ts, n_grp)
        o_s, sb_acc, sb_car = _decode_sb(proj, pools[:2], page_table, i, lmat_page, ts, n_head, n_tail)
        if n_pages > n_head:
            o_s = lax.cond(jnp.max(sb_car) > SB_DEAD,
                           lambda st: _decode_sb(proj, pools[:2], page_table, i, lmat_page, ts, n_head, n_tail, st),
                           lambda st: o_s, (sb_acc, sb_car))
        y_s = _outproj(y_s, o_r, o_s, o_f, p_sample[i].reshape(n_s, d_ple), w_op, n_s)
        outs["rs_s"].append(s_new)
        piece = lambda k, wd: proj[:, k * PIECE:k * PIECE + wd]
        outs["sks"].append(piece(5, D_SB)); outs["svs"].append(piece(6, D_SB))
        outs["fks"].append(kf_new); outs["fvs"].append(piece(10, D_FOX))
        outs["fls"].append(lf_new[:, :H_FOX])

    st = lambda k: jnp.stack(outs[k], axis=0)
    kv_p = lambda a, h: jnp.transpose(a.reshape(depth, bp, h, HEAD_DIM, tp), (0, 1, 4, 2, 3))
    kv_s = lambda k, h: st(k).reshape(depth, bs, ts, h, HEAD_DIM)
    return (y_p, y_s.reshape(bs, ts, d), st("rs_p"), st("rs_s"),
            kv_p(kv_stacked[0], H_SB), kv_p(kv_stacked[1], H_SB), kv_s("sks", H_SB), kv_s("svs", H_SB),
            kv_p(kv_stacked[2], H_FOX), kv_p(kv_stacked[3], H_FOX), jnp.transpose(st("flp"), (0, 1, 3, 2)),
            kv_s("fks", H_FOX), kv_s("fvs", H_FOX), jnp.transpose(st("fls"), (0, 1, 3, 2)))
```

```python
import functools

import jax
import jax.numpy as jnp
import numpy as np
from jax import lax
from jax.experimental import pallas as pl
from jax.experimental.pallas import tpu as pltpu

F32 = jnp.float32
BF16 = jnp.bfloat16

HEAD_DIM = 64
H_RET, H_SB, H_FOX = 6, 5, 5
D_RET, D_SB, D_FOX = H_RET * HEAD_DIM, H_SB * HEAD_DIM, H_FOX * HEAD_DIM
PAGE_SIZE = 128
RET_CHUNK = 128
ATT_BLOCK = 128
ROW_CHUNK = 32
LANES = 128
SUBLANES = 8
ROPE_BASE = 10000.0
EPS = 1e-6
QK_SCALE = HEAD_DIM ** -0.5
SB_DEAD = -104.0
NEG_BIG = -1e30
PIECE = 384
VMEM_LIMIT = 48 * 1024 * 1024


def _dot(a, b):
    return jnp.dot(a, b, preferred_element_type=F32)


def _dot_nt(a, b):
    return lax.dot_general(a, b, (((1,), (1,)), ((), ())), preferred_element_type=F32)


def _dot_tn(a, b):
    return lax.dot_general(a, b, (((0,), (0,)), ((), ())), preferred_element_type=F32)


def _split2(x):
    hi = x.astype(BF16)
    lo = (x - hi.astype(F32)).astype(BF16)
    return hi, lo


def _split3(x):
    p1 = x.astype(BF16)
    r1 = x - p1.astype(F32)
    p2 = r1.astype(BF16)
    p3 = (r1 - p2.astype(F32)).astype(BF16)
    return p1, p2, p3


def _dot_exact2(x, m):
    hi, lo = _split2(x)
    return _dot(hi, m) + _dot(lo, m)


def _dot_exact3(x, m):
    p1, p2, p3 = _split3(x)
    return _dot(p1, m) + _dot(p2, m) + _dot(p3, m)


def _softplus(z):
    return jnp.maximum(z, 0.0) + jnp.log(1.0 + jnp.exp(-jnp.abs(z)))


def _log_sigmoid(x):
    return jnp.minimum(x, 0.0) - jnp.log1p(jnp.exp(-jnp.abs(x)))


def _silu(g):
    return g * jax.nn.sigmoid(g)


def _rotary_rows(x, cos, sin_signed):
    lane = lax.broadcasted_iota(jnp.int32, (x.shape[0], LANES), 1)
    first_half = (lane % HEAD_DIM) < (HEAD_DIM // 2)
    out = []
    for c in range(x.shape[1] // LANES):
        xs = x[:, c * LANES:(c + 1) * LANES]
        partner = jnp.where(first_half,
                            pltpu.roll(xs, LANES - HEAD_DIM // 2, axis=1),
                            pltpu.roll(xs, HEAD_DIM // 2, axis=1))
        out.append(xs * cos + partner * sin_signed)
    return jnp.concatenate(out, axis=1)


def _inproj_prompt_kernel(x_ref, g_ref, cos_ref, sin_ref, cost_ref, sint_ref, bf_ref, gk_ref,
                          wrow_ref, wkr_ref, wks_ref, wvs_ref, wkf_ref, wvf_ref, wfl_ref, *rest, first_of):
    (qr_ref, vr_ref, gr_ref, qs_ref, gs_ref, qf_ref, gf_ref,
     krt_ref, kst_ref, vst_ref, kft_ref, vft_ref, lft_ref) = rest[-13:]
    if first_of is not None:
        for ref in (kst_ref, vst_ref, kft_ref, vft_ref):
            if first_of > 1:
                ref[1:] = jnp.zeros((first_of - 1,) + ref.shape[1:], F32)
        kst_ref, vst_ref, kft_ref, vft_ref = (r.at[0] for r in (kst_ref, vst_ref, kft_ref, vft_ref))
    x = x_ref[...]
    ms = jnp.mean(x * x, axis=-1, keepdims=True)
    h = (x * lax.rsqrt(ms + EPS) * g_ref[...]).astype(BF16)

    row = _dot(h, wrow_ref[...])
    qr_ref[...] = _rotary_rows(row[:, 0:D_RET], cos_ref[...], sin_ref[...])
    vr_ref[...] = row[:, D_RET:2 * D_RET]
    gr_ref[...] = _silu(row[:, 2 * D_RET:3 * D_RET])
    o = 3 * D_RET
    qs_ref[...] = row[:, o:o + D_SB]
    gs_ref[...] = _silu(row[:, o + D_SB:o + 2 * D_SB])
    qf_ref[...] = row[:, o + 2 * D_SB:o + 3 * D_SB]
    gf_ref[...] = _silu(row[:, o + 3 * D_SB:o + 4 * D_SB])

    half = HEAD_DIM // 2
    krt = _dot_nt(wkr_ref[...], h)
    cost, sint = cost_ref[...], sint_ref[...]
    for hd in range(H_RET):
        x1 = krt[hd * HEAD_DIM:hd * HEAD_DIM + half]
        x2 = krt[hd * HEAD_DIM + half:(hd + 1) * HEAD_DIM]
        krt_ref[hd * HEAD_DIM:hd * HEAD_DIM + half, :] = (x1 * cost - x2 * sint) * QK_SCALE
        krt_ref[hd * HEAD_DIM + half:(hd + 1) * HEAD_DIM, :] = (x1 * sint + x2 * cost) * QK_SCALE

    kst_ref[...] = _dot_nt(wks_ref[...], h)
    vst_ref[...] = _dot_nt(wvs_ref[...], h)
    vft_ref[...] = _dot_nt(wvf_ref[...], h)
    kft = _dot_nt(wkf_ref[...], h)
    gk = gk_ref[...]
    for hd in range(H_FOX):
        xh = kft[hd * HEAD_DIM:(hd + 1) * HEAD_DIM]
        msh = jnp.mean(xh * xh, axis=0, keepdims=True)
        kft_ref[hd * HEAD_DIM:(hd + 1) * HEAD_DIM, :] = xh * lax.rsqrt(msh + EPS) * gk
    lft_ref[...] = _log_sigmoid(_dot_nt(wfl_ref[...], h) + bf_ref[...])


def _inproj_prompt(x, g_norm, tabs, b_f8, gk_col, w, tm, layer, depth, stacked):
    bsz, t, d = x.shape
    n_row = 3 * D_RET + 4 * D_SB
    grid = (bsz, t // tm)
    row_spec = lambda n: pl.BlockSpec((None, tm, n), lambda b, i: (b, i, 0))
    col_spec = lambda n: pl.BlockSpec((None, n, tm), lambda b, i: (b, 0, i))
    full = lambda a: pl.BlockSpec(a.shape, lambda b, i: (0,) * a.ndim)
    cos, sin, cost, sint = tabs
    in_specs = [
        row_spec(d), full(g_norm),
        pl.BlockSpec((tm, LANES), lambda b, i: (i, 0)), pl.BlockSpec((tm, LANES), lambda b, i: (i, 0)),
        pl.BlockSpec((HEAD_DIM // 2, tm), lambda b, i: (0, i)), pl.BlockSpec((HEAD_DIM // 2, tm), lambda b, i: (0, i)),
        full(b_f8), full(gk_col),
    ] + [full(a) for a in w]
    rs = lambda n: jax.ShapeDtypeStruct((bsz, t, n), F32)
    cs = lambda n: jax.ShapeDtypeStruct((bsz, n, t), F32)
    stk = jax.ShapeDtypeStruct((depth, bsz, D_SB, t), F32)
    aliases, extra = {}, []
    if stacked is None:
        stk_spec = pl.BlockSpec((depth, None, D_SB, tm), lambda b, i: (0, b, 0, i))
    else:
        stk_spec = pl.BlockSpec((None, None, D_SB, tm), lambda b, i: (layer, b, 0, i))
        extra = list(stacked)
        in_specs += [pl.BlockSpec(memory_space=pl.ANY)] * len(extra)
        first_in, first_out = len(in_specs) - len(extra), 8
        aliases = {first_in + k: first_out + k for k in range(len(extra))}
    out_shape = [rs(D_RET), rs(D_RET), rs(D_RET), rs(D_SB), rs(D_SB), rs(D_FOX), rs(D_FOX),
                 cs(D_RET), stk, stk, stk, stk, cs(SUBLANES)]
    out_specs = [row_spec(D_RET)] * 3 + [row_spec(D_SB)] * 4 + \
                [col_spec(D_RET)] + [stk_spec] * 4 + [col_spec(SUBLANES)]
    return pl.pallas_call(
        functools.partial(_inproj_prompt_kernel, first_of=depth if stacked is None else None),
        grid=grid, in_specs=in_specs, out_specs=out_specs, out_shape=out_shape,
        input_output_aliases=aliases,
        compiler_params=pltpu.CompilerParams(dimension_semantics=("parallel", "parallel"),
                                             vmem_limit_bytes=VMEM_LIMIT),
        name="inproj_prompt",
    )(x, g_norm, cos, sin, cost, sint, b_f8, gk_col, *w, *extra)


def _ret_prompt_kernel(qr_ref, krt_ref, vr_ref, gr_ref, dmask_ref, qdec_ref, kdec_ref, cdec_ref, ggn_ref,
                       o_ref, sfin_ref, s_scr):
    c = pl.program_id(1)

    @pl.when(c == 0)
    def _():
        s_scr[...] = jnp.zeros_like(s_scr)

    for hd in range(H_RET):
        sl = slice(hd * HEAD_DIM, (hd + 1) * HEAD_DIM)
        q = qr_ref[:, sl].astype(BF16)
        kt = krt_ref[sl, :]
        v = vr_ref[:, sl].astype(BF16)
        sc = _dot(q, kt.astype(BF16)) * dmask_ref[hd]
        inner = _dot(sc.astype(BF16), v)
        s_old = s_scr[hd]
        cross = _dot(q, s_old.astype(BF16)) * qdec_ref[hd]
        o = inner + cross
        s_scr[hd] = s_old * cdec_ref[hd] + _dot((kt * kdec_ref[hd]).astype(BF16), v)
        mu = jnp.mean(o, axis=-1, keepdims=True)
        dev = o - mu
        var = jnp.mean(dev * dev, axis=-1, keepdims=True)
        o_ref[:, sl] = dev * lax.rsqrt(var + EPS) * ggn_ref[:, sl] * gr_ref[:, sl]

    @pl.when(c == pl.num_programs(1) - 1)
    def _():
        sfin_ref[...] = s_scr[...]


def _ret_prompt(qr, krt, vr, gr, dec, ggn):
    bsz, t, _ = qr.shape
    blk = RET_CHUNK
    dmask, qdec, kdec, cdec = dec
    row = pl.BlockSpec((None, blk, D_RET), lambda b, c: (b, c, 0))
    full = lambda a: pl.BlockSpec(a.shape, lambda b, c: (0,) * a.ndim)
    return pl.pallas_call(
        _ret_prompt_kernel, grid=(bsz, t // blk),
        in_specs=[row, pl.BlockSpec((None, D_RET, blk), lambda b, c: (b, 0, c)), row, row,
                  full(dmask), full(qdec), full(kdec), full(cdec), full(ggn)],
        out_specs=[row, pl.BlockSpec((None, H_RET, HEAD_DIM, HEAD_DIM), lambda b, c: (b, 0, 0, 0))],
        out_shape=[jax.ShapeDtypeStruct((bsz, t, D_RET), F32),
                   jax.ShapeDtypeStruct((bsz, H_RET, HEAD_DIM, HEAD_DIM), F32)],
        scratch_shapes=[pltpu.VMEM((H_RET, HEAD_DIM, HEAD_DIM), F32)],
        compiler_params=pltpu.CompilerParams(dimension_semantics=("parallel", "arbitrary"),
                                             vmem_limit_bytes=VMEM_LIMIT),
        name="ret_prompt",
    )(qr, krt, vr, gr, dmask, qdec, kdec, cdec, ggn)


def _sb_prompt_kernel(q_ref, kt_ref, vt_ref, g_ref, lmat_ref, o_ref,
                      q_scr, car_scr, acc_scr, z_scr, ls_scr, hi_scr, lo_scr, a_scr, tot_scr):
    i = pl.program_id(1)
    tq = q_ref.shape[0]
    reps = tq // LANES
    for hd in range(H_SB):
        sl = slice(hd * HEAD_DIM, (hd + 1) * HEAD_DIM)
        q_scr[hd] = (q_ref[:, sl] * QK_SCALE).astype(BF16)
    car_scr[...] = jnp.zeros_like(car_scr)
    acc_scr[...] = jnp.zeros_like(acc_scr)

    def block(j, diag):
        off = pl.multiple_of(j * tq, tq)
        lmat = lmat_ref[...]
        for hd in range(H_SB):
            sl = slice(hd * HEAD_DIM, (hd + 1) * HEAD_DIM)
            z_scr[hd] = _dot(q_scr[hd], kt_ref[sl, pl.ds(off, tq)].astype(BF16))
        chunk_masks = []
        for r in range(0, tq, ROW_CHUNK):
            if diag:
                rowi = lax.broadcasted_iota(jnp.int32, (ROW_CHUNK, tq), 0) + r
                coli = lax.broadcasted_iota(jnp.int32, (ROW_CHUNK, tq), 1)
                chunk_masks.append(coli < rowi)
            else:
                chunk_masks.append(None)
        for hd in range(H_SB):
            for ci, r in enumerate(range(0, tq, ROW_CHUNK)):
                rows = slice(r, r + ROW_CHUNK)
                z = z_scr[hd, rows, :]
                sp = _softplus(z)
                ls_scr[hd, rows, :] = z - sp
                if diag:
                    sp = jnp.where(chunk_masks[ci], sp, 0.0)
                hi, lo = _split2(sp)
                hi_scr[hd, rows, :] = hi
                lo_scr[hd, rows, :] = lo
                tot = jnp.sum(sp, axis=-1, keepdims=True)
                tot_scr[hd, rows, :] = jnp.broadcast_to(tot, (ROW_CHUNK, LANES))
        for hd in range(H_SB):
            z_scr[hd] = _dot(hi_scr[hd], lmat) + _dot(lo_scr[hd], lmat)
        for hd in range(H_SB):
            for ci, r in enumerate(range(0, tq, ROW_CHUNK)):
                rows = slice(r, r + ROW_CHUNK)
                carry = car_scr[hd, rows, :]
                a = jnp.exp(ls_scr[hd, rows, :] - z_scr[hd, rows, :] + jnp.tile(carry, (1, reps)))
                if diag:
                    a = jnp.where(chunk_masks[ci], a, 0.0)
                a_scr[hd, rows, :] = a.astype(BF16)
                car_scr[hd, rows, :] = carry - tot_scr[hd, rows, :]
        for hd in range(H_SB):
            sl = slice(hd * HEAD_DIM, (hd + 1) * HEAD_DIM)
            acc_scr[hd] += _dot_nt(a_scr[hd], vt_ref[sl, pl.ds(off, tq)].astype(BF16))
        return jnp.max(car_scr[...])

    cmax = block(i, True)

    def cond(st):
        j, cmax = st
        return jnp.logical_and(j >= 0, cmax > SB_DEAD)

    def body(st):
        j, _ = st
        return j - 1, block(j, False)

    lax.while_loop(cond, body, (i - 1, cmax))
    for hd in range(H_SB):
        sl = slice(hd * HEAD_DIM, (hd + 1) * HEAD_DIM)
        o_ref[:, sl] = acc_scr[hd] * g_ref[:, sl]


def _sb_prompt(qs, kst, vst, gs, lmat, tq, layer):
    bsz, t, _ = qs.shape
    row = pl.BlockSpec((None, tq, D_SB), lambda b, i: (b, i, 0))
    seq = pl.BlockSpec((None, None, D_SB, t), lambda b, i: (layer, b, 0, 0))
    return pl.pallas_call(
        _sb_prompt_kernel, grid=(bsz, t // tq),
        in_specs=[row, seq, seq, row, pl.BlockSpec(lmat.shape, lambda b, i: (0, 0))],
        out_specs=row, out_shape=jax.ShapeDtypeStruct((bsz, t, D_SB), F32),
        scratch_shapes=[pltpu.VMEM((H_SB, tq, HEAD_DIM), BF16), pltpu.VMEM((H_SB, tq, LANES), F32),
                        pltpu.VMEM((H_SB, tq, HEAD_DIM), F32),
                        pltpu.VMEM((H_SB, tq, tq), F32), pltpu.VMEM((H_SB, tq, tq), F32),
                        pltpu.VMEM((H_SB, tq, tq), BF16), pltpu.VMEM((H_SB, tq, tq), BF16),
                        pltpu.VMEM((H_SB, tq, tq), BF16), pltpu.VMEM((H_SB, tq, LANES), F32)],
        compiler_params=pltpu.CompilerParams(dimension_semantics=("parallel", "arbitrary"),
                                             vmem_limit_bytes=VMEM_LIMIT),
        name="sb_prompt",
    )(qs, kst, vst, gs, lmat)


def _fox_prompt_kernel(q_ref, kt_ref, vt_ref, lft_ref, g_ref, gq_ref, seg_ref, umat_ref, o_ref,
                       c_scr, q_scr, m_scr, acc_scr, s_scr, p_scr, a_scr):
    i = pl.program_id(1)
    tq = q_ref.shape[0]
    t = kt_ref.shape[1]
    cw = umat_ref.shape[0]

    @pl.when(i == 0)
    def _():
        umat = umat_ref[...]
        carry = jnp.zeros((SUBLANES, 1), F32)
        for c in range(t // cw):
            cs = _dot_exact3(lft_ref[:, c * cw:(c + 1) * cw], umat) + carry
            c_scr[:, c * cw:(c + 1) * cw] = cs
            carry = cs[:, cw - 1:cw]

    qraw = q_ref[...]
    ssq = _dot_exact2(qraw * qraw, seg_ref[...])
    qn = qraw * lax.rsqrt(ssq * (1.0 / HEAD_DIM) + EPS) * gq_ref[...] * QK_SCALE
    for hd in range(H_FOX):
        q_scr[hd] = qn[:, hd * HEAD_DIM:(hd + 1) * HEAD_DIM].astype(BF16)
    m_scr[...] = jnp.full_like(m_scr, NEG_BIG)
    acc_scr[...] = jnp.zeros_like(acc_scr)
    rowi = lax.broadcasted_iota(jnp.int32, (tq, tq), 0)
    coli = lax.broadcasted_iota(jnp.int32, (tq, tq), 1)
    diag_mask = coli <= rowi

    ones_rows = jnp.ones((HEAD_DIM, tq), BF16)

    def block(j, mask):
        off = pl.multiple_of(j * tq, tq)
        for hd in range(H_FOX):
            sl = slice(hd * HEAD_DIM, (hd + 1) * HEAD_DIM)
            kt = kt_ref[sl, pl.ds(off, tq)].astype(BF16)
            s = _dot(q_scr[hd], kt) - c_scr[hd:hd + 1, pl.ds(off, tq)]
            if mask is not None:
                s = jnp.where(mask, s, NEG_BIG)
            s_scr[hd] = s
        for hd in range(H_FOX):
            for r in range(0, tq, ROW_CHUNK):
                rows = slice(r, r + ROW_CHUNK)
                s = s_scr[hd, rows, :]
                m_old = m_scr[hd, rows, :]
                mx = jnp.max(s, axis=-1, keepdims=True)
                m_new = jnp.maximum(m_old, jnp.broadcast_to(mx, (ROW_CHUNK, LANES)))
                p_scr[hd, rows, :] = jnp.exp(s - jnp.tile(m_new, (1, tq // LANES))).astype(BF16)
                a_scr[hd, rows, :] = jnp.exp(m_old - m_new)
                m_scr[hd, rows, :] = m_new
        for hd in range(H_FOX):
            sl = slice(hd * HEAD_DIM, (hd + 1) * HEAD_DIM)
            vt = jnp.concatenate([vt_ref[sl, pl.ds(off, tq)].astype(BF16), ones_rows], axis=0)
            acc_scr[hd] = a_scr[hd] * acc_scr[hd] + _dot_nt(p_scr[hd], vt)

    def body(j, carry):
        block(j, None)
        return carry

    lax.fori_loop(0, i, body, 0)
    block(i, diag_mask)
    for hd in range(H_FOX):
        sl = slice(hd * HEAD_DIM, (hd + 1) * HEAD_DIM)
        acc = acc_scr[hd]
        o_ref[:, sl] = acc[:, :HEAD_DIM] / acc[:, HEAD_DIM:HEAD_DIM + 1] * g_ref[:, sl]


def _fox_prompt(qf, kft, vft, lft, gf, gq_row, seg, umat, tq, layer):
    bsz, t, _ = qf.shape
    row = pl.BlockSpec((None, tq, D_FOX), lambda b, i: (b, i, 0))
    seq = pl.BlockSpec((None, None, D_FOX, t), lambda b, i: (layer, b, 0, 0))
    full = lambda a: pl.BlockSpec(a.shape, lambda b, i: (0,) * a.ndim)
    return pl.pallas_call(
        _fox_prompt_kernel, grid=(bsz, t // tq),
        in_specs=[row, seq, seq, pl.BlockSpec((None, SUBLANES, t), lambda b, i: (b, 0, 0)), row,
                  full(gq_row), full(seg), full(umat)],
        out_specs=row, out_shape=jax.ShapeDtypeStruct((bsz, t, D_FOX), F32),
        scratch_shapes=[pltpu.VMEM((SUBLANES, t), F32), pltpu.VMEM((H_FOX, tq, HEAD_DIM), BF16),
                        pltpu.VMEM((H_FOX, tq, LANES), F32), pltpu.VMEM((H_FOX, tq, 2 * HEAD_DIM), F32),
                        pltpu.VMEM((H_FOX, tq, tq), F32), pltpu.VMEM((H_FOX, tq, tq), BF16),
                        pltpu.VMEM((H_FOX, tq, LANES), F32)],
        compiler_params=pltpu.CompilerParams(dimension_semantics=("parallel", "arbitrary"),
                                             vmem_limit_bytes=VMEM_LIMIT),
        name="fox_prompt",
    )(qf, kft, vft, lft, gf, gq_row, seg, umat)


def _outproj_kernel(y_ref, or_ref, os_ref, of_ref, p_ref, wor_ref, wos_ref, wof_ref, wpe_ref, gpe_ref, wpg_ref,
                    o_ref):
    m = (_dot(or_ref[...].astype(BF16), wor_ref[...]) + _dot(os_ref[...].astype(BF16), wos_ref[...])
         + _dot(of_ref[...].astype(BF16), wof_ref[...]))
    y1 = y_ref[...] + m
    ms = jnp.mean(y1 * y1, axis=-1, keepdims=True)
    n = (y1 * lax.rsqrt(ms + EPS) * gpe_ref[...]).astype(BF16)
    gate = jax.nn.sigmoid(_dot(n, wpg_ref[...]))
    pe = _dot(p_ref[...].astype(BF16), wpe_ref[...])
    o_ref[...] = y1 + pe * gate


def _outproj(y, o_r, o_s, o_f, p, w, tm):
    n, d = y.shape
    row = lambda a: pl.BlockSpec((tm, a.shape[1]), lambda i: (i, 0))
    full = lambda a: pl.BlockSpec(a.shape, lambda i: (0,) * a.ndim)
    return pl.pallas_call(
        _outproj_kernel, grid=(n // tm,),
        in_specs=[row(y), row(o_r), row(o_s), row(o_f), row(p)] + [full(a) for a in w],
        out_specs=row(y), out_shape=jax.ShapeDtypeStruct((n, d), F32),
        compiler_params=pltpu.CompilerParams(dimension_semantics=("parallel",), vmem_limit_bytes=VMEM_LIMIT),
        name="outproj",
    )(y, o_r, o_s, o_f, p, *w)


def _inproj_sample_kernel(x_ref, g_ref, wrow_ref, wkr_ref, wks_ref, wvs_ref, wkf_ref, wvf_ref, wfl_ref,
                          o_ref, flt_ref):
    x = x_ref[...]
    ms = jnp.mean(x * x, axis=-1, keepdims=True)
    h = (x * lax.rsqrt(ms + EPS) * g_ref[...]).astype(BF16)
    row = _dot(h, wrow_ref[...])
    o_ref[...] = jnp.zeros_like(o_ref)

    def put(slot, val):
        o_ref[:, slot * PIECE:slot * PIECE + val.shape[1]] = val

    put(0, row[:, 0:D_RET])
    put(2, row[:, D_RET:2 * D_RET])
    put(3, _silu(row[:, 2 * D_RET:3 * D_RET]))
    o = 3 * D_RET
    put(4, row[:, o:o + D_SB])
    put(7, _silu(row[:, o + D_SB:o + 2 * D_SB]))
    put(8, row[:, o + 2 * D_SB:o + 3 * D_SB])
    put(11, _silu(row[:, o + 3 * D_SB:o + 4 * D_SB]))
    put(1, _dot_nt(h, wkr_ref[...]))
    put(5, _dot_nt(h, wks_ref[...]))
    put(6, _dot_nt(h, wvs_ref[...]))
    put(9, _dot_nt(h, wkf_ref[...]))
    put(10, _dot_nt(h, wvf_ref[...]))
    flt_ref[...] = _dot_nt(wfl_ref[...], h)


def _inproj_sample(x, g_norm, w):
    n, d = x.shape
    ncol = 12 * PIECE
    full = lambda a: pl.BlockSpec(a.shape, lambda i: (0,) * a.ndim)
    return pl.pallas_call(
        _inproj_sample_kernel, grid=(1,),
        in_specs=[full(x), full(g_norm)] + [full(a) for a in w],
        out_specs=[pl.BlockSpec((n, ncol), lambda i: (0, 0)), pl.BlockSpec((SUBLANES, n), lambda i: (0, 0))],
        out_shape=[jax.ShapeDtypeStruct((n, ncol), F32), jax.ShapeDtypeStruct((SUBLANES, n), F32)],
        compiler_params=pltpu.CompilerParams(dimension_semantics=("arbitrary",), vmem_limit_bytes=VMEM_LIMIT),
        name="inproj_sample",
    )(x, g_norm, *w)


def _ret_sample_kernel(q_ref, k_ref, v_ref, g_ref, s0_ref, cos_ref, sin_ref, dmask_ref, qdec_ref, kdec_ref,
                       cdec_ref, ggn_ref, o_ref, s_ref):
    cos, sin = cos_ref[...], sin_ref[...]
    qr = _rotary_rows(q_ref[...], cos, sin)
    kr = _rotary_rows(k_ref[...], cos, sin) * QK_SCALE
    for hd in range(H_RET):
        sl = slice(hd * HEAD_DIM, (hd + 1) * HEAD_DIM)
        q = qr[:, sl].astype(BF16)
        k = kr[:, sl]
        v = v_ref[:, sl].astype(BF16)
        sc = _dot_nt(q, k.astype(BF16)) * dmask_ref[hd]
        inner = _dot(sc.astype(BF16), v)
        s_old = s0_ref[hd]
        cross = _dot(q, s_old.astype(BF16)) * qdec_ref[hd]
        o = inner + cross
        s_ref[hd] = s_old * cdec_ref[hd] + _dot_tn((k * kdec_ref[hd]).astype(BF16), v)
        mu = jnp.mean(o, axis=-1, keepdims=True)
        dev = o - mu
        var = jnp.mean(dev * dev, axis=-1, keepdims=True)
        y = dev * lax.rsqrt(var + EPS) * ggn_ref[:, sl]
        o_ref[:, sl] = y * g_ref[:, sl]


def _ret_sample(proj, s0, tabs, dec, ggn, ts):
    n = proj.shape[0]
    bsz = n // ts
    cos, sin = tabs
    dmask, qdec, kdec, cdec = dec
    piece = lambda k: pl.BlockSpec((ts, PIECE), lambda b, k=k: (b, k))
    full = lambda a: pl.BlockSpec(a.shape, lambda b: (0,) * a.ndim)
    st = pl.BlockSpec((None, H_RET, HEAD_DIM, HEAD_DIM), lambda b: (b, 0, 0, 0))
    return pl.pallas_call(
        _ret_sample_kernel, grid=(bsz,),
        in_specs=[piece(0), piece(1), piece(2), piece(3), st, full(cos), full(sin),
                  full(dmask), full(qdec), full(kdec), full(cdec), full(ggn)],
        out_specs=[pl.BlockSpec((ts, D_RET), lambda b: (b, 0)), st],
        out_shape=[jax.ShapeDtypeStruct((n, D_RET), F32),
                   jax.ShapeDtypeStruct((bsz, H_RET, HEAD_DIM, HEAD_DIM), F32)],
        compiler_params=pltpu.CompilerParams(dimension_semantics=("parallel",), vmem_limit_bytes=VMEM_LIMIT),
        name="ret_sample",
    )(proj, proj, proj, proj, s0, cos, sin, dmask, qdec, kdec, cdec, ggn)


def _head_rows(x, ts):
    rows = H_SB * ts
    r = lax.broadcasted_iota(jnp.int32, (rows, D_SB), 0)
    c = lax.broadcasted_iota(jnp.int32, (rows, D_SB), 1)
    tiled = jnp.concatenate([x] * H_SB, axis=0)
    return jnp.where(r // ts == c // HEAD_DIM, tiled, 0.0)


def _head_cols(x, ts):
    c = lax.broadcasted_iota(jnp.int32, (ts, D_SB), 1)
    out = jnp.zeros((ts, D_SB), F32)
    for hd in range(H_SB):
        out = out + jnp.where(c // HEAD_DIM == hd, x[hd * ts:(hd + 1) * ts], 0.0)
    return out


def _new_token_masks(ts, strict):
    rows = H_SB * ts
    tpos = lax.broadcasted_iota(jnp.int32, (rows, PAGE_SIZE), 0) % ts
    coli = lax.broadcasted_iota(jnp.int32, (rows, PAGE_SIZE), 1)
    return coli < tpos if strict else coli <= tpos


def _sb_blocks(zs, v_dots, lmat, acc, car):
    for (z, mask), v_dot in zip(zs, v_dots):
        sp = _softplus(z)
        ls = z - sp
        if mask is not None:
            sp = jnp.where(mask, sp, 0.0)
        a = jnp.exp(ls - _dot_exact2(sp, lmat) + car)
        if mask is not None:
            a = jnp.where(mask, a, 0.0)
        acc = acc + v_dot(a.astype(BF16))
        car = car - jnp.sum(sp, axis=-1, keepdims=True)
    return acc, car


def _decode_sb_kernel(pt_ref, qs_ref, ks_ref, vs_ref, gs_ref, lmat_ref, *rest, n_grp, ts, head):
    if head:
        pages = rest[:2 * n_grp]
        os_ref, acco_ref, caro_ref = rest[2 * n_grp:]
    else:
        acci_ref, cari_ref = rest[:2]
        pages = rest[2:2 + 2 * n_grp]
        os_ref, qbs_scr, acc_scr, car_scr = rest[2 + 2 * n_grp:]
    lmat = lmat_ref[...]
    order = list(range(n_grp - 1, -1, -1))

    def page_blocks(qbs):
        zs, dots = [], []
        for gi in order:
            kt_ref, vt_ref = pages[2 * gi:2 * gi + 2]
            zs.append((_dot(qbs, kt_ref[...].astype(BF16)), None))
            dots.append(lambda a, r=vt_ref: _dot_nt(a, r[...].astype(BF16)))
        return zs, dots

    qbs_new = _head_rows(qs_ref[:, :D_SB] * QK_SCALE, ts).astype(BF16)
    if head:
        pad = jnp.zeros((PAGE_SIZE - ts, D_SB), F32)
        k_new = jnp.concatenate([ks_ref[:, :D_SB], pad], axis=0).astype(BF16)
        v_new = jnp.concatenate([vs_ref[:, :D_SB], pad], axis=0).astype(BF16)
        zs, dots = page_blocks(qbs_new)
        acc, car = _sb_blocks([(_dot_nt(qbs_new, k_new), _new_token_masks(ts, True))] + zs,
                              [lambda a: _dot(a, v_new)] + dots, lmat,
                              jnp.zeros((H_SB * ts, D_SB), F32), jnp.zeros((H_SB * ts, 1), F32))
        acco_ref[...] = acc
        caro_ref[...] = jnp.broadcast_to(car, caro_ref.shape)
        os_ref[...] = _head_cols(acc, ts) * gs_ref[:, :D_SB]
    else:
        s = pl.program_id(1)

        @pl.when(s == 0)
        def _():
            qbs_scr[...] = qbs_new
            acc_scr[...] = acci_ref[...]
            car_scr[...] = cari_ref[:, 0:1]

        @pl.when(jnp.max(car_scr[...]) > SB_DEAD)
        def _():
            zs, dots = page_blocks(qbs_scr[...])
            acc, car = _sb_blocks(zs, dots, lmat, acc_scr[...], car_scr[...])
            acc_scr[...] = acc
            car_scr[...] = car

        @pl.when(s == pl.num_programs(1) - 1)
        def _():
            os_ref[...] = _head_cols(acc_scr[...], ts) * gs_ref[:, :D_SB]


def _decode_sb(proj, pools, page_table, layer, lmat, ts, n_head, n_grp, state=None):
    n = proj.shape[0]
    bsz, n_pages = page_table.shape
    rows = H_SB * ts
    head = state is None
    per_step = n_head if head else n_grp
    n_steps = 1 if head else (n_pages - n_head) // n_grp
    last = n_pages - 1 if head else n_pages - n_head - 1
    piece = lambda k: pl.BlockSpec((ts, PIECE), lambda b, s, pt, k=k: (b, k))

    def page_spec(gi):
        return pl.BlockSpec((None, None, D_SB, PAGE_SIZE),
                            lambda b, s, pt, gi=gi: (layer, pt[b, last - s * per_step - (per_step - 1 - gi)], 0, 0))

    page_specs = [page_spec(gi) for gi in range(per_step) for _ in range(2)]
    page_args = [p for _ in range(per_step) for p in pools]
    out_row = pl.BlockSpec((ts, D_SB), lambda b, s, pt: (b, 0))
    st_acc = pl.BlockSpec((rows, D_SB), lambda b, s, pt: (b, 0))
    st_car = pl.BlockSpec((rows, LANES), lambda b, s, pt: (b, 0))
    in_specs = [piece(4), piece(5), piece(6), piece(7), pl.BlockSpec(lmat.shape, lambda b, s, pt: (0, 0))]
    args = [proj] * 4 + [lmat]
    if head:
        out_specs = [out_row, st_acc, st_car]
        out_shape = [jax.ShapeDtypeStruct((n, D_SB), F32), jax.ShapeDtypeStruct((bsz * rows, D_SB), F32),
                     jax.ShapeDtypeStruct((bsz * rows, LANES), F32)]
        scratch = []
    else:
        in_specs += [st_acc, st_car]
        args += list(state)
        out_specs = out_row
        out_shape = jax.ShapeDtypeStruct((n, D_SB), F32)
        scratch = [pltpu.VMEM((rows, D_SB), BF16), pltpu.VMEM((rows, D_SB), F32), pltpu.VMEM((rows, 1), F32)]
    grid_spec = pltpu.PrefetchScalarGridSpec(
        num_scalar_prefetch=1, grid=(bsz, n_steps), in_specs=in_specs + page_specs, out_specs=out_specs,
        scratch_shapes=scratch)
    return pl.pallas_call(
        functools.partial(_decode_sb_kernel, n_grp=per_step, ts=ts, head=head), grid_spec=grid_spec,
        out_shape=out_shape,
        compiler_params=pltpu.CompilerParams(dimension_semantics=("parallel", "arbitrary"),
                                             vmem_limit_bytes=VMEM_LIMIT),
        name="decode_sb_head" if head else "decode_sb_tail",
    )(page_table, *args, *page_args)


def _decode_fox_kernel(pt_ref, qf_ref, kf_ref, vf_ref, gf_ref, flt_ref, bf_ref, gq_ref, gk_ref, seg_ref, lmat_ref,
                       kpool_ref, vpool_ref, lfpool_ref, of_ref, kfn_ref, lfn_ref,
                       kbuf, vbuf, lfbuf, sem, *, n_grp, ts, layer):
    b = pl.program_id(0)
    n_seq = pl.num_programs(0)
    n_pages = pt_ref.shape[1]
    n_groups = n_pages // n_grp
    rows = H_FOX * ts
    lmat = lmat_ref[...]

    def group_copies(seq, g, slot):
        out = []
        for gi in range(n_grp):
            page = pt_ref[seq, n_pages - 1 - g * n_grp - gi]
            tile = pl.multiple_of((page // SUBLANES) * SUBLANES, SUBLANES)
            out.append(pltpu.make_async_copy(kpool_ref.at[layer, page], kbuf.at[slot, gi], sem.at[slot, 0]))
            out.append(pltpu.make_async_copy(vpool_ref.at[layer, page], vbuf.at[slot, gi], sem.at[slot, 1]))
            out.append(pltpu.make_async_copy(lfpool_ref.at[layer, :, pl.ds(tile, SUBLANES), :], lfbuf.at[slot, gi],
                                             sem.at[slot, 2]))
        return out

    @pl.when(b == 0)
    def _():
        for cp in group_copies(b, 0, 0):
            cp.start()

    def rep_heads(x):
        return jnp.concatenate([jnp.broadcast_to(x[hd:hd + 1], (ts, x.shape[1])) for hd in range(H_FOX)], axis=0)

    def fox_block(sc, mask, v_dot, m_old, l, acc):
        if mask is not None:
            sc = jnp.where(mask, sc, NEG_BIG)
        m_new = jnp.maximum(m_old, jnp.max(sc, axis=-1, keepdims=True))
        alpha = jnp.exp(m_old - m_new)
        p = jnp.exp(sc - m_new)
        l = alpha * l + jnp.sum(p, axis=-1, keepdims=True)
        acc = alpha * acc + v_dot(p.astype(BF16))
        return m_new, l, acc

    pad = jnp.zeros((PAGE_SIZE - ts, D_FOX), F32)
    seg = seg_ref[...]
    qraw = qf_ref[:, :D_FOX]
    qn = qraw * lax.rsqrt(_dot_exact2(qraw * qraw, seg) * (1.0 / HEAD_DIM) + EPS) * gq_ref[...] * QK_SCALE
    qbf = _head_rows(qn, ts).astype(BF16)
    kraw = kf_ref[:, :D_FOX]
    kn = kraw * lax.rsqrt(_dot_exact2(kraw * kraw, seg) * (1.0 / HEAD_DIM) + EPS) * gk_ref[...]
    kfn_ref[...] = kn
    lf_new = _log_sigmoid(flt_ref[...] + bf_ref[...])
    lfn_ref[...] = lf_new
    lf_pad = jnp.concatenate([lf_new, jnp.zeros((SUBLANES, PAGE_SIZE - ts), F32)], axis=1)
    lf_rows = rep_heads(lf_pad)
    cn = jnp.sum(lf_rows, axis=-1, keepdims=True) - _dot_exact3(lf_rows, lmat)
    kn_pad = jnp.concatenate([kn, pad], axis=0).astype(BF16)
    vf_new = jnp.concatenate([vf_ref[:, :D_FOX], pad], axis=0).astype(BF16)
    m, l, acc = fox_block(_dot_nt(qbf, kn_pad) - cn, _new_token_masks(ts, False), lambda p: _dot(p, vf_new),
                          jnp.full((rows, 1), NEG_BIG, F32), jnp.zeros((rows, 1), F32),
                          jnp.zeros((rows, D_FOX), F32))

    rcar = jnp.zeros((rows, 1), F32)
    for g in range(n_groups):
        slot = g % 2
        if g + 1 < n_groups:
            for cp in group_copies(b, g + 1, 1 - slot):
                cp.start()
        else:
            @pl.when(b + 1 < n_seq)
            def _():
                for cp in group_copies(b + 1, 0, 1 - slot):
                    cp.start()
        for cp in group_copies(b, g, slot):
            cp.wait()
        lf_rows = []
        for gi in range(n_grp):
            page = pt_ref[b, n_pages - 1 - g * n_grp - gi]
            lf_pg = lfbuf[slot, gi, :, pl.ds(page % SUBLANES, 1), :]
            lf_rows.append(jnp.concatenate([jnp.broadcast_to(lf_pg[hd], (ts, PAGE_SIZE)) for hd in range(H_FOX)],
                                           axis=0))
        later = _dot_exact3(jnp.concatenate(lf_rows, axis=0), lmat)
        biases = []
        for k, lr in enumerate(lf_rows):
            biases.append(later[k * rows:(k + 1) * rows] + rcar)
            rcar = rcar + jnp.sum(lr, axis=-1, keepdims=True)
        kt_all = jnp.concatenate([kbuf[slot, gi].astype(BF16) for gi in range(n_grp)], axis=1)
        vt_all = jnp.concatenate([vbuf[slot, gi].astype(BF16) for gi in range(n_grp)], axis=1)
        sc = _dot(qbf, kt_all) + jnp.concatenate(biases, axis=1)
        m, l, acc = fox_block(sc, None, lambda p, vt_all=vt_all: _dot_nt(p, vt_all), m, l, acc)

    of_ref[...] = _head_cols(acc / l, ts) * gf_ref[:, :D_FOX]


def _decode_fox(proj, flt, pools, lf_pool, page_table, layer, b_f8, gq_row, gk_row, seg, lmat, ts, n_grp):
    n = proj.shape[0]
    bsz, n_pages = page_table.shape
    assert n_pages % (2 * n_grp) == 0
    piece = lambda k: pl.BlockSpec((ts, PIECE), lambda b, pt, k=k: (b, k))
    full = lambda a: pl.BlockSpec(a.shape, lambda b, pt: (0,) * a.ndim)
    hbm = pl.BlockSpec(memory_space=pl.ANY)
    out_row = pl.BlockSpec((ts, D_FOX), lambda b, pt: (b, 0))
    small = pl.BlockSpec((None, SUBLANES, ts), lambda b, pt: (b, 0, 0))
    grid_spec = pltpu.PrefetchScalarGridSpec(
        num_scalar_prefetch=1, grid=(bsz,),
        in_specs=[piece(8), piece(9), piece(10), piece(11), small,
                  full(b_f8), full(gq_row), full(gk_row), full(seg), full(lmat), hbm, hbm, hbm],
        out_specs=[out_row, out_row, small],
        scratch_shapes=[pltpu.VMEM((2, n_grp, D_FOX, PAGE_SIZE), F32), pltpu.VMEM((2, n_grp, D_FOX, PAGE_SIZE), F32),
                        pltpu.VMEM((2, n_grp, H_FOX, SUBLANES, PAGE_SIZE), F32), pltpu.SemaphoreType.DMA((2, 3))],
    )
    return pl.pallas_call(
        functools.partial(_decode_fox_kernel, n_grp=n_grp, ts=ts, layer=layer), grid_spec=grid_spec,
        out_shape=[jax.ShapeDtypeStruct((n, D_FOX), F32), jax.ShapeDtypeStruct((n, D_FOX), F32),
                   jax.ShapeDtypeStruct((bsz, SUBLANES, ts), F32)],
        compiler_params=pltpu.CompilerParams(dimension_semantics=("arbitrary",), vmem_limit_bytes=VMEM_LIMIT),
        name="decode_fox",
    )(page_table, *([proj] * 4), flt, b_f8, gq_row, gk_row, seg, lmat, *pools, lf_pool)


def _rotary_tables(pos):
    half = HEAD_DIM // 2
    inv = ROPE_BASE ** (-jnp.arange(half, dtype=F32) / half)
    ang = pos.astype(F32)[:, None] * inv[None, :]
    cos, sin = jnp.cos(ang), jnp.sin(ang)
    reps = LANES // HEAD_DIM
    cos_row = jnp.tile(jnp.concatenate([cos, cos], axis=1), (1, reps))
    sin_row = jnp.tile(jnp.concatenate([-sin, sin], axis=1), (1, reps))
    return cos_row, sin_row, cos.T, sin.T


def _decay_tables(blk):
    log_g = jnp.log1p(-jnp.exp2(-5.0 - jnp.arange(H_RET, dtype=F32)))
    i = jnp.arange(blk, dtype=F32)
    diff = i[:, None] - i[None, :]
    dmask = jnp.where(diff[None] >= 0, jnp.exp(jnp.maximum(diff, 0.0)[None] * log_g[:, None, None]), 0.0)
    q_dec = jnp.exp((i + 1.0)[None, :] * log_g[:, None])
    k_dec = jnp.exp((blk - 1.0 - i)[None, :] * log_g[:, None])
    c_dec = jnp.exp(blk * log_g)
    return dmask, q_dec, k_dec, c_dec


def kernel(x_prompt, x_sample, state_ret, cache_sb_k, cache_sb_v, cache_fox_k, cache_fox_v, cache_fox_logf,
           page_table, p_prompt, p_sample, g_norm, w_in, b_f, g_ret_gn, g_fox_q, g_fox_k, w_out, w_pe, g_pe, w_pg):
    bp, tp, d = x_prompt.shape
    bs, ts, _ = x_sample.shape
    depth = w_in.shape[0]
    n_pages = page_table.shape[1]
    past_len = n_pages * PAGE_SIZE
    d_ple = p_prompt.shape[-1]
    assert tp % ATT_BLOCK == 0 and tp % RET_CHUNK == 0 and ts == SUBLANES

    cos_p, sin_p, cost_p, sint_p = _rotary_tables(jnp.arange(tp, dtype=jnp.int32))
    cos_s, sin_s, _, _ = _rotary_tables(past_len + jnp.arange(ts, dtype=jnp.int32))
    dm_p, qd_p, kd_p, cd_p = _decay_tables(RET_CHUNK)
    dec_p = (dm_p, qd_p[:, :, None], kd_p[:, None, :], cd_p[:, None, None])
    dm_s, qd_s, kd_s, cd_s = _decay_tables(ts)
    dec_s = (dm_s, qd_s[:, :, None], kd_s[:, :, None], cd_s[:, None, None])
    tq_att = 2 * ATT_BLOCK if tp % (2 * ATT_BLOCK) == 0 else ATT_BLOCK
    later = lambda n: (jnp.arange(n)[:, None] > jnp.arange(n)[None, :]).astype(BF16)
    lmat_page, lmat_att = later(PAGE_SIZE), later(tq_att)
    ar = jnp.arange(tq_att)
    umat = (ar[:, None] <= ar[None, :]).astype(BF16)
    lane_head = jnp.arange(D_FOX) // HEAD_DIM
    seg = (lane_head[:, None] == lane_head[None, :]).astype(BF16)

    def pool_t(c):
        c = jnp.transpose(c, (0, 1, 3, 4, 2))
        return c.reshape(c.shape[0], c.shape[1], c.shape[2] * c.shape[3], c.shape[4])
    pools = tuple(pool_t(c) for c in (cache_sb_k, cache_sb_v, cache_fox_k, cache_fox_v))
    lf_pool = jnp.transpose(cache_fox_logf, (0, 3, 1, 2))

    n_grp = next(g for g in (16, 8, 4, 2, 1) if n_pages % (2 * g) == 0)
    n_head = min(4, n_pages)
    n_tail = next(g for g in (8, 7, 6, 5, 4, 3, 2, 1) if (n_pages - n_head) % g == 0)
    tm_in = next(m for m in (512, 256, ATT_BLOCK) if tp % m == 0)
    n_p, n_s = bp * tp, bs * ts
    tm_out = 512 if n_p % 512 == 0 else ATT_BLOCK

    y_p = x_prompt
    y_s = x_sample.reshape(n_s, d)
    outs = {k: [] for k in ("rs_p", "rs_s", "sks", "svs", "flp", "fks", "fvs", "fls")}
    kv_stacked = None
    for i in range(depth):
        w = w_in[i]
        splits = np.cumsum([D_RET] * 4 + [D_SB] * 4 + [D_FOX] * 4).tolist()
        (w_qr, w_kr, w_vr, w_gr, w_qs, w_ks, w_vs, w_gs, w_qf, w_kf, w_vf, w_gf, w_fl) = jnp.split(w, splits, axis=1)
        w_row = jnp.concatenate([w_qr, w_vr, w_gr, w_qs, w_gs, w_qf, w_gf], axis=1).astype(BF16)
        w_flt = jnp.pad(w_fl.T, ((0, SUBLANES - H_FOX), (0, 0))).astype(BF16)
        w_t = [a.T.astype(BF16) for a in (w_kr, w_ks, w_vs, w_kf, w_vf)] + [w_flt]
        gn = g_norm[i][None, :]
        b_f8 = jnp.pad(b_f[i], (0, SUBLANES - H_FOX))[:, None]
        gq_row = jnp.tile(g_fox_q[i], H_FOX)[None, :]
        gk_row = jnp.tile(g_fox_k[i], H_FOX)[None, :]
        gk_col = g_fox_k[i][:, None]
        ggn = g_ret_gn[i][None, :]
        w_o = w_out[i].astype(BF16)
        w_op = (w_o[:D_RET], w_o[D_RET:D_RET + D_SB], w_o[D_RET + D_SB:],
                w_pe[i].astype(BF16), g_pe[i][None, :], w_pg[i].astype(BF16))

        (qr, vr, gr, qs, gs, qf, gf, krt, kst, vst, kft, vft, lft) = _inproj_prompt(
            y_p, gn, (cos_p, sin_p, cost_p, sint_p), b_f8, gk_col, [w_row] + w_t, tm_in, i, depth, kv_stacked)
        kv_stacked = (kst, vst, kft, vft)
        o_r, s_fin = _ret_prompt(qr, krt, vr, gr, dec_p, ggn)
        o_s = _sb_prompt(qs, kst, vst, gs, lmat_att, tq_att, i)
        o_f = _fox_prompt(qf, kft, vft, lft, gf, gq_row, seg, umat, tq_att, i)
        y_p = _outproj(y_p.reshape(n_p, d), o_r.reshape(n_p, D_RET), o_s.reshape(n_p, D_SB),
                       o_f.reshape(n_p, D_FOX), p_prompt[i].reshape(n_p, d_ple), w_op, tm_out).reshape(bp, tp, d)
        outs["rs_p"].append(s_fin)
        outs["flp"].append(lft[:, :H_FOX])

        proj, flt = _inproj_sample(y_s, gn, [w_row] + w_t)
        flt_b = jnp.transpose(flt.reshape(SUBLANES, bs, ts), (1, 0, 2))
        o_r, s_new = _ret_sample(proj, state_ret[i], (cos_s, sin_s), dec_s, ggn, ts)
        o_f, kf_new, lf_new = _decode_fox(proj, flt_b, pools[2:], lf_pool, page_table, i,
                                          b_f8, gq_row, gk_row, seg, lmat_page, ts, n_grp)
        o_s, sb_acc, sb_car = _decode_sb(proj, pools[:2], page_table, i, lmat_page, ts, n_head, n_tail)
        if n_pages > n_head:
            o_s = lax.cond(jnp.max(sb_car) > SB_DEAD,
                           lambda st: _decode_sb(proj, pools[:2], page_table, i, lmat_page, ts, n_head, n_tail, st),
                           lambda st: o_s, (sb_acc, sb_car))
        y_s = _outproj(y_s, o_r, o_s, o_f, p_sample[i].reshape(n_s, d_ple), w_op, n_s)
        outs["rs_s"].append(s_new)
        piece = lambda k, wd: proj[:, k * PIECE:k * PIECE + wd]
        outs["sks"].append(piece(5, D_SB)); outs["svs"].append(piece(6, D_SB))
        outs["fks"].append(kf_new); outs["fvs"].append(piece(10, D_FOX))
        outs["fls"].append(lf_new[:, :H_FOX])

    st = lambda k: jnp.stack(outs[k], axis=0)
    kv_p = lambda a, h: jnp.transpose(a.reshape(depth, bp, h, HEAD_DIM, tp), (0, 1, 4, 2, 3))
    kv_s = lambda k, h: st(k).reshape(depth, bs, ts, h, HEAD_DIM)
    return (y_p, y_s.reshape(bs, ts, d), st("rs_p"), st("rs_s"),
            kv_p(kv_stacked[0], H_SB), kv_p(kv_stacked[1], H_SB), kv_s("sks", H_SB), kv_s("svs", H_SB),
            kv_p(kv_stacked[2], H_FOX), kv_p(kv_stacked[3], H_FOX), jnp.transpose(st("flp"), (0, 1, 3, 2)),
            kv_s("fks", H_FOX), kv_s("fvs", H_FOX), jnp.transpose(st("fls"), (0, 1, 3, 2)))
```

```python
import functools

import jax
import jax.numpy as jnp
import numpy as np
from jax import lax
from jax.experimental import pallas as pl
from jax.experimental.pallas import tpu as pltpu

F32 = jnp.float32
BF16 = jnp.bfloat16

HEAD_DIM = 64
H_RET, H_SB, H_FOX = 6, 5, 5
D_RET, D_SB, D_FOX = H_RET * HEAD_DIM, H_SB * HEAD_DIM, H_FOX * HEAD_DIM
PAGE_SIZE = 128
RET_CHUNK = 128
ATT_BLOCK = 128
ROW_CHUNK = 32
LANES = 128
SUBLANES = 8
ROPE_BASE = 10000.0
EPS = 1e-6
QK_SCALE = HEAD_DIM ** -0.5
SB_DEAD = -104.0
NEG_BIG = -1e30
PIECE = 384
VMEM_LIMIT = 48 * 1024 * 1024


def _dot(a, b):
    return jnp.dot(a, b, preferred_element_type=F32)


def _dot_nt(a, b):
    return lax.dot_general(a, b, (((1,), (1,)), ((), ())), preferred_element_type=F32)


def _dot_tn(a, b):
    return lax.dot_general(a, b, (((0,), (0,)), ((), ())), preferred_element_type=F32)


def _split2(x):
    hi = x.astype(BF16)
    lo = (x - hi.astype(F32)).astype(BF16)
    return hi, lo


def _split3(x):
    p1 = x.astype(BF16)
    r1 = x - p1.astype(F32)
    p2 = r1.astype(BF16)
    p3 = (r1 - p2.astype(F32)).astype(BF16)
    return p1, p2, p3


def _dot_exact2(x, m):
    hi, lo = _split2(x)
    return _dot(hi, m) + _dot(lo, m)


def _dot_exact3(x, m):
    p1, p2, p3 = _split3(x)
    return _dot(p1, m) + _dot(p2, m) + _dot(p3, m)


def _softplus(z):
    return jnp.maximum(z, 0.0) + jnp.log(1.0 + jnp.exp(-jnp.abs(z)))


def _log_sigmoid(x):
    return jnp.minimum(x, 0.0) - jnp.log1p(jnp.exp(-jnp.abs(x)))


def _silu(g):
    return g * jax.nn.sigmoid(g)


def _rotary_rows(x, cos, sin_signed):
    lane = lax.broadcasted_iota(jnp.int32, (x.shape[0], LANES), 1)
    first_half = (lane % HEAD_DIM) < (HEAD_DIM // 2)
    out = []
    for c in range(x.shape[1] // LANES):
        xs = x[:, c * LANES:(c + 1) * LANES]
        partner = jnp.where(first_half,
                            pltpu.roll(xs, LANES - HEAD_DIM // 2, axis=1),
                            pltpu.roll(xs, HEAD_DIM // 2, axis=1))
        out.append(xs * cos + partner * sin_signed)
    return jnp.concatenate(out, axis=1)


def _inproj_prompt_kernel(x_ref, g_ref, cos_ref, sin_ref, cost_ref, sint_ref, bf_ref, gk_ref,
                          wrow_ref, wkr_ref, wks_ref, wvs_ref, wkf_ref, wvf_ref, wfl_ref, *rest, first_of):
    (qr_ref, vr_ref, gr_ref, qs_ref, gs_ref, qf_ref, gf_ref,
     krt_ref, kst_ref, vst_ref, kft_ref, vft_ref, lft_ref) = rest[-13:]
    if first_of is not None:
        for ref in (kst_ref, vst_ref, kft_ref, vft_ref):
            if first_of > 1:
                ref[1:] = jnp.zeros((first_of - 1,) + ref.shape[1:], F32)
        kst_ref, vst_ref, kft_ref, vft_ref = (r.at[0] for r in (kst_ref, vst_ref, kft_ref, vft_ref))
    x = x_ref[...]
    ms = jnp.mean(x * x, axis=-1, keepdims=True)
    h = (x * lax.rsqrt(ms + EPS) * g_ref[...]).astype(BF16)

    row = _dot_nt(h, wrow_ref[...])
    qr_ref[...] = _rotary_rows(row[:, 0:D_RET], cos_ref[...], sin_ref[...])
    vr_ref[...] = row[:, D_RET:2 * D_RET]
    gr_ref[...] = _silu(row[:, 2 * D_RET:3 * D_RET])
    o = 3 * D_RET
    qs_ref[...] = row[:, o:o + D_SB]
    gs_ref[...] = _silu(row[:, o + D_SB:o + 2 * D_SB])
    qf_ref[...] = row[:, o + 2 * D_SB:o + 3 * D_SB]
    gf_ref[...] = _silu(row[:, o + 3 * D_SB:o + 4 * D_SB])

    half = HEAD_DIM // 2
    krt = _dot_nt(wkr_ref[...], h)
    cost, sint = cost_ref[...], sint_ref[...]
    for hd in range(H_RET):
        x1 = krt[hd * HEAD_DIM:hd * HEAD_DIM + half]
        x2 = krt[hd * HEAD_DIM + half:(hd + 1) * HEAD_DIM]
        krt_ref[hd * HEAD_DIM:hd * HEAD_DIM + half, :] = (x1 * cost - x2 * sint) * QK_SCALE
        krt_ref[hd * HEAD_DIM + half:(hd + 1) * HEAD_DIM, :] = (x1 * sint + x2 * cost) * QK_SCALE

    kst_ref[...] = _dot_nt(wks_ref[...], h)
    vst_ref[...] = _dot_nt(wvs_ref[...], h)
    vft_ref[...] = _dot_nt(wvf_ref[...], h)
    kft = _dot_nt(wkf_ref[...], h)
    gk = gk_ref[...]
    for hd in range(H_FOX):
        xh = kft[hd * HEAD_DIM:(hd + 1) * HEAD_DIM]
        msh = jnp.mean(xh * xh, axis=0, keepdims=True)
        kft_ref[hd * HEAD_DIM:(hd + 1) * HEAD_DIM, :] = xh * lax.rsqrt(msh + EPS) * gk
    lft_ref[...] = _log_sigmoid(_dot_nt(wfl_ref[...], h) + bf_ref[...])


def _inproj_prompt(x, g_norm, tabs, b_f8, gk_col, w, tm, layer, depth, stacked):
    bsz, t, d = x.shape
    n_row = 3 * D_RET + 4 * D_SB
    grid = (bsz, t // tm)
    row_spec = lambda n: pl.BlockSpec((None, tm, n), lambda b, i: (b, i, 0))
    col_spec = lambda n: pl.BlockSpec((None, n, tm), lambda b, i: (b, 0, i))
    full = lambda a: pl.BlockSpec(a.shape, lambda b, i: (0,) * a.ndim)
    cos, sin, cost, sint = tabs
    in_specs = [
        row_spec(d), full(g_norm),
        pl.BlockSpec((tm, LANES), lambda b, i: (i, 0)), pl.BlockSpec((tm, LANES), lambda b, i: (i, 0)),
        pl.BlockSpec((HEAD_DIM // 2, tm), lambda b, i: (0, i)), pl.BlockSpec((HEAD_DIM // 2, tm), lambda b, i: (0, i)),
        full(b_f8), full(gk_col),
    ] + [full(a) for a in w]
    rs = lambda n: jax.ShapeDtypeStruct((bsz, t, n), F32)
    cs = lambda n: jax.ShapeDtypeStruct((bsz, n, t), F32)
    stk = jax.ShapeDtypeStruct((depth, bsz, D_SB, t), F32)
    aliases, extra = {}, []
    if stacked is None:
        stk_spec = pl.BlockSpec((depth, None, D_SB, tm), lambda b, i: (0, b, 0, i))
    else:
        stk_spec = pl.BlockSpec((None, None, D_SB, tm), lambda b, i: (layer, b, 0, i))
        extra = list(stacked)
        in_specs += [pl.BlockSpec(memory_space=pl.ANY)] * len(extra)
        first_in, first_out = len(in_specs) - len(extra), 8
        aliases = {first_in + k: first_out + k for k in range(len(extra))}
    out_shape = [rs(D_RET), rs(D_RET), rs(D_RET), rs(D_SB), rs(D_SB), rs(D_FOX), rs(D_FOX),
                 cs(D_RET), stk, stk, stk, stk, cs(SUBLANES)]
    out_specs = [row_spec(D_RET)] * 3 + [row_spec(D_SB)] * 4 + \
                [col_spec(D_RET)] + [stk_spec] * 4 + [col_spec(SUBLANES)]
    return pl.pallas_call(
        functools.partial(_inproj_prompt_kernel, first_of=depth if stacked is None else None),
        grid=grid, in_specs=in_specs, out_specs=out_specs, out_shape=out_shape,
        input_output_aliases=aliases,
        compiler_params=pltpu.CompilerParams(dimension_semantics=("parallel", "parallel"),
                                             vmem_limit_bytes=VMEM_LIMIT),
        name="inproj_prompt",
    )(x, g_norm, cos, sin, cost, sint, b_f8, gk_col, *w, *extra)


def _ret_prompt_kernel(qr_ref, krt_ref, vr_ref, gr_ref, dmask_ref, qdec_ref, kdec_ref, cdec_ref, ggn_ref,
                       o_ref, sfin_ref, s_scr):
    c = pl.program_id(1)

    @pl.when(c == 0)
    def _():
        s_scr[...] = jnp.zeros_like(s_scr)

    for sub in range(qr_ref.shape[0] // RET_CHUNK):
        rows = slice(sub * RET_CHUNK, (sub + 1) * RET_CHUNK)
        for hd in range(H_RET):
            sl = slice(hd * HEAD_DIM, (hd + 1) * HEAD_DIM)
            q = qr_ref[rows, sl].astype(BF16)
            kt = krt_ref[sl, rows]
            v = vr_ref[rows, sl].astype(BF16)
            sc = _dot(q, kt.astype(BF16)) * dmask_ref[hd]
            inner = _dot(sc.astype(BF16), v)
            s_old = s_scr[hd]
            cross = _dot(q, s_old.astype(BF16)) * qdec_ref[hd]
            o = inner + cross
            s_scr[hd] = s_old * cdec_ref[hd] + _dot((kt * kdec_ref[hd]).astype(BF16), v)
            mu = jnp.mean(o, axis=-1, keepdims=True)
            dev = o - mu
            var = jnp.mean(dev * dev, axis=-1, keepdims=True)
            o_ref[rows, sl] = dev * lax.rsqrt(var + EPS) * ggn_ref[:, sl] * gr_ref[rows, sl]

    @pl.when(c == pl.num_programs(1) - 1)
    def _():
        sfin_ref[...] = s_scr[...]


def _ret_prompt(qr, krt, vr, gr, dec, ggn):
    bsz, t, _ = qr.shape
    blk = RET_CHUNK
    dmask, qdec, kdec, cdec = dec
    row = pl.BlockSpec((None, blk, D_RET), lambda b, c: (b, c, 0))
    full = lambda a: pl.BlockSpec(a.shape, lambda b, c: (0,) * a.ndim)
    return pl.pallas_call(
        _ret_prompt_kernel, grid=(bsz, t // blk),
        in_specs=[row, pl.BlockSpec((None, D_RET, blk), lambda b, c: (b, 0, c)), row, row,
                  full(dmask), full(qdec), full(kdec), full(cdec), full(ggn)],
        out_specs=[row, pl.BlockSpec((None, H_RET, HEAD_DIM, HEAD_DIM), lambda b, c: (b, 0, 0, 0))],
        out_shape=[jax.ShapeDtypeStruct((bsz, t, D_RET), F32),
                   jax.ShapeDtypeStruct((bsz, H_RET, HEAD_DIM, HEAD_DIM), F32)],
        scratch_shapes=[pltpu.VMEM((H_RET, HEAD_DIM, HEAD_DIM), F32)],
        compiler_params=pltpu.CompilerParams(dimension_semantics=("parallel", "arbitrary"),
                                             vmem_limit_bytes=VMEM_LIMIT),
        name="ret_prompt",
    )(qr, krt, vr, gr, dmask, qdec, kdec, cdec, ggn)


def _sb_prompt_kernel(q_ref, kt_ref, vt_ref, g_ref, lmat_ref, o_ref,
                      q_scr, car_scr, acc_scr, z_scr, ls_scr, hi_scr, lo_scr, a_scr, tot_scr):
    i = pl.program_id(1)
    tq = q_ref.shape[0]
    reps = tq // LANES
    for hd in range(H_SB):
        sl = slice(hd * HEAD_DIM, (hd + 1) * HEAD_DIM)
        q_scr[hd] = (q_ref[:, sl] * QK_SCALE).astype(BF16)
    car_scr[...] = jnp.zeros_like(car_scr)
    acc_scr[...] = jnp.zeros_like(acc_scr)

    def block(j, diag):
        off = pl.multiple_of(j * tq, tq)
        lmat = lmat_ref[...]
        for hd in range(H_SB):
            sl = slice(hd * HEAD_DIM, (hd + 1) * HEAD_DIM)
            z_scr[hd] = _dot(q_scr[hd], kt_ref[sl, pl.ds(off, tq)].astype(BF16))
        chunk_masks = []
        for r in range(0, tq, ROW_CHUNK):
            if diag:
                rowi = lax.broadcasted_iota(jnp.int32, (ROW_CHUNK, tq), 0) + r
                coli = lax.broadcasted_iota(jnp.int32, (ROW_CHUNK, tq), 1)
                chunk_masks.append(coli < rowi)
            else:
                chunk_masks.append(None)
        for hd in range(H_SB):
            for ci, r in enumerate(range(0, tq, ROW_CHUNK)):
                rows = slice(r, r + ROW_CHUNK)
                z = z_scr[hd, rows, :]
                sp = _softplus(z)
                ls_scr[hd, rows, :] = z - sp
                if diag:
                    sp = jnp.where(chunk_masks[ci], sp, 0.0)
                hi, lo = _split2(sp)
                hi_scr[hd, rows, :] = hi
                lo_scr[hd, rows, :] = lo
                tot = jnp.sum(sp, axis=-1, keepdims=True)
                tot_scr[hd, rows, :] = jnp.broadcast_to(tot, (ROW_CHUNK, LANES))
        for hd in range(H_SB):
            z_scr[hd] = _dot(hi_scr[hd], lmat) + _dot(lo_scr[hd], lmat)
        for hd in range(H_SB):
            for ci, r in enumerate(range(0, tq, ROW_CHUNK)):
                rows = slice(r, r + ROW_CHUNK)
                carry = car_scr[hd, rows, :]
                a = jnp.exp(ls_scr[hd, rows, :] - z_scr[hd, rows, :] + jnp.tile(carry, (1, reps)))
                if diag:
                    a = jnp.where(chunk_masks[ci], a, 0.0)
                a_scr[hd, rows, :] = a.astype(BF16)
                car_scr[hd, rows, :] = carry - tot_scr[hd, rows, :]
        for hd in range(H_SB):
            sl = slice(hd * HEAD_DIM, (hd + 1) * HEAD_DIM)
            acc_scr[hd] += _dot_nt(a_scr[hd], vt_ref[sl, pl.ds(off, tq)].astype(BF16))
        return jnp.max(car_scr[...])

    cmax = block(i, True)

    def cond(st):
        j, cmax = st
        return jnp.logical_and(j >= 0, cmax > SB_DEAD)

    def body(st):
        j, _ = st
        return j - 1, block(j, False)

    lax.while_loop(cond, body, (i - 1, cmax))
    for hd in range(H_SB):
        sl = slice(hd * HEAD_DIM, (hd + 1) * HEAD_DIM)
        o_ref[:, sl] = acc_scr[hd] * g_ref[:, sl]


def _sb_prompt(qs, kst, vst, gs, lmat, tq, layer):
    bsz, t, _ = qs.shape
    row = pl.BlockSpec((None, tq, D_SB), lambda b, i: (b, i, 0))
    seq = pl.BlockSpec((None, None, D_SB, t), lambda b, i: (layer, b, 0, 0))
    return pl.pallas_call(
        _sb_prompt_kernel, grid=(bsz, t // tq),
        in_specs=[row, seq, seq, row, pl.BlockSpec(lmat.shape, lambda b, i: (0, 0))],
        out_specs=row, out_shape=jax.ShapeDtypeStruct((bsz, t, D_SB), F32),
        scratch_shapes=[pltpu.VMEM((H_SB, tq, HEAD_DIM), BF16), pltpu.VMEM((H_SB, tq, LANES), F32),
                        pltpu.VMEM((H_SB, tq, HEAD_DIM), F32),
                        pltpu.VMEM((H_SB, tq, tq), F32), pltpu.VMEM((H_SB, tq, tq), F32),
                        pltpu.VMEM((H_SB, tq, tq), BF16), pltpu.VMEM((H_SB, tq, tq), BF16),
                        pltpu.VMEM((H_SB, tq, tq), BF16), pltpu.VMEM((H_SB, tq, LANES), F32)],
        compiler_params=pltpu.CompilerParams(dimension_semantics=("parallel", "arbitrary"),
                                             vmem_limit_bytes=VMEM_LIMIT),
        name="sb_prompt",
    )(qs, kst, vst, gs, lmat)


def _fox_prompt_kernel(q_ref, kt_ref, vt_ref, lft_ref, g_ref, gq_ref, seg_ref, umat_ref, o_ref,
                       c_scr, q_scr, m_scr, acc_scr, s_scr, p_scr, a_scr):
    i = pl.program_id(1)
    tq = q_ref.shape[0]
    t = kt_ref.shape[1]
    cw = umat_ref.shape[0]

    @pl.when(i == 0)
    def _():
        umat = umat_ref[...]
        carry = jnp.zeros((SUBLANES, 1), F32)
        for c in range(t // cw):
            cs = _dot_exact3(lft_ref[:, c * cw:(c + 1) * cw], umat) + carry
            c_scr[:, c * cw:(c + 1) * cw] = cs
            carry = cs[:, cw - 1:cw]

    qraw = q_ref[...]
    ssq = _dot_exact2(qraw * qraw, seg_ref[...])
    qn = qraw * lax.rsqrt(ssq * (1.0 / HEAD_DIM) + EPS) * gq_ref[...] * QK_SCALE
    for hd in range(H_FOX):
        q_scr[hd] = qn[:, hd * HEAD_DIM:(hd + 1) * HEAD_DIM].astype(BF16)
    m_scr[...] = jnp.full_like(m_scr, NEG_BIG)
    acc_scr[...] = jnp.zeros_like(acc_scr)
    rowi = lax.broadcasted_iota(jnp.int32, (tq, tq), 0)
    coli = lax.broadcasted_iota(jnp.int32, (tq, tq), 1)
    diag_mask = coli <= rowi

    def block(off, width, mask):
        ones_rows = jnp.ones((HEAD_DIM, width), BF16)
        for hd in range(H_FOX):
            sl = slice(hd * HEAD_DIM, (hd + 1) * HEAD_DIM)
            kt = kt_ref[sl, pl.ds(off, width)].astype(BF16)
            s = _dot(q_scr[hd], kt) - c_scr[hd:hd + 1, pl.ds(off, width)]
            if mask is not None:
                s = jnp.where(mask, s, NEG_BIG)
            s_scr[hd, :, :width] = s
        for hd in range(H_FOX):
            for r in range(0, tq, ROW_CHUNK):
                rows = slice(r, r + ROW_CHUNK)
                s = s_scr[hd, rows, :width]
                m_old = m_scr[hd, rows, :]
                mx = jnp.max(s, axis=-1, keepdims=True)
                m_new = jnp.maximum(m_old, jnp.broadcast_to(mx, (ROW_CHUNK, LANES)))
                p_scr[hd, rows, :width] = jnp.exp(s - jnp.tile(m_new, (1, width // LANES))).astype(BF16)
                a_scr[hd, rows, :] = jnp.exp(m_old - m_new)
                m_scr[hd, rows, :] = m_new
        for hd in range(H_FOX):
            sl = slice(hd * HEAD_DIM, (hd + 1) * HEAD_DIM)
            vt = jnp.concatenate([vt_ref[sl, pl.ds(off, width)].astype(BF16), ones_rows], axis=0)
            acc_scr[hd] = a_scr[hd] * acc_scr[hd] + _dot_nt(p_scr[hd, :, :width], vt)

    def body(j, carry):
        block(pl.multiple_of(j * 2 * tq, 2 * tq), 2 * tq, None)
        return carry

    lax.fori_loop(0, i // 2, body, 0)

    @pl.when(i % 2 == 1)
    def _():
        block(pl.multiple_of((i - 1) * tq, tq), tq, None)

    block(pl.multiple_of(i * tq, tq), tq, diag_mask)
    for hd in range(H_FOX):
        sl = slice(hd * HEAD_DIM, (hd + 1) * HEAD_DIM)
        acc = acc_scr[hd]
        o_ref[:, sl] = acc[:, :HEAD_DIM] / acc[:, HEAD_DIM:HEAD_DIM + 1] * g_ref[:, sl]


def _fox_prompt(qf, kft, vft, lft, gf, gq_row, seg, umat, tq, layer):
    bsz, t, _ = qf.shape
    row = pl.BlockSpec((None, tq, D_FOX), lambda b, i: (b, i, 0))
    seq = pl.BlockSpec((None, None, D_FOX, t), lambda b, i: (layer, b, 0, 0))
    full = lambda a: pl.BlockSpec(a.shape, lambda b, i: (0,) * a.ndim)
    return pl.pallas_call(
        _fox_prompt_kernel, grid=(bsz, t // tq),
        in_specs=[row, seq, seq, pl.BlockSpec((None, SUBLANES, t), lambda b, i: (b, 0, 0)), row,
                  full(gq_row), full(seg), full(umat)],
        out_specs=row, out_shape=jax.ShapeDtypeStruct((bsz, t, D_FOX), F32),
        scratch_shapes=[pltpu.VMEM((SUBLANES, t), F32), pltpu.VMEM((H_FOX, tq, HEAD_DIM), BF16),
                        pltpu.VMEM((H_FOX, tq, LANES), F32), pltpu.VMEM((H_FOX, tq, 2 * HEAD_DIM), F32),
                        pltpu.VMEM((H_FOX, tq, 2 * tq), F32), pltpu.VMEM((H_FOX, tq, 2 * tq), BF16),
                        pltpu.VMEM((H_FOX, tq, LANES), F32)],
        compiler_params=pltpu.CompilerParams(dimension_semantics=("parallel", "arbitrary"),
                                             vmem_limit_bytes=VMEM_LIMIT),
        name="fox_prompt",
    )(qf, kft, vft, lft, gf, gq_row, seg, umat)


def _outproj_kernel(y_ref, or_ref, os_ref, of_ref, p_ref, wor_ref, wos_ref, wof_ref, wpe_ref, gpe_ref, wpg_ref,
                    o_ref):
    m = (_dot(or_ref[...].astype(BF16), wor_ref[...]) + _dot(os_ref[...].astype(BF16), wos_ref[...])
         + _dot(of_ref[...].astype(BF16), wof_ref[...]))
    y1 = y_ref[...] + m
    ms = jnp.mean(y1 * y1, axis=-1, keepdims=True)
    n = (y1 * lax.rsqrt(ms + EPS) * gpe_ref[...]).astype(BF16)
    gate = jax.nn.sigmoid(_dot(n, wpg_ref[...]))
    pe = _dot(p_ref[...].astype(BF16), wpe_ref[...])
    o_ref[...] = y1 + pe * gate


def _outproj(y, o_r, o_s, o_f, p, w, tm):
    n, d = y.shape
    row = lambda a: pl.BlockSpec((tm, a.shape[1]), lambda i: (i, 0))
    full = lambda a: pl.BlockSpec(a.shape, lambda i: (0,) * a.ndim)
    return pl.pallas_call(
        _outproj_kernel, grid=(n // tm,),
        in_specs=[row(y), row(o_r), row(o_s), row(o_f), row(p)] + [full(a) for a in w],
        out_specs=row(y), out_shape=jax.ShapeDtypeStruct((n, d), F32),
        compiler_params=pltpu.CompilerParams(dimension_semantics=("parallel",), vmem_limit_bytes=VMEM_LIMIT),
        name="outproj",
    )(y, o_r, o_s, o_f, p, *w)


def _inproj_sample_kernel(x_ref, g_ref, wrow_ref, wkr_ref, wks_ref, wvs_ref, wkf_ref, wvf_ref, wfl_ref,
                          o_ref, flt_ref):
    x = x_ref[...]
    ms = jnp.mean(x * x, axis=-1, keepdims=True)
    h = (x * lax.rsqrt(ms + EPS) * g_ref[...]).astype(BF16)
    row = _dot_nt(h, wrow_ref[...])
    o_ref[...] = jnp.zeros_like(o_ref)

    def put(slot, val):
        o_ref[:, slot * PIECE:slot * PIECE + val.shape[1]] = val

    put(0, row[:, 0:D_RET])
    put(2, row[:, D_RET:2 * D_RET])
    put(3, _silu(row[:, 2 * D_RET:3 * D_RET]))
    o = 3 * D_RET
    put(4, row[:, o:o + D_SB])
    put(7, _silu(row[:, o + D_SB:o + 2 * D_SB]))
    put(8, row[:, o + 2 * D_SB:o + 3 * D_SB])
    put(11, _silu(row[:, o + 3 * D_SB:o + 4 * D_SB]))
    put(1, _dot_nt(h, wkr_ref[...]))
    put(5, _dot_nt(h, wks_ref[...]))
    put(6, _dot_nt(h, wvs_ref[...]))
    put(9, _dot_nt(h, wkf_ref[...]))
    put(10, _dot_nt(h, wvf_ref[...]))
    flt_ref[...] = _dot_nt(wfl_ref[...], h)


def _inproj_sample(x, g_norm, w):
    n, d = x.shape
    ncol = 12 * PIECE
    full = lambda a: pl.BlockSpec(a.shape, lambda i: (0,) * a.ndim)
    return pl.pallas_call(
        _inproj_sample_kernel, grid=(1,),
        in_specs=[full(x), full(g_norm)] + [full(a) for a in w],
        out_specs=[pl.BlockSpec((n, ncol), lambda i: (0, 0)), pl.BlockSpec((SUBLANES, n), lambda i: (0, 0))],
        out_shape=[jax.ShapeDtypeStruct((n, ncol), F32), jax.ShapeDtypeStruct((SUBLANES, n), F32)],
        compiler_params=pltpu.CompilerParams(dimension_semantics=("arbitrary",), vmem_limit_bytes=VMEM_LIMIT),
        name="inproj_sample",
    )(x, g_norm, *w)


def _ret_sample_kernel(q_ref, k_ref, v_ref, g_ref, s0_ref, cos_ref, sin_ref, dmask_ref, qdec_ref, kdec_ref,
                       cdec_ref, ggn_ref, o_ref, s_ref):
    cos, sin = cos_ref[...], sin_ref[...]
    qr = _rotary_rows(q_ref[...], cos, sin)
    kr = _rotary_rows(k_ref[...], cos, sin) * QK_SCALE
    for hd in range(H_RET):
        sl = slice(hd * HEAD_DIM, (hd + 1) * HEAD_DIM)
        q = qr[:, sl].astype(BF16)
        k = kr[:, sl]
        v = v_ref[:, sl].astype(BF16)
        sc = _dot_nt(q, k.astype(BF16)) * dmask_ref[hd]
        inner = _dot(sc.astype(BF16), v)
        s_old = s0_ref[hd]
        cross = _dot(q, s_old.astype(BF16)) * qdec_ref[hd]
        o = inner + cross
        s_ref[hd] = s_old * cdec_ref[hd] + _dot_tn((k * kdec_ref[hd]).astype(BF16), v)
        mu = jnp.mean(o, axis=-1, keepdims=True)
        dev = o - mu
        var = jnp.mean(dev * dev, axis=-1, keepdims=True)
        y = dev * lax.rsqrt(var + EPS) * ggn_ref[:, sl]
        o_ref[:, sl] = y * g_ref[:, sl]


def _ret_sample(proj, s0, tabs, dec, ggn, ts):
    n = proj.shape[0]
    bsz = n // ts
    cos, sin = tabs
    dmask, qdec, kdec, cdec = dec
    piece = lambda k: pl.BlockSpec((ts, PIECE), lambda b, k=k: (b, k))
    full = lambda a: pl.BlockSpec(a.shape, lambda b: (0,) * a.ndim)
    st = pl.BlockSpec((None, H_RET, HEAD_DIM, HEAD_DIM), lambda b: (b, 0, 0, 0))
    return pl.pallas_call(
        _ret_sample_kernel, grid=(bsz,),
        in_specs=[piece(0), piece(1), piece(2), piece(3), st, full(cos), full(sin),
                  full(dmask), full(qdec), full(kdec), full(cdec), full(ggn)],
        out_specs=[pl.BlockSpec((ts, D_RET), lambda b: (b, 0)), st],
        out_shape=[jax.ShapeDtypeStruct((n, D_RET), F32),
                   jax.ShapeDtypeStruct((bsz, H_RET, HEAD_DIM, HEAD_DIM), F32)],
        compiler_params=pltpu.CompilerParams(dimension_semantics=("parallel",), vmem_limit_bytes=VMEM_LIMIT),
        name="ret_sample",
    )(proj, proj, proj, proj, s0, cos, sin, dmask, qdec, kdec, cdec, ggn)


def _head_rows(x, ts):
    rows = H_SB * ts
    r = lax.broadcasted_iota(jnp.int32, (rows, D_SB), 0)
    c = lax.broadcasted_iota(jnp.int32, (rows, D_SB), 1)
    tiled = jnp.concatenate([x] * H_SB, axis=0)
    return jnp.where(r // ts == c // HEAD_DIM, tiled, 0.0)


def _head_cols(x, ts):
    c = lax.broadcasted_iota(jnp.int32, (ts, D_SB), 1)
    out = jnp.zeros((ts, D_SB), F32)
    for hd in range(H_SB):
        out = out + jnp.where(c // HEAD_DIM == hd, x[hd * ts:(hd + 1) * ts], 0.0)
    return out


def _new_token_masks(ts, strict):
    rows = H_SB * ts
    tpos = lax.broadcasted_iota(jnp.int32, (rows, PAGE_SIZE), 0) % ts
    coli = lax.broadcasted_iota(jnp.int32, (rows, PAGE_SIZE), 1)
    return coli < tpos if strict else coli <= tpos


def _sb_blocks(zs, v_dots, lmat, acc, car):
    for (z, mask), v_dot in zip(zs, v_dots):
        sp = _softplus(z)
        ls = z - sp
        if mask is not None:
            sp = jnp.where(mask, sp, 0.0)
        a = jnp.exp(ls - _dot_exact2(sp, lmat) + car)
        if mask is not None:
            a = jnp.where(mask, a, 0.0)
        acc = acc + v_dot(a.astype(BF16))
        car = car - jnp.sum(sp, axis=-1, keepdims=True)
    return acc, car


def _decode_sb_kernel(pt_ref, qs_ref, ks_ref, vs_ref, gs_ref, lmat_ref, *rest, n_grp, ts, head):
    if head:
        pages = rest[:2 * n_grp]
        os_ref, acco_ref, caro_ref = rest[2 * n_grp:]
    else:
        acci_ref, cari_ref = rest[:2]
        pages = rest[2:2 + 2 * n_grp]
        os_ref, qbs_scr, acc_scr, car_scr = rest[2 + 2 * n_grp:]
    lmat = lmat_ref[...]
    order = list(range(n_grp - 1, -1, -1))

    def page_blocks(qbs):
        zs, dots = [], []
        for gi in order:
            kt_ref, vt_ref = pages[2 * gi:2 * gi + 2]
            zs.append((_dot(qbs, kt_ref[...].astype(BF16)), None))
            dots.append(lambda a, r=vt_ref: _dot_nt(a, r[...].astype(BF16)))
        return zs, dots

    qbs_new = _head_rows(qs_ref[:, :D_SB] * QK_SCALE, ts).astype(BF16)
    if head:
        pad = jnp.zeros((PAGE_SIZE - ts, D_SB), F32)
        k_new = jnp.concatenate([ks_ref[:, :D_SB], pad], axis=0).astype(BF16)
        v_new = jnp.concatenate([vs_ref[:, :D_SB], pad], axis=0).astype(BF16)
        zs, dots = page_blocks(qbs_new)
        acc, car = _sb_blocks([(_dot_nt(qbs_new, k_new), _new_token_masks(ts, True))] + zs,
                              [lambda a: _dot(a, v_new)] + dots, lmat,
                              jnp.zeros((H_SB * ts, D_SB), F32), jnp.zeros((H_SB * ts, 1), F32))
        acco_ref[...] = acc
        caro_ref[...] = jnp.broadcast_to(car, caro_ref.shape)
        os_ref[...] = _head_cols(acc, ts) * gs_ref[:, :D_SB]
    else:
        s = pl.program_id(1)

        @pl.when(s == 0)
        def _():
            qbs_scr[...] = qbs_new
            acc_scr[...] = acci_ref[...]
            car_scr[...] = cari_ref[:, 0:1]

        @pl.when(jnp.max(car_scr[...]) > SB_DEAD)
        def _():
            zs, dots = page_blocks(qbs_scr[...])
            acc, car = _sb_blocks(zs, dots, lmat, acc_scr[...], car_scr[...])
            acc_scr[...] = acc
            car_scr[...] = car

        @pl.when(s == pl.num_programs(1) - 1)
        def _():
            os_ref[...] = _head_cols(acc_scr[...], ts) * gs_ref[:, :D_SB]


def _decode_sb(proj, pools, page_table, layer, lmat, ts, n_head, n_grp, state=None):
    n = proj.shape[0]
    bsz, n_pages = page_table.shape
    rows = H_SB * ts
    head = state is None
    per_step = n_head if head else n_grp
    n_steps = 1 if head else (n_pages - n_head) // n_grp
    last = n_pages - 1 if head else n_pages - n_head - 1
    piece = lambda k: pl.BlockSpec((ts, PIECE), lambda b, s, pt, k=k: (b, k))

    def page_spec(gi):
        return pl.BlockSpec((None, None, D_SB, PAGE_SIZE),
                            lambda b, s, pt, gi=gi: (layer, pt[b, last - s * per_step - (per_step - 1 - gi)], 0, 0))

    page_specs = [page_spec(gi) for gi in range(per_step) for _ in range(2)]
    page_args = [p for _ in range(per_step) for p in pools]
    out_row = pl.BlockSpec((ts, D_SB), lambda b, s, pt: (b, 0))
    st_acc = pl.BlockSpec((rows, D_SB), lambda b, s, pt: (b, 0))
    st_car = pl.BlockSpec((rows, LANES), lambda b, s, pt: (b, 0))
    in_specs = [piece(4), piece(5), piece(6), piece(7), pl.BlockSpec(lmat.shape, lambda b, s, pt: (0, 0))]
    args = [proj] * 4 + [lmat]
    if head:
        out_specs = [out_row, st_acc, st_car]
        out_shape = [jax.ShapeDtypeStruct((n, D_SB), F32), jax.ShapeDtypeStruct((bsz * rows, D_SB), F32),
                     jax.ShapeDtypeStruct((bsz * rows, LANES), F32)]
        scratch = []
    else:
        in_specs += [st_acc, st_car]
        args += list(state)
        out_specs = out_row
        out_shape = jax.ShapeDtypeStruct((n, D_SB), F32)
        scratch = [pltpu.VMEM((rows, D_SB), BF16), pltpu.VMEM((rows, D_SB), F32), pltpu.VMEM((rows, 1), F32)]
    grid_spec = pltpu.PrefetchScalarGridSpec(
        num_scalar_prefetch=1, grid=(bsz, n_steps), in_specs=in_specs + page_specs, out_specs=out_specs,
        scratch_shapes=scratch)
    return pl.pallas_call(
        functools.partial(_decode_sb_kernel, n_grp=per_step, ts=ts, head=head), grid_spec=grid_spec,
        out_shape=out_shape,
        compiler_params=pltpu.CompilerParams(dimension_semantics=("parallel", "arbitrary"),
                                             vmem_limit_bytes=VMEM_LIMIT),
        name="decode_sb_head" if head else "decode_sb_tail",
    )(page_table, *args, *page_args)


def _decode_fox_kernel(pt_ref, qf_ref, kf_ref, vf_ref, gf_ref, flt_ref, bf_ref, gq_ref, gk_ref, seg_ref, lmat_ref,
                       kpool_ref, vpool_ref, lfpool_ref, of_ref, kfn_ref, lfn_ref,
                       kbuf, vbuf, lfall, sem, lfsem, *, n_grp, ts, layer):
    b = pl.program_id(0)
    n_seq = pl.num_programs(0)
    n_pages = pt_ref.shape[1]
    n_groups = n_pages // n_grp
    rows = H_FOX * ts
    lmat = lmat_ref[...]

    def group_copies(seq, g, slot):
        out = []
        for gi in range(n_grp):
            page = pt_ref[seq, n_pages - 1 - g * n_grp - gi]
            out.append(pltpu.make_async_copy(kpool_ref.at[layer, page], kbuf.at[slot, gi], sem.at[slot, 0]))
            out.append(pltpu.make_async_copy(vpool_ref.at[layer, page], vbuf.at[slot, gi], sem.at[slot, 1]))
        return out

    lf_copy = pltpu.make_async_copy(lfpool_ref.at[layer], lfall, lfsem.at[0])

    @pl.when(b == 0)
    def _():
        lf_copy.start()
        for cp in group_copies(b, 0, 0):
            cp.start()

    def rep_heads(x):
        return jnp.concatenate([jnp.broadcast_to(x[hd:hd + 1], (ts, x.shape[1])) for hd in range(H_FOX)], axis=0)

    def fox_block(sc, mask, v_dot, m_old, l, acc):
        if mask is not None:
            sc = jnp.where(mask, sc, NEG_BIG)
        m_new = jnp.maximum(m_old, jnp.max(sc, axis=-1, keepdims=True))
        alpha = jnp.exp(m_old - m_new)
        p = jnp.exp(sc - m_new)
        l = alpha * l + jnp.sum(p, axis=-1, keepdims=True)
        acc = alpha * acc + v_dot(p.astype(BF16))
        return m_new, l, acc

    pad = jnp.zeros((PAGE_SIZE - ts, D_FOX), F32)
    seg = seg_ref[...]
    qraw = qf_ref[:, :D_FOX]
    qn = qraw * lax.rsqrt(_dot_exact2(qraw * qraw, seg) * (1.0 / HEAD_DIM) + EPS) * gq_ref[...] * QK_SCALE
    qbf = _head_rows(qn, ts).astype(BF16)
    kraw = kf_ref[:, :D_FOX]
    kn = kraw * lax.rsqrt(_dot_exact2(kraw * kraw, seg) * (1.0 / HEAD_DIM) + EPS) * gk_ref[...]
    kfn_ref[...] = kn
    lf_new = _log_sigmoid(flt_ref[...] + bf_ref[...])
    lfn_ref[...] = lf_new
    lf_pad = jnp.concatenate([lf_new, jnp.zeros((SUBLANES, PAGE_SIZE - ts), F32)], axis=1)
    lf_rows = rep_heads(lf_pad)
    cn = jnp.sum(lf_rows, axis=-1, keepdims=True) - _dot_exact3(lf_rows, lmat)
    kn_pad = jnp.concatenate([kn, pad], axis=0).astype(BF16)
    vf_new = jnp.concatenate([vf_ref[:, :D_FOX], pad], axis=0).astype(BF16)
    m, l, acc = fox_block(_dot_nt(qbf, kn_pad) - cn, _new_token_masks(ts, False), lambda p: _dot(p, vf_new),
                          jnp.full((rows, 1), NEG_BIG, F32), jnp.zeros((rows, 1), F32),
                          jnp.zeros((rows, D_FOX), F32))

    @pl.when(b == 0)
    def _():
        lf_copy.wait()

    rcar = jnp.zeros((rows, 1), F32)
    for g in range(n_groups):
        slot = g % 2
        if g + 1 < n_groups:
            for cp in group_copies(b, g + 1, 1 - slot):
                cp.start()
        else:
            @pl.when(b + 1 < n_seq)
            def _():
                for cp in group_copies(b + 1, 0, 1 - slot):
                    cp.start()
        for cp in group_copies(b, g, slot):
            cp.wait()
        lf_rows = []
        for gi in range(n_grp):
            page = pt_ref[b, n_pages - 1 - g * n_grp - gi]
            lf_pg = lfall[:, pl.ds(page, 1), :]
            lf_rows.append(jnp.concatenate([jnp.broadcast_to(lf_pg[hd], (ts, PAGE_SIZE)) for hd in range(H_FOX)],
                                           axis=0))
        later = _dot_exact3(jnp.concatenate(lf_rows, axis=0), lmat)
        biases = []
        for k, lr in enumerate(lf_rows):
            biases.append(later[k * rows:(k + 1) * rows] + rcar)
            rcar = rcar + jnp.sum(lr, axis=-1, keepdims=True)
        kt_all = jnp.concatenate([kbuf[slot, gi].astype(BF16) for gi in range(n_grp)], axis=1)
        vt_all = jnp.concatenate([vbuf[slot, gi].astype(BF16) for gi in range(n_grp)], axis=1)
        sc = _dot(qbf, kt_all) + jnp.concatenate(biases, axis=1)
        m, l, acc = fox_block(sc, None, lambda p, vt_all=vt_all: _dot_nt(p, vt_all), m, l, acc)

    of_ref[...] = _head_cols(acc / l, ts) * gf_ref[:, :D_FOX]


def _decode_fox(proj, flt, pools, lf_pool, page_table, layer, b_f8, gq_row, gk_row, seg, lmat, ts, n_grp):
    n = proj.shape[0]
    bsz, n_pages = page_table.shape
    assert n_pages % (2 * n_grp) == 0
    piece = lambda k: pl.BlockSpec((ts, PIECE), lambda b, pt, k=k: (b, k))
    full = lambda a: pl.BlockSpec(a.shape, lambda b, pt: (0,) * a.ndim)
    hbm = pl.BlockSpec(memory_space=pl.ANY)
    out_row = pl.BlockSpec((ts, D_FOX), lambda b, pt: (b, 0))
    small = pl.BlockSpec((None, SUBLANES, ts), lambda b, pt: (b, 0, 0))
    grid_spec = pltpu.PrefetchScalarGridSpec(
        num_scalar_prefetch=1, grid=(bsz,),
        in_specs=[piece(8), piece(9), piece(10), piece(11), small,
                  full(b_f8), full(gq_row), full(gk_row), full(seg), full(lmat), hbm, hbm, hbm],
        out_specs=[out_row, out_row, small],
        scratch_shapes=[pltpu.VMEM((2, n_grp, D_FOX, PAGE_SIZE), F32), pltpu.VMEM((2, n_grp, D_FOX, PAGE_SIZE), F32),
                        pltpu.VMEM(lf_pool.shape[1:], F32), pltpu.SemaphoreType.DMA((2, 2)),
                        pltpu.SemaphoreType.DMA((1,))],
    )
    return pl.pallas_call(
        functools.partial(_decode_fox_kernel, n_grp=n_grp, ts=ts, layer=layer), grid_spec=grid_spec,
        out_shape=[jax.ShapeDtypeStruct((n, D_FOX), F32), jax.ShapeDtypeStruct((n, D_FOX), F32),
                   jax.ShapeDtypeStruct((bsz, SUBLANES, ts), F32)],
        compiler_params=pltpu.CompilerParams(dimension_semantics=("arbitrary",), vmem_limit_bytes=VMEM_LIMIT),
        name="decode_fox",
    )(page_table, *([proj] * 4), flt, b_f8, gq_row, gk_row, seg, lmat, *pools, lf_pool)


def _rotary_tables(pos):
    half = HEAD_DIM // 2
    inv = ROPE_BASE ** (-jnp.arange(half, dtype=F32) / half)
    ang = pos.astype(F32)[:, None] * inv[None, :]
    cos, sin = jnp.cos(ang), jnp.sin(ang)
    reps = LANES // HEAD_DIM
    cos_row = jnp.tile(jnp.concatenate([cos, cos], axis=1), (1, reps))
    sin_row = jnp.tile(jnp.concatenate([-sin, sin], axis=1), (1, reps))
    return cos_row, sin_row, cos.T, sin.T


def _decay_tables(blk):
    log_g = jnp.log1p(-jnp.exp2(-5.0 - jnp.arange(H_RET, dtype=F32)))
    i = jnp.arange(blk, dtype=F32)
    diff = i[:, None] - i[None, :]
    dmask = jnp.where(diff[None] >= 0, jnp.exp(jnp.maximum(diff, 0.0)[None] * log_g[:, None, None]), 0.0)
    q_dec = jnp.exp((i + 1.0)[None, :] * log_g[:, None])
    k_dec = jnp.exp((blk - 1.0 - i)[None, :] * log_g[:, None])
    c_dec = jnp.exp(blk * log_g)
    return dmask, q_dec, k_dec, c_dec


def kernel(x_prompt, x_sample, state_ret, cache_sb_k, cache_sb_v, cache_fox_k, cache_fox_v, cache_fox_logf,
           page_table, p_prompt, p_sample, g_norm, w_in, b_f, g_ret_gn, g_fox_q, g_fox_k, w_out, w_pe, g_pe, w_pg):
    bp, tp, d = x_prompt.shape
    bs, ts, _ = x_sample.shape
    depth = w_in.shape[0]
    n_pages = page_table.shape[1]
    past_len = n_pages * PAGE_SIZE
    d_ple = p_prompt.shape[-1]
    assert tp % ATT_BLOCK == 0 and tp % RET_CHUNK == 0 and ts == SUBLANES

    cos_p, sin_p, cost_p, sint_p = _rotary_tables(jnp.arange(tp, dtype=jnp.int32))
    cos_s, sin_s, _, _ = _rotary_tables(past_len + jnp.arange(ts, dtype=jnp.int32))
    dm_p, qd_p, kd_p, cd_p = _decay_tables(RET_CHUNK)
    dec_p = (dm_p, qd_p[:, :, None], kd_p[:, None, :], cd_p[:, None, None])
    dm_s, qd_s, kd_s, cd_s = _decay_tables(ts)
    dec_s = (dm_s, qd_s[:, :, None], kd_s[:, :, None], cd_s[:, None, None])
    tq_att = 2 * ATT_BLOCK if tp % (2 * ATT_BLOCK) == 0 else ATT_BLOCK
    later = lambda n: (jnp.arange(n)[:, None] > jnp.arange(n)[None, :]).astype(BF16)
    lmat_page, lmat_att = later(PAGE_SIZE), later(tq_att)
    ar = jnp.arange(tq_att)
    umat = (ar[:, None] <= ar[None, :]).astype(BF16)
    lane_head = jnp.arange(D_FOX) // HEAD_DIM
    seg = (lane_head[:, None] == lane_head[None, :]).astype(BF16)

    def pool_t(c):
        c = jnp.transpose(c, (0, 1, 3, 4, 2))
        return c.reshape(c.shape[0], c.shape[1], c.shape[2] * c.shape[3], c.shape[4])
    pools = tuple(pool_t(c) for c in (cache_sb_k, cache_sb_v, cache_fox_k, cache_fox_v))
    lf_pool = jnp.transpose(cache_fox_logf, (0, 3, 1, 2))

    n_grp = next(g for g in (16, 8, 4, 2, 1) if n_pages % (2 * g) == 0)
    n_head = min(2, n_pages)
    n_tail = next(g for g in (8, 7, 6, 5, 4, 3, 2, 1) if (n_pages - n_head) % g == 0)
    tm_in = next(m for m in (512, 256, ATT_BLOCK) if tp % m == 0)
    n_p, n_s = bp * tp, bs * ts
    tm_out = 512 if n_p % 512 == 0 else ATT_BLOCK

    y_p = x_prompt
    y_s = x_sample.reshape(n_s, d)
    outs = {k: [] for k in ("rs_p", "rs_s", "sks", "svs", "flp", "fks", "fvs", "fls")}
    kv_stacked = None
    for i in range(depth):
        w = w_in[i]
        splits = np.cumsum([D_RET] * 4 + [D_SB] * 4 + [D_FOX] * 4).tolist()
        (w_qr, w_kr, w_vr, w_gr, w_qs, w_ks, w_vs, w_gs, w_qf, w_kf, w_vf, w_gf, w_fl) = jnp.split(w, splits, axis=1)
        w_row = jnp.concatenate([w_qr, w_vr, w_gr, w_qs, w_gs, w_qf, w_gf], axis=1).T.astype(BF16)
        w_flt = jnp.pad(w_fl.T, ((0, SUBLANES - H_FOX), (0, 0))).astype(BF16)
        w_t = [a.T.astype(BF16) for a in (w_kr, w_ks, w_vs, w_kf, w_vf)] + [w_flt]
        gn = g_norm[i][None, :]
        b_f8 = jnp.pad(b_f[i], (0, SUBLANES - H_FOX))[:, None]
        gq_row = jnp.tile(g_fox_q[i], H_FOX)[None, :]
        gk_row = jnp.tile(g_fox_k[i], H_FOX)[None, :]
        gk_col = g_fox_k[i][:, None]
        ggn = g_ret_gn[i][None, :]
        w_o = w_out[i].astype(BF16)
        w_op = (w_o[:D_RET], w_o[D_RET:D_RET + D_SB], w_o[D_RET + D_SB:],
                w_pe[i].astype(BF16), g_pe[i][None, :], w_pg[i].astype(BF16))

        (qr, vr, gr, qs, gs, qf, gf, krt, kst, vst, kft, vft, lft) = _inproj_prompt(
            y_p, gn, (cos_p, sin_p, cost_p, sint_p), b_f8, gk_col, [w_row] + w_t, tm_in, i, depth, kv_stacked)
        kv_stacked = (kst, vst, kft, vft)
        o_r, s_fin = _ret_prompt(qr, krt, vr, gr, dec_p, ggn)
        o_s = _sb_prompt(qs, kst, vst, gs, lmat_att, tq_att, i)
        o_f = _fox_prompt(qf, kft, vft, lft, gf, gq_row, seg, umat, tq_att, i)
        y_p = _outproj(y_p.reshape(n_p, d), o_r.reshape(n_p, D_RET), o_s.reshape(n_p, D_SB),
                       o_f.reshape(n_p, D_FOX), p_prompt[i].reshape(n_p, d_ple), w_op, tm_out).reshape(bp, tp, d)
        outs["rs_p"].append(s_fin)
        outs["flp"].append(lft[:, :H_FOX])

        proj, flt = _inproj_sample(y_s, gn, [w_row] + w_t)
        flt_b = jnp.transpose(flt.reshape(SUBLANES, bs, ts), (1, 0, 2))
        o_r, s_new = _ret_sample(proj, state_ret[i], (cos_s, sin_s), dec_s, ggn, ts)
        o_f, kf_new, lf_new = _decode_fox(proj, flt_b, pools[2:], lf_pool, page_table, i,
                                          b_f8, gq_row, gk_row, seg, lmat_page, ts, n_grp)
        o_s, sb_acc, sb_car = _decode_sb(proj, pools[:2], page_table, i, lmat_page, ts, n_head, n_tail)
        if n_pages > n_head:
            o_s = lax.cond(jnp.max(sb_car) > SB_DEAD,
                           lambda st: _decode_sb(proj, pools[:2], page_table, i, lmat_page, ts, n_head, n_tail, st),
                           lambda st: o_s, (sb_acc, sb_car))
        y_s = _outproj(y_s, o_r, o_s, o_f, p_sample[i].reshape(n_s, d_ple), w_op, n_s)
        outs["rs_s"].append(s_new)
        piece = lambda k, wd: proj[:, k * PIECE:k * PIECE + wd]
        outs["sks"].append(piece(5, D_SB)); outs["svs"].append(piece(6, D_SB))
        outs["fks"].append(kf_new); outs["fvs"].append(piece(10, D_FOX))
        outs["fls"].append(lf_new[:, :H_FOX])

    st = lambda k: jnp.stack(outs[k], axis=0)
    kv_p = lambda a, h: jnp.transpose(a.reshape(depth, bp, h, HEAD_DIM, tp), (0, 1, 4, 2, 3))
    kv_s = lambda k, h: st(k).reshape(depth, bs, ts, h, HEAD_DIM)
    return (y_p, y_s.reshape(bs, ts, d), st("rs_p"), st("rs_s"),
            kv_p(kv_stacked[0], H_SB), kv_p(kv_stacked[1], H_SB), kv_s("sks", H_SB), kv_s("svs", H_SB),
            kv_p(kv_stacked[2], H_FOX), kv_p(kv_stacked[3], H_FOX), jnp.transpose(st("flp"), (0, 1, 3, 2)),
            kv_s("fks", H_FOX), kv_s("fvs", H_FOX), jnp.transpose(st("fls"), (0, 1, 3, 2)))
```

```python
import functools

import jax
import jax.numpy as jnp
import numpy as np
from jax import lax
from jax.experimental import pallas as pl
from jax.experimental.pallas import tpu as pltpu

F32 = jnp.float32
BF16 = jnp.bfloat16

HEAD_DIM = 64
H_RET, H_SB, H_FOX = 6, 5, 5
D_RET, D_SB, D_FOX = H_RET * HEAD_DIM, H_SB * HEAD_DIM, H_FOX * HEAD_DIM
PAGE_SIZE = 128
RET_CHUNK = 128
ATT_BLOCK = 128
ROW_CHUNK = 32
LANES = 128
SUBLANES = 8
ROPE_BASE = 10000.0
EPS = 1e-6
QK_SCALE = HEAD_DIM ** -0.5
SB_DEAD = -104.0
NEG_BIG = -1e30
PIECE = 384
VMEM_LIMIT = 48 * 1024 * 1024


def _dot(a, b):
    return jnp.dot(a, b, preferred_element_type=F32)


def _dot_nt(a, b):
    return lax.dot_general(a, b, (((1,), (1,)), ((), ())), preferred_element_type=F32)


def _dot_tn(a, b):
    return lax.dot_general(a, b, (((0,), (0,)), ((), ())), preferred_element_type=F32)


def _split2(x):
    hi = x.astype(BF16)
    lo = (x - hi.astype(F32)).astype(BF16)
    return hi, lo


def _split3(x):
    p1 = x.astype(BF16)
    r1 = x - p1.astype(F32)
    p2 = r1.astype(BF16)
    p3 = (r1 - p2.astype(F32)).astype(BF16)
    return p1, p2, p3


def _dot_exact2(x, m):
    hi, lo = _split2(x)
    return _dot(hi, m) + _dot(lo, m)


def _dot_exact3(x, m):
    p1, p2, p3 = _split3(x)
    return _dot(p1, m) + _dot(p2, m) + _dot(p3, m)


def _softplus(z):
    return jnp.maximum(z, 0.0) + jnp.log(1.0 + jnp.exp(-jnp.abs(z)))


def _log_sigmoid(x):
    return jnp.minimum(x, 0.0) - jnp.log1p(jnp.exp(-jnp.abs(x)))


def _silu(g):
    return g * jax.nn.sigmoid(g)


def _rotary_rows(x, cos, sin_signed):
    lane = lax.broadcasted_iota(jnp.int32, (x.shape[0], LANES), 1)
    first_half = (lane % HEAD_DIM) < (HEAD_DIM // 2)
    out = []
    for c in range(x.shape[1] // LANES):
        xs = x[:, c * LANES:(c + 1) * LANES]
        partner = jnp.where(first_half,
                            pltpu.roll(xs, LANES - HEAD_DIM // 2, axis=1),
                            pltpu.roll(xs, HEAD_DIM // 2, axis=1))
        out.append(xs * cos + partner * sin_signed)
    return jnp.concatenate(out, axis=1)


def _inproj_prompt_kernel(x_ref, g_ref, cos_ref, sin_ref, cost_ref, sint_ref, bf_ref, gk_ref,
                          wrow_ref, wkr_ref, wks_ref, wvs_ref, wkf_ref, wvf_ref, wfl_ref, *rest, first_of):
    (qr_ref, vr_ref, gr_ref, qs_ref, gs_ref, qf_ref, gf_ref,
     krt_ref, kst_ref, vst_ref, kft_ref, vft_ref, lft_ref) = rest[-13:]
    if first_of is not None:
        for ref in (kst_ref, vst_ref, kft_ref, vft_ref):
            if first_of > 1:
                ref[1:] = jnp.zeros((first_of - 1,) + ref.shape[1:], F32)
        kst_ref, vst_ref, kft_ref, vft_ref = (r.at[0] for r in (kst_ref, vst_ref, kft_ref, vft_ref))
    x = x_ref[...]
    ms = jnp.mean(x * x, axis=-1, keepdims=True)
    h = (x * lax.rsqrt(ms + EPS) * g_ref[...]).astype(BF16)

    row = _dot_nt(h, wrow_ref[...])
    qr_ref[...] = _rotary_rows(row[:, 0:D_RET], cos_ref[...], sin_ref[...])
    vr_ref[...] = row[:, D_RET:2 * D_RET]
    gr_ref[...] = _silu(row[:, 2 * D_RET:3 * D_RET])
    o = 3 * D_RET
    qs_ref[...] = row[:, o:o + D_SB]
    gs_ref[...] = _silu(row[:, o + D_SB:o + 2 * D_SB])
    qf_ref[...] = row[:, o + 2 * D_SB:o + 3 * D_SB]
    gf_ref[...] = _silu(row[:, o + 3 * D_SB:o + 4 * D_SB])

    half = HEAD_DIM // 2
    krt = _dot_nt(wkr_ref[...], h)
    cost, sint = cost_ref[...], sint_ref[...]
    for hd in range(H_RET):
        x1 = krt[hd * HEAD_DIM:hd * HEAD_DIM + half]
        x2 = krt[hd * HEAD_DIM + half:(hd + 1) * HEAD_DIM]
        krt_ref[hd * HEAD_DIM:hd * HEAD_DIM + half, :] = (x1 * cost - x2 * sint) * QK_SCALE
        krt_ref[hd * HEAD_DIM + half:(hd + 1) * HEAD_DIM, :] = (x1 * sint + x2 * cost) * QK_SCALE

    kst_ref[...] = _dot_nt(wks_ref[...], h)
    vst_ref[...] = _dot_nt(wvs_ref[...], h)
    vft_ref[...] = _dot_nt(wvf_ref[...], h)
    kft = _dot_nt(wkf_ref[...], h)
    gk = gk_ref[...]
    for hd in range(H_FOX):
        xh = kft[hd * HEAD_DIM:(hd + 1) * HEAD_DIM]
        msh = jnp.mean(xh * xh, axis=0, keepdims=True)
        kft_ref[hd * HEAD_DIM:(hd + 1) * HEAD_DIM, :] = xh * lax.rsqrt(msh + EPS) * gk
    lft_ref[...] = _log_sigmoid(_dot_nt(wfl_ref[...], h) + bf_ref[...])


def _inproj_prompt(x, g_norm, tabs, b_f8, gk_col, w, tm, layer, depth, stacked):
    bsz, t, d = x.shape
    n_row = 3 * D_RET + 4 * D_SB
    grid = (bsz, t // tm)
    row_spec = lambda n: pl.BlockSpec((None, tm, n), lambda b, i: (b, i, 0))
    col_spec = lambda n: pl.BlockSpec((None, n, tm), lambda b, i: (b, 0, i))
    full = lambda a: pl.BlockSpec(a.shape, lambda b, i: (0,) * a.ndim)
    cos, sin, cost, sint = tabs
    in_specs = [
        row_spec(d), full(g_norm),
        pl.BlockSpec((tm, LANES), lambda b, i: (i, 0)), pl.BlockSpec((tm, LANES), lambda b, i: (i, 0)),
        pl.BlockSpec((HEAD_DIM // 2, tm), lambda b, i: (0, i)), pl.BlockSpec((HEAD_DIM // 2, tm), lambda b, i: (0, i)),
        full(b_f8), full(gk_col),
    ] + [full(a) for a in w]
    rs = lambda n: jax.ShapeDtypeStruct((bsz, t, n), F32)
    cs = lambda n: jax.ShapeDtypeStruct((bsz, n, t), F32)
    stk = jax.ShapeDtypeStruct((depth, bsz, D_SB, t), F32)
    aliases, extra = {}, []
    if stacked is None:
        stk_spec = pl.BlockSpec((depth, None, D_SB, tm), lambda b, i: (0, b, 0, i))
    else:
        stk_spec = pl.BlockSpec((None, None, D_SB, tm), lambda b, i: (layer, b, 0, i))
        extra = list(stacked)
        in_specs += [pl.BlockSpec(memory_space=pl.ANY)] * len(extra)
        first_in, first_out = len(in_specs) - len(extra), 8
        aliases = {first_in + k: first_out + k for k in range(len(extra))}
    out_shape = [rs(D_RET), rs(D_RET), rs(D_RET), rs(D_SB), rs(D_SB), rs(D_FOX), rs(D_FOX),
                 cs(D_RET), stk, stk, stk, stk, cs(SUBLANES)]
    out_specs = [row_spec(D_RET)] * 3 + [row_spec(D_SB)] * 4 + \
                [col_spec(D_RET)] + [stk_spec] * 4 + [col_spec(SUBLANES)]
    return pl.pallas_call(
        functools.partial(_inproj_prompt_kernel, first_of=depth if stacked is None else None),
        grid=grid, in_specs=in_specs, out_specs=out_specs, out_shape=out_shape,
        input_output_aliases=aliases,
        compiler_params=pltpu.CompilerParams(dimension_semantics=("parallel", "parallel"),
                                             vmem_limit_bytes=VMEM_LIMIT),
        name="inproj_prompt",
    )(x, g_norm, cos, sin, cost, sint, b_f8, gk_col, *w, *extra)


def _ret_prompt_kernel(qr_ref, krt_ref, vr_ref, gr_ref, dmask_ref, qdec_ref, kdec_ref, cdec_ref, ggn_ref,
                       o_ref, sfin_ref, s_scr):
    c = pl.program_id(1)

    @pl.when(c == 0)
    def _():
        s_scr[...] = jnp.zeros_like(s_scr)

    for sub in range(qr_ref.shape[0] // RET_CHUNK):
        rows = slice(sub * RET_CHUNK, (sub + 1) * RET_CHUNK)
        for hd in range(H_RET):
            sl = slice(hd * HEAD_DIM, (hd + 1) * HEAD_DIM)
            q = qr_ref[rows, sl].astype(BF16)
            kt = krt_ref[sl, rows]
            v = vr_ref[rows, sl].astype(BF16)
            sc = _dot(q, kt.astype(BF16)) * dmask_ref[hd]
            inner = _dot(sc.astype(BF16), v)
            s_old = s_scr[hd]
            cross = _dot(q, s_old.astype(BF16)) * qdec_ref[hd]
            o = inner + cross
            s_scr[hd] = s_old * cdec_ref[hd] + _dot((kt * kdec_ref[hd]).astype(BF16), v)
            mu = jnp.mean(o, axis=-1, keepdims=True)
            dev = o - mu
            var = jnp.mean(dev * dev, axis=-1, keepdims=True)
            o_ref[rows, sl] = dev * lax.rsqrt(var + EPS) * ggn_ref[:, sl] * gr_ref[rows, sl]

    @pl.when(c == pl.num_programs(1) - 1)
    def _():
        sfin_ref[...] = s_scr[...]


def _ret_prompt(qr, krt, vr, gr, dec, ggn):
    bsz, t, _ = qr.shape
    blk = RET_CHUNK
    dmask, qdec, kdec, cdec = dec
    row = pl.BlockSpec((None, blk, D_RET), lambda b, c: (b, c, 0))
    full = lambda a: pl.BlockSpec(a.shape, lambda b, c: (0,) * a.ndim)
    return pl.pallas_call(
        _ret_prompt_kernel, grid=(bsz, t // blk),
        in_specs=[row, pl.BlockSpec((None, D_RET, blk), lambda b, c: (b, 0, c)), row, row,
                  full(dmask), full(qdec), full(kdec), full(cdec), full(ggn)],
        out_specs=[row, pl.BlockSpec((None, H_RET, HEAD_DIM, HEAD_DIM), lambda b, c: (b, 0, 0, 0))],
        out_shape=[jax.ShapeDtypeStruct((bsz, t, D_RET), F32),
                   jax.ShapeDtypeStruct((bsz, H_RET, HEAD_DIM, HEAD_DIM), F32)],
        scratch_shapes=[pltpu.VMEM((H_RET, HEAD_DIM, HEAD_DIM), F32)],
        compiler_params=pltpu.CompilerParams(dimension_semantics=("parallel", "arbitrary"),
                                             vmem_limit_bytes=VMEM_LIMIT),
        name="ret_prompt",
    )(qr, krt, vr, gr, dmask, qdec, kdec, cdec, ggn)


def _sb_prompt_kernel(q_ref, kt_ref, vt_ref, g_ref, lmat_ref, o_ref,
                      q_scr, car_scr, acc_scr, z_scr, ls_scr, hi_scr, lo_scr, a_scr, tot_scr):
    i = pl.program_id(1)
    tq = q_ref.shape[0]
    reps = tq // LANES
    for hd in range(H_SB):
        sl = slice(hd * HEAD_DIM, (hd + 1) * HEAD_DIM)
        q_scr[hd] = (q_ref[:, sl] * QK_SCALE).astype(BF16)
    car_scr[...] = jnp.zeros_like(car_scr)
    acc_scr[...] = jnp.zeros_like(acc_scr)

    def block(j, diag):
        off = pl.multiple_of(j * tq, tq)
        lmat = lmat_ref[...]
        for hd in range(H_SB):
            sl = slice(hd * HEAD_DIM, (hd + 1) * HEAD_DIM)
            z_scr[hd] = _dot(q_scr[hd], kt_ref[sl, pl.ds(off, tq)].astype(BF16))
        chunk_masks = []
        for r in range(0, tq, ROW_CHUNK):
            if diag:
                rowi = lax.broadcasted_iota(jnp.int32, (ROW_CHUNK, tq), 0) + r
                coli = lax.broadcasted_iota(jnp.int32, (ROW_CHUNK, tq), 1)
                chunk_masks.append(coli < rowi)
            else:
                chunk_masks.append(None)
        for hd in range(H_SB):
            for ci, r in enumerate(range(0, tq, ROW_CHUNK)):
                rows = slice(r, r + ROW_CHUNK)
                z = z_scr[hd, rows, :]
                sp = _softplus(z)
                ls_scr[hd, rows, :] = z - sp
                if diag:
                    sp = jnp.where(chunk_masks[ci], sp, 0.0)
                hi, lo = _split2(sp)
                hi_scr[hd, rows, :] = hi
                lo_scr[hd, rows, :] = lo
                tot = jnp.sum(sp, axis=-1, keepdims=True)
                tot_scr[hd, rows, :] = jnp.broadcast_to(tot, (ROW_CHUNK, LANES))
        for hd in range(H_SB):
            z_scr[hd] = _dot(hi_scr[hd], lmat) + _dot(lo_scr[hd], lmat)
        for hd in range(H_SB):
            for ci, r in enumerate(range(0, tq, ROW_CHUNK)):
                rows = slice(r, r + ROW_CHUNK)
                carry = car_scr[hd, rows, :]
                a = jnp.exp(ls_scr[hd, rows, :] - z_scr[hd, rows, :] + jnp.tile(carry, (1, reps)))
                if diag:
                    a = jnp.where(chunk_masks[ci], a, 0.0)
                a_scr[hd, rows, :] = a.astype(BF16)
                car_scr[hd, rows, :] = carry - tot_scr[hd, rows, :]
        for hd in range(H_SB):
            sl = slice(hd * HEAD_DIM, (hd + 1) * HEAD_DIM)
            acc_scr[hd] += _dot_nt(a_scr[hd], vt_ref[sl, pl.ds(off, tq)].astype(BF16))
        return jnp.max(car_scr[...])

    cmax = block(i, True)

    def cond(st):
        j, cmax = st
        return jnp.logical_and(j >= 0, cmax > SB_DEAD)

    def body(st):
        j, _ = st
        return j - 1, block(j, False)

    lax.while_loop(cond, body, (i - 1, cmax))
    for hd in range(H_SB):
        sl = slice(hd * HEAD_DIM, (hd + 1) * HEAD_DIM)
        o_ref[:, sl] = acc_scr[hd] * g_ref[:, sl]


def _sb_prompt(qs, kst, vst, gs, lmat, tq, layer):
    bsz, t, _ = qs.shape
    row = pl.BlockSpec((None, tq, D_SB), lambda b, i: (b, i, 0))
    seq = pl.BlockSpec((None, None, D_SB, t), lambda b, i: (layer, b, 0, 0))
    return pl.pallas_call(
        _sb_prompt_kernel, grid=(bsz, t // tq),
        in_specs=[row, seq, seq, row, pl.BlockSpec(lmat.shape, lambda b, i: (0, 0))],
        out_specs=row, out_shape=jax.ShapeDtypeStruct((bsz, t, D_SB), F32),
        scratch_shapes=[pltpu.VMEM((H_SB, tq, HEAD_DIM), BF16), pltpu.VMEM((H_SB, tq, LANES), F32),
                        pltpu.VMEM((H_SB, tq, HEAD_DIM), F32),
                        pltpu.VMEM((H_SB, tq, tq), F32), pltpu.VMEM((H_SB, tq, tq), F32),
                        pltpu.VMEM((H_SB, tq, tq), BF16), pltpu.VMEM((H_SB, tq, tq), BF16),
                        pltpu.VMEM((H_SB, tq, tq), BF16), pltpu.VMEM((H_SB, tq, LANES), F32)],
        compiler_params=pltpu.CompilerParams(dimension_semantics=("parallel", "arbitrary"),
                                             vmem_limit_bytes=VMEM_LIMIT),
        name="sb_prompt",
    )(qs, kst, vst, gs, lmat)


def _fox_prompt_kernel(q_ref, kt_ref, vt_ref, lft_ref, g_ref, gq_ref, seg_ref, umat_ref, o_ref,
                       c_scr, q_scr, m_scr, acc_scr, s_scr, p_scr, a_scr):
    i = pl.program_id(1)
    tq = q_ref.shape[0]
    t = kt_ref.shape[1]
    cw = umat_ref.shape[0]

    @pl.when(i == 0)
    def _():
        umat = umat_ref[...]
        carry = jnp.zeros((SUBLANES, 1), F32)
        for c in range(t // cw):
            cs = _dot_exact3(lft_ref[:, c * cw:(c + 1) * cw], umat) + carry
            c_scr[:, c * cw:(c + 1) * cw] = cs
            carry = cs[:, cw - 1:cw]

    qraw = q_ref[...]
    ssq = _dot_exact2(qraw * qraw, seg_ref[...])
    qn = qraw * lax.rsqrt(ssq * (1.0 / HEAD_DIM) + EPS) * gq_ref[...] * QK_SCALE
    for hd in range(H_FOX):
        q_scr[hd] = qn[:, hd * HEAD_DIM:(hd + 1) * HEAD_DIM].astype(BF16)
    m_scr[...] = jnp.full_like(m_scr, NEG_BIG)
    acc_scr[...] = jnp.zeros_like(acc_scr)
    rowi = lax.broadcasted_iota(jnp.int32, (tq, tq), 0)
    coli = lax.broadcasted_iota(jnp.int32, (tq, tq), 1)
    diag_mask = coli <= rowi

    def block(off, width, mask):
        ones_rows = jnp.ones((HEAD_DIM, width), BF16)
        for hd in range(H_FOX):
            sl = slice(hd * HEAD_DIM, (hd + 1) * HEAD_DIM)
            kt = kt_ref[sl, pl.ds(off, width)].astype(BF16)
            s = _dot(q_scr[hd], kt) - c_scr[hd:hd + 1, pl.ds(off, width)]
            if mask is not None:
                s = jnp.where(mask, s, NEG_BIG)
            s_scr[hd, :, :width] = s
        for hd in range(H_FOX):
            for r in range(0, tq, ROW_CHUNK):
                rows = slice(r, r + ROW_CHUNK)
                s = s_scr[hd, rows, :width]
                m_old = m_scr[hd, rows, :]
                mx = jnp.max(s, axis=-1, keepdims=True)
                m_new = jnp.maximum(m_old, jnp.broadcast_to(mx, (ROW_CHUNK, LANES)))
                p_scr[hd, rows, :width] = jnp.exp(s - jnp.tile(m_new, (1, width // LANES))).astype(BF16)
                a_scr[hd, rows, :] = jnp.exp(m_old - m_new)
                m_scr[hd, rows, :] = m_new
        for hd in range(H_FOX):
            sl = slice(hd * HEAD_DIM, (hd + 1) * HEAD_DIM)
            vt = jnp.concatenate([vt_ref[sl, pl.ds(off, width)].astype(BF16), ones_rows], axis=0)
            acc_scr[hd] = a_scr[hd] * acc_scr[hd] + _dot_nt(p_scr[hd, :, :width], vt)

    def body(j, carry):
        block(pl.multiple_of(j * 2 * tq, 2 * tq), 2 * tq, None)
        return carry

    lax.fori_loop(0, i // 2, body, 0)

    @pl.when(i % 2 == 1)
    def _():
        block(pl.multiple_of((i - 1) * tq, tq), tq, None)

    block(pl.multiple_of(i * tq, tq), tq, diag_mask)
    for hd in range(H_FOX):
        sl = slice(hd * HEAD_DIM, (hd + 1) * HEAD_DIM)
        acc = acc_scr[hd]
        o_ref[:, sl] = acc[:, :HEAD_DIM] / acc[:, HEAD_DIM:HEAD_DIM + 1] * g_ref[:, sl]


def _fox_prompt(qf, kft, vft, lft, gf, gq_row, seg, umat, tq, layer):
    bsz, t, _ = qf.shape
    row = pl.BlockSpec((None, tq, D_FOX), lambda b, i: (b, i, 0))
    seq = pl.BlockSpec((None, None, D_FOX, t), lambda b, i: (layer, b, 0, 0))
    full = lambda a: pl.BlockSpec(a.shape, lambda b, i: (0,) * a.ndim)
    return pl.pallas_call(
        _fox_prompt_kernel, grid=(bsz, t // tq),
        in_specs=[row, seq, seq, pl.BlockSpec((None, SUBLANES, t), lambda b, i: (b, 0, 0)), row,
                  full(gq_row), full(seg), full(umat)],
        out_specs=row, out_shape=jax.ShapeDtypeStruct((bsz, t, D_FOX), F32),
        scratch_shapes=[pltpu.VMEM((SUBLANES, t), F32), pltpu.VMEM((H_FOX, tq, HEAD_DIM), BF16),
                        pltpu.VMEM((H_FOX, tq, LANES), F32), pltpu.VMEM((H_FOX, tq, 2 * HEAD_DIM), F32),
                        pltpu.VMEM((H_FOX, tq, 2 * tq), F32), pltpu.VMEM((H_FOX, tq, 2 * tq), BF16),
                        pltpu.VMEM((H_FOX, tq, LANES), F32)],
        compiler_params=pltpu.CompilerParams(dimension_semantics=("parallel", "arbitrary"),
                                             vmem_limit_bytes=VMEM_LIMIT),
        name="fox_prompt",
    )(qf, kft, vft, lft, gf, gq_row, seg, umat)


def _outproj_kernel(y_ref, or_ref, os_ref, of_ref, p_ref, wor_ref, wos_ref, wof_ref, wpe_ref, gpe_ref, wpg_ref,
                    o_ref):
    m = (_dot(or_ref[...].astype(BF16), wor_ref[...]) + _dot(os_ref[...].astype(BF16), wos_ref[...])
         + _dot(of_ref[...].astype(BF16), wof_ref[...]))
    y1 = y_ref[...] + m
    ms = jnp.mean(y1 * y1, axis=-1, keepdims=True)
    n = (y1 * lax.rsqrt(ms + EPS) * gpe_ref[...]).astype(BF16)
    gate = jax.nn.sigmoid(_dot(n, wpg_ref[...]))
    pe = _dot(p_ref[...].astype(BF16), wpe_ref[...])
    o_ref[...] = y1 + pe * gate


def _outproj(y, o_r, o_s, o_f, p, w, tm):
    n, d = y.shape
    row = lambda a: pl.BlockSpec((tm, a.shape[1]), lambda i: (i, 0))
    full = lambda a: pl.BlockSpec(a.shape, lambda i: (0,) * a.ndim)
    return pl.pallas_call(
        _outproj_kernel, grid=(n // tm,),
        in_specs=[row(y), row(o_r), row(o_s), row(o_f), row(p)] + [full(a) for a in w],
        out_specs=row(y), out_shape=jax.ShapeDtypeStruct((n, d), F32),
        compiler_params=pltpu.CompilerParams(dimension_semantics=("parallel",), vmem_limit_bytes=VMEM_LIMIT),
        name="outproj",
    )(y, o_r, o_s, o_f, p, *w)


def _inproj_sample_kernel(x_ref, g_ref, wrow_ref, wkr_ref, wks_ref, wvs_ref, wkf_ref, wvf_ref, wfl_ref,
                          o_ref, flt_ref):
    x = x_ref[...]
    ms = jnp.mean(x * x, axis=-1, keepdims=True)
    h = (x * lax.rsqrt(ms + EPS) * g_ref[...]).astype(BF16)
    row = _dot_nt(h, wrow_ref[...])
    o_ref[...] = jnp.zeros_like(o_ref)

    def put(slot, val):
        o_ref[:, slot * PIECE:slot * PIECE + val.shape[1]] = val

    put(0, row[:, 0:D_RET])
    put(2, row[:, D_RET:2 * D_RET])
    put(3, _silu(row[:, 2 * D_RET:3 * D_RET]))
    o = 3 * D_RET
    put(4, row[:, o:o + D_SB])
    put(7, _silu(row[:, o + D_SB:o + 2 * D_SB]))
    put(8, row[:, o + 2 * D_SB:o + 3 * D_SB])
    put(11, _silu(row[:, o + 3 * D_SB:o + 4 * D_SB]))
    put(1, _dot_nt(h, wkr_ref[...]))
    put(5, _dot_nt(h, wks_ref[...]))
    put(6, _dot_nt(h, wvs_ref[...]))
    put(9, _dot_nt(h, wkf_ref[...]))
    put(10, _dot_nt(h, wvf_ref[...]))
    flt_ref[...] = _dot_nt(wfl_ref[...], h)


def _inproj_sample(x, g_norm, w):
    n, d = x.shape
    ncol = 12 * PIECE
    full = lambda a: pl.BlockSpec(a.shape, lambda i: (0,) * a.ndim)
    return pl.pallas_call(
        _inproj_sample_kernel, grid=(1,),
        in_specs=[full(x), full(g_norm)] + [full(a) for a in w],
        out_specs=[pl.BlockSpec((n, ncol), lambda i: (0, 0)), pl.BlockSpec((SUBLANES, n), lambda i: (0, 0))],
        out_shape=[jax.ShapeDtypeStruct((n, ncol), F32), jax.ShapeDtypeStruct((SUBLANES, n), F32)],
        compiler_params=pltpu.CompilerParams(dimension_semantics=("arbitrary",), vmem_limit_bytes=VMEM_LIMIT),
        name="inproj_sample",
    )(x, g_norm, *w)


def _ret_sample_kernel(q_ref, k_ref, v_ref, g_ref, s0_ref, cos_ref, sin_ref, dmask_ref, qdec_ref, kdec_ref,
                       cdec_ref, ggn_ref, o_ref, s_ref, *, ts):
    n_seq = s0_ref.shape[0]
    cos = jnp.concatenate([cos_ref[...]] * n_seq, axis=0)
    sin = jnp.concatenate([sin_ref[...]] * n_seq, axis=0)
    qr = _rotary_rows(q_ref[...], cos, sin)
    kr = _rotary_rows(k_ref[...], cos, sin) * QK_SCALE
    for sq in range(n_seq):
        rows = slice(sq * ts, (sq + 1) * ts)
        for hd in range(H_RET):
            sl = slice(hd * HEAD_DIM, (hd + 1) * HEAD_DIM)
            q = qr[rows, sl].astype(BF16)
            k = kr[rows, sl]
            v = v_ref[rows, sl].astype(BF16)
            sc = _dot_nt(q, k.astype(BF16)) * dmask_ref[hd]
            inner = _dot(sc.astype(BF16), v)
            s_old = s0_ref[sq, hd]
            cross = _dot(q, s_old.astype(BF16)) * qdec_ref[hd]
            o = inner + cross
            s_ref[sq, hd] = s_old * cdec_ref[hd] + _dot_tn((k * kdec_ref[hd]).astype(BF16), v)
            mu = jnp.mean(o, axis=-1, keepdims=True)
            dev = o - mu
            var = jnp.mean(dev * dev, axis=-1, keepdims=True)
            y = dev * lax.rsqrt(var + EPS) * ggn_ref[:, sl]
            o_ref[rows, sl] = y * g_ref[rows, sl]


def _ret_sample(proj, s0, tabs, dec, ggn, ts):
    n = proj.shape[0]
    bsz = n // ts
    per = next(m for m in (4, 2, 1) if bsz % m == 0)
    cos, sin = tabs
    dmask, qdec, kdec, cdec = dec
    piece = lambda k: pl.BlockSpec((per * ts, PIECE), lambda b, k=k: (b, k))
    full = lambda a: pl.BlockSpec(a.shape, lambda b: (0,) * a.ndim)
    st = pl.BlockSpec((per, H_RET, HEAD_DIM, HEAD_DIM), lambda b: (b, 0, 0, 0))
    return pl.pallas_call(
        functools.partial(_ret_sample_kernel, ts=ts), grid=(bsz // per,),
        in_specs=[piece(0), piece(1), piece(2), piece(3), st, full(cos), full(sin),
                  full(dmask), full(qdec), full(kdec), full(cdec), full(ggn)],
        out_specs=[pl.BlockSpec((per * ts, D_RET), lambda b: (b, 0)), st],
        out_shape=[jax.ShapeDtypeStruct((n, D_RET), F32),
                   jax.ShapeDtypeStruct((bsz, H_RET, HEAD_DIM, HEAD_DIM), F32)],
        compiler_params=pltpu.CompilerParams(dimension_semantics=("parallel",), vmem_limit_bytes=VMEM_LIMIT),
        name="ret_sample",
    )(proj, proj, proj, proj, s0, cos, sin, dmask, qdec, kdec, cdec, ggn)


def _head_rows(x, ts):
    rows = H_SB * ts
    r = lax.broadcasted_iota(jnp.int32, (rows, D_SB), 0)
    c = lax.broadcasted_iota(jnp.int32, (rows, D_SB), 1)
    tiled = jnp.concatenate([x] * H_SB, axis=0)
    return jnp.where(r // ts == c // HEAD_DIM, tiled, 0.0)


def _head_cols(x, ts):
    c = lax.broadcasted_iota(jnp.int32, (ts, D_SB), 1)
    out = jnp.zeros((ts, D_SB), F32)
    for hd in range(H_SB):
        out = out + jnp.where(c // HEAD_DIM == hd, x[hd * ts:(hd + 1) * ts], 0.0)
    return out


def _new_token_masks(ts, strict):
    rows = H_SB * ts
    tpos = lax.broadcasted_iota(jnp.int32, (rows, PAGE_SIZE), 0) % ts
    coli = lax.broadcasted_iota(jnp.int32, (rows, PAGE_SIZE), 1)
    return coli < tpos if strict else coli <= tpos


def _sb_blocks(zs, v_dots, lmat, acc, car):
    for (z, mask), v_dot in zip(zs, v_dots):
        sp = _softplus(z)
        ls = z - sp
        if mask is not None:
            sp = jnp.where(mask, sp, 0.0)
        a = jnp.exp(ls - _dot_exact2(sp, lmat) + car)
        if mask is not None:
            a = jnp.where(mask, a, 0.0)
        acc = acc + v_dot(a.astype(BF16))
        car = car - jnp.sum(sp, axis=-1, keepdims=True)
    return acc, car


def _decode_sb_kernel(pt_ref, qs_ref, ks_ref, vs_ref, gs_ref, lmat_ref, *rest, n_grp, ts, head):
    if head:
        pages = rest[:2 * n_grp]
        os_ref, acco_ref, caro_ref = rest[2 * n_grp:]
    else:
        acci_ref, cari_ref = rest[:2]
        pages = rest[2:2 + 2 * n_grp]
        os_ref, qbs_scr, acc_scr, car_scr = rest[2 + 2 * n_grp:]
    lmat = lmat_ref[...]
    order = list(range(n_grp - 1, -1, -1))

    def page_blocks(qbs):
        zs, dots = [], []
        for gi in order:
            kt_ref, vt_ref = pages[2 * gi:2 * gi + 2]
            zs.append((_dot(qbs, kt_ref[...].astype(BF16)), None))
            dots.append(lambda a, r=vt_ref: _dot_nt(a, r[...].astype(BF16)))
        return zs, dots

    qbs_new = _head_rows(qs_ref[:, :D_SB] * QK_SCALE, ts).astype(BF16)
    if head:
        pad = jnp.zeros((PAGE_SIZE - ts, D_SB), F32)
        k_new = jnp.concatenate([ks_ref[:, :D_SB], pad], axis=0).astype(BF16)
        v_new = jnp.concatenate([vs_ref[:, :D_SB], pad], axis=0).astype(BF16)
        zs, dots = page_blocks(qbs_new)
        acc, car = _sb_blocks([(_dot_nt(qbs_new, k_new), _new_token_masks(ts, True))] + zs,
                              [lambda a: _dot(a, v_new)] + dots, lmat,
                              jnp.zeros((H_SB * ts, D_SB), F32), jnp.zeros((H_SB * ts, 1), F32))
        acco_ref[...] = acc
        caro_ref[...] = jnp.broadcast_to(car, caro_ref.shape)
        os_ref[...] = _head_cols(acc, ts) * gs_ref[:, :D_SB]
    else:
        s = pl.program_id(1)

        @pl.when(s == 0)
        def _():
            qbs_scr[...] = qbs_new
            acc_scr[...] = acci_ref[...]
            car_scr[...] = cari_ref[:, 0:1]

        @pl.when(jnp.max(car_scr[...]) > SB_DEAD)
        def _():
            zs, dots = page_blocks(qbs_scr[...])
            acc, car = _sb_blocks(zs, dots, lmat, acc_scr[...], car_scr[...])
            acc_scr[...] = acc
            car_scr[...] = car

        @pl.when(s == pl.num_programs(1) - 1)
        def _():
            os_ref[...] = _head_cols(acc_scr[...], ts) * gs_ref[:, :D_SB]


def _decode_sb(proj, pools, page_table, layer, lmat, ts, n_head, n_grp, state=None):
    n = proj.shape[0]
    bsz, n_pages = page_table.shape
    rows = H_SB * ts
    head = state is None
    per_step = n_head if head else n_grp
    n_steps = 1 if head else (n_pages - n_head) // n_grp
    last = n_pages - 1 if head else n_pages - n_head - 1
    piece = lambda k: pl.BlockSpec((ts, PIECE), lambda b, s, pt, k=k: (b, k))

    def page_spec(gi):
        return pl.BlockSpec((None, None, D_SB, PAGE_SIZE),
                            lambda b, s, pt, gi=gi: (layer, pt[b, last - s * per_step - (per_step - 1 - gi)], 0, 0))

    page_specs = [page_spec(gi) for gi in range(per_step) for _ in range(2)]
    page_args = [p for _ in range(per_step) for p in pools]
    out_row = pl.BlockSpec((ts, D_SB), lambda b, s, pt: (b, 0))
    st_acc = pl.BlockSpec((rows, D_SB), lambda b, s, pt: (b, 0))
    st_car = pl.BlockSpec((rows, LANES), lambda b, s, pt: (b, 0))
    in_specs = [piece(4), piece(5), piece(6), piece(7), pl.BlockSpec(lmat.shape, lambda b, s, pt: (0, 0))]
    args = [proj] * 4 + [lmat]
    if head:
        out_specs = [out_row, st_acc, st_car]
        out_shape = [jax.ShapeDtypeStruct((n, D_SB), F32), jax.ShapeDtypeStruct((bsz * rows, D_SB), F32),
                     jax.ShapeDtypeStruct((bsz * rows, LANES), F32)]
        scratch = []
    else:
        in_specs += [st_acc, st_car]
        args += list(state)
        out_specs = out_row
        out_shape = jax.ShapeDtypeStruct((n, D_SB), F32)
        scratch = [pltpu.VMEM((rows, D_SB), BF16), pltpu.VMEM((rows, D_SB), F32), pltpu.VMEM((rows, 1), F32)]
    grid_spec = pltpu.PrefetchScalarGridSpec(
        num_scalar_prefetch=1, grid=(bsz, n_steps), in_specs=in_specs + page_specs, out_specs=out_specs,
        scratch_shapes=scratch)
    return pl.pallas_call(
        functools.partial(_decode_sb_kernel, n_grp=per_step, ts=ts, head=head), grid_spec=grid_spec,
        out_shape=out_shape,
        compiler_params=pltpu.CompilerParams(dimension_semantics=("parallel", "arbitrary"),
                                             vmem_limit_bytes=VMEM_LIMIT),
        name="decode_sb_head" if head else "decode_sb_tail",
    )(page_table, *args, *page_args)


def _decode_fox_kernel(pt_ref, qf_ref, kf_ref, vf_ref, gf_ref, flt_ref, bf_ref, gq_ref, gk_ref, seg_ref, lmat_ref,
                       kpool_ref, vpool_ref, lfpool_ref, of_ref, kfn_ref, lfn_ref,
                       kbuf, vbuf, lfall, sem, lfsem, *, n_grp, ts, layer):
    b = pl.program_id(0)
    n_seq = pl.num_programs(0)
    n_pages = pt_ref.shape[1]
    n_groups = n_pages // n_grp
    rows = H_FOX * ts
    lmat = lmat_ref[...]

    def group_copies(seq, g, slot):
        out = []
        for gi in range(n_grp):
            page = pt_ref[seq, n_pages - 1 - g * n_grp - gi]
            out.append(pltpu.make_async_copy(kpool_ref.at[layer, page], kbuf.at[slot, gi], sem.at[slot, 0]))
            out.append(pltpu.make_async_copy(vpool_ref.at[layer, page], vbuf.at[slot, gi], sem.at[slot, 1]))
        return out

    lf_copy = pltpu.make_async_copy(lfpool_ref.at[layer], lfall, lfsem.at[0])

    @pl.when(b == 0)
    def _():
        lf_copy.start()
        for cp in group_copies(b, 0, 0):
            cp.start()

    def rep_heads(x):
        return jnp.concatenate([jnp.broadcast_to(x[hd:hd + 1], (ts, x.shape[1])) for hd in range(H_FOX)], axis=0)

    def fox_block(sc, mask, v_dot, m_old, l, acc):
        if mask is not None:
            sc = jnp.where(mask, sc, NEG_BIG)
        m_new = jnp.maximum(m_old, jnp.max(sc, axis=-1, keepdims=True))
        alpha = jnp.exp(m_old - m_new)
        p = jnp.exp(sc - m_new)
        l = alpha * l + jnp.sum(p, axis=-1, keepdims=True)
        acc = alpha * acc + v_dot(p.astype(BF16))
        return m_new, l, acc

    pad = jnp.zeros((PAGE_SIZE - ts, D_FOX), F32)
    seg = seg_ref[...]
    qraw = qf_ref[:, :D_FOX]
    qn = qraw * lax.rsqrt(_dot_exact2(qraw * qraw, seg) * (1.0 / HEAD_DIM) + EPS) * gq_ref[...] * QK_SCALE
    qbf = _head_rows(qn, ts).astype(BF16)
    kraw = kf_ref[:, :D_FOX]
    kn = kraw * lax.rsqrt(_dot_exact2(kraw * kraw, seg) * (1.0 / HEAD_DIM) + EPS) * gk_ref[...]
    kfn_ref[...] = kn
    lf_new = _log_sigmoid(flt_ref[...] + bf_ref[...])
    lfn_ref[...] = lf_new
    lf_pad = jnp.concatenate([lf_new, jnp.zeros((SUBLANES, PAGE_SIZE - ts), F32)], axis=1)
    lf_rows = rep_heads(lf_pad)
    cn = jnp.sum(lf_rows, axis=-1, keepdims=True) - _dot_exact3(lf_rows, lmat)
    kn_pad = jnp.concatenate([kn, pad], axis=0).astype(BF16)
    vf_new = jnp.concatenate([vf_ref[:, :D_FOX], pad], axis=0).astype(BF16)
    m, l, acc = fox_block(_dot_nt(qbf, kn_pad) - cn, _new_token_masks(ts, False), lambda p: _dot(p, vf_new),
                          jnp.full((rows, 1), NEG_BIG, F32), jnp.zeros((rows, 1), F32),
                          jnp.zeros((rows, D_FOX), F32))

    @pl.when(b == 0)
    def _():
        lf_copy.wait()

    rcar = jnp.zeros((rows, 1), F32)
    for g in range(n_groups):
        slot = g % 2
        if g + 1 < n_groups:
            for cp in group_copies(b, g + 1, 1 - slot):
                cp.start()
        else:
            @pl.when(b + 1 < n_seq)
            def _():
                for cp in group_copies(b + 1, 0, 1 - slot):
                    cp.start()
        for cp in group_copies(b, g, slot):
            cp.wait()
        lf_rows = []
        for gi in range(n_grp):
            page = pt_ref[b, n_pages - 1 - g * n_grp - gi]
            lf_pg = lfall[:, pl.ds(page, 1), :]
            lf_rows.append(jnp.concatenate([jnp.broadcast_to(lf_pg[hd], (ts, PAGE_SIZE)) for hd in range(H_FOX)],
                                           axis=0))
        later = _dot_exact3(jnp.concatenate(lf_rows, axis=0), lmat)
        biases = []
        for k, lr in enumerate(lf_rows):
            biases.append(later[k * rows:(k + 1) * rows] + rcar)
            rcar = rcar + jnp.sum(lr, axis=-1, keepdims=True)
        kt_all = jnp.concatenate([kbuf[slot, gi].astype(BF16) for gi in range(n_grp)], axis=1)
        vt_all = jnp.concatenate([vbuf[slot, gi].astype(BF16) for gi in range(n_grp)], axis=1)
        sc = _dot(qbf, kt_all) + jnp.concatenate(biases, axis=1)
        m, l, acc = fox_block(sc, None, lambda p, vt_all=vt_all: _dot_nt(p, vt_all), m, l, acc)

    of_ref[...] = _head_cols(acc / l, ts) * gf_ref[:, :D_FOX]


def _decode_fox(proj, flt, pools, lf_pool, page_table, layer, b_f8, gq_row, gk_row, seg, lmat, ts, n_grp):
    n = proj.shape[0]
    bsz, n_pages = page_table.shape
    assert n_pages % (2 * n_grp) == 0
    piece = lambda k: pl.BlockSpec((ts, PIECE), lambda b, pt, k=k: (b, k))
    full = lambda a: pl.BlockSpec(a.shape, lambda b, pt: (0,) * a.ndim)
    hbm = pl.BlockSpec(memory_space=pl.ANY)
    out_row = pl.BlockSpec((ts, D_FOX), lambda b, pt: (b, 0))
    small = pl.BlockSpec((None, SUBLANES, ts), lambda b, pt: (b, 0, 0))
    grid_spec = pltpu.PrefetchScalarGridSpec(
        num_scalar_prefetch=1, grid=(bsz,),
        in_specs=[piece(8), piece(9), piece(10), piece(11), small,
                  full(b_f8), full(gq_row), full(gk_row), full(seg), full(lmat), hbm, hbm, hbm],
        out_specs=[out_row, out_row, small],
        scratch_shapes=[pltpu.VMEM((2, n_grp, D_FOX, PAGE_SIZE), F32), pltpu.VMEM((2, n_grp, D_FOX, PAGE_SIZE), F32),
                        pltpu.VMEM(lf_pool.shape[1:], F32), pltpu.SemaphoreType.DMA((2, 2)),
                        pltpu.SemaphoreType.DMA((1,))],
    )
    return pl.pallas_call(
        functools.partial(_decode_fox_kernel, n_grp=n_grp, ts=ts, layer=layer), grid_spec=grid_spec,
        out_shape=[jax.ShapeDtypeStruct((n, D_FOX), F32), jax.ShapeDtypeStruct((n, D_FOX), F32),
                   jax.ShapeDtypeStruct((bsz, SUBLANES, ts), F32)],
        compiler_params=pltpu.CompilerParams(dimension_semantics=("arbitrary",), vmem_limit_bytes=VMEM_LIMIT),
        name="decode_fox",
    )(page_table, *([proj] * 4), flt, b_f8, gq_row, gk_row, seg, lmat, *pools, lf_pool)


def _rotary_tables(pos):
    half = HEAD_DIM // 2
    inv = ROPE_BASE ** (-jnp.arange(half, dtype=F32) / half)
    ang = pos.astype(F32)[:, None] * inv[None, :]
    cos, sin = jnp.cos(ang), jnp.sin(ang)
    reps = LANES // HEAD_DIM
    cos_row = jnp.tile(jnp.concatenate([cos, cos], axis=1), (1, reps))
    sin_row = jnp.tile(jnp.concatenate([-sin, sin], axis=1), (1, reps))
    return cos_row, sin_row, cos.T, sin.T


def _decay_tables(blk):
    log_g = jnp.log1p(-jnp.exp2(-5.0 - jnp.arange(H_RET, dtype=F32)))
    i = jnp.arange(blk, dtype=F32)
    diff = i[:, None] - i[None, :]
    dmask = jnp.where(diff[None] >= 0, jnp.exp(jnp.maximum(diff, 0.0)[None] * log_g[:, None, None]), 0.0)
    q_dec = jnp.exp((i + 1.0)[None, :] * log_g[:, None])
    k_dec = jnp.exp((blk - 1.0 - i)[None, :] * log_g[:, None])
    c_dec = jnp.exp(blk * log_g)
    return dmask, q_dec, k_dec, c_dec


def kernel(x_prompt, x_sample, state_ret, cache_sb_k, cache_sb_v, cache_fox_k, cache_fox_v, cache_fox_logf,
           page_table, p_prompt, p_sample, g_norm, w_in, b_f, g_ret_gn, g_fox_q, g_fox_k, w_out, w_pe, g_pe, w_pg):
    bp, tp, d = x_prompt.shape
    bs, ts, _ = x_sample.shape
    depth = w_in.shape[0]
    n_pages = page_table.shape[1]
    past_len = n_pages * PAGE_SIZE
    d_ple = p_prompt.shape[-1]
    assert tp % ATT_BLOCK == 0 and tp % RET_CHUNK == 0 and ts == SUBLANES

    cos_p, sin_p, cost_p, sint_p = _rotary_tables(jnp.arange(tp, dtype=jnp.int32))
    cos_s, sin_s, _, _ = _rotary_tables(past_len + jnp.arange(ts, dtype=jnp.int32))
    dm_p, qd_p, kd_p, cd_p = _decay_tables(RET_CHUNK)
    dec_p = (dm_p, qd_p[:, :, None], kd_p[:, None, :], cd_p[:, None, None])
    dm_s, qd_s, kd_s, cd_s = _decay_tables(ts)
    dec_s = (dm_s, qd_s[:, :, None], kd_s[:, :, None], cd_s[:, None, None])
    tq_att = 2 * ATT_BLOCK if tp % (2 * ATT_BLOCK) == 0 else ATT_BLOCK
    later = lambda n: (jnp.arange(n)[:, None] > jnp.arange(n)[None, :]).astype(BF16)
    lmat_page, lmat_att = later(PAGE_SIZE), later(tq_att)
    ar = jnp.arange(tq_att)
    umat = (ar[:, None] <= ar[None, :]).astype(BF16)
    lane_head = jnp.arange(D_FOX) // HEAD_DIM
    seg = (lane_head[:, None] == lane_head[None, :]).astype(BF16)

    def pool_t(c):
        c = jnp.transpose(c, (0, 1, 3, 4, 2))
        return c.reshape(c.shape[0], c.shape[1], c.shape[2] * c.shape[3], c.shape[4])
    pools = tuple(pool_t(c) for c in (cache_sb_k, cache_sb_v, cache_fox_k, cache_fox_v))
    lf_pool = jnp.transpose(cache_fox_logf, (0, 3, 1, 2))

    n_grp = next(g for g in (16, 8, 4, 2, 1) if n_pages % (2 * g) == 0)
    n_head = min(2, n_pages)
    n_tail = next(g for g in (8, 7, 6, 5, 4, 3, 2, 1) if (n_pages - n_head) % g == 0)
    tm_in = next(m for m in (512, 256, ATT_BLOCK) if tp % m == 0)
    n_p, n_s = bp * tp, bs * ts
    tm_out = 512 if n_p % 512 == 0 else ATT_BLOCK

    y_p = x_prompt
    y_s = x_sample.reshape(n_s, d)
    outs = {k: [] for k in ("rs_p", "rs_s", "sks", "svs", "flp", "fks", "fvs", "fls")}
    kv_stacked = None
    w_in_t = jnp.transpose(w_in, (0, 2, 1)).astype(BF16)
    splits = np.cumsum([D_RET] * 4 + [D_SB] * 4 + [D_FOX] * 4).tolist()
    for i in range(depth):
        (w_qr, w_kr, w_vr, w_gr, w_qs, w_ks, w_vs, w_gs, w_qf, w_kf, w_vf, w_gf, w_fl) = jnp.split(
            w_in_t[i], splits, axis=0)
        w_row = jnp.concatenate([w_qr, w_vr, w_gr, w_qs, w_gs, w_qf, w_gf], axis=0)
        w_flt = jnp.pad(w_fl, ((0, SUBLANES - H_FOX), (0, 0)))
        w_t = [w_kr, w_ks, w_vs, w_kf, w_vf, w_flt]
        gn = g_norm[i][None, :]
        b_f8 = jnp.pad(b_f[i], (0, SUBLANES - H_FOX))[:, None]
        gq_row = jnp.tile(g_fox_q[i], H_FOX)[None, :]
        gk_row = jnp.tile(g_fox_k[i], H_FOX)[None, :]
        gk_col = g_fox_k[i][:, None]
        ggn = g_ret_gn[i][None, :]
        w_o = w_out[i].astype(BF16)
        w_op = (w_o[:D_RET], w_o[D_RET:D_RET + D_SB], w_o[D_RET + D_SB:],
                w_pe[i].astype(BF16), g_pe[i][None, :], w_pg[i].astype(BF16))

        (qr, vr, gr, qs, gs, qf, gf, krt, kst, vst, kft, vft, lft) = _inproj_prompt(
            y_p, gn, (cos_p, sin_p, cost_p, sint_p), b_f8, gk_col, [w_row] + w_t, tm_in, i, depth, kv_stacked)
        kv_stacked = (kst, vst, kft, vft)
        o_r, s_fin = _ret_prompt(qr, krt, vr, gr, dec_p, ggn)
        o_s = _sb_prompt(qs, kst, vst, gs, lmat_att, tq_att, i)
        o_f = _fox_prompt(qf, kft, vft, lft, gf, gq_row, seg, umat, tq_att, i)
        y_p = _outproj(y_p.reshape(n_p, d), o_r.reshape(n_p, D_RET), o_s.reshape(n_p, D_SB),
                       o_f.reshape(n_p, D_FOX), p_prompt[i].reshape(n_p, d_ple), w_op, tm_out).reshape(bp, tp, d)
        outs["rs_p"].append(s_fin)
        outs["flp"].append(lft[:, :H_FOX])

        proj, flt = _inproj_sample(y_s, gn, [w_row] + w_t)
        flt_b = jnp.transpose(flt.reshape(SUBLANES, bs, ts), (1, 0, 2))
        o_r, s_new = _ret_sample(proj, state_ret[i], (cos_s, sin_s), dec_s, ggn, ts)
        o_f, kf_new, lf_new = _decode_fox(proj, flt_b, pools[2:], lf_pool, page_table, i,
                                          b_f8, gq_row, gk_row, seg, lmat_page, ts, n_grp)
        o_s, sb_acc, sb_car = _decode_sb(proj, pools[:2], page_table, i, lmat_page, ts, n_head, n_tail)
        if n_pages > n_head:
            o_s = lax.cond(jnp.max(sb_car) > SB_DEAD,
                           lambda st: _decode_sb(proj, pools[:2], page_table, i, lmat_page, ts, n_head, n_tail, st),
                           lambda st: o_s, (sb_acc, sb_car))
        y_s = _outproj(y_s, o_r, o_s, o_f, p_sample[i].reshape(n_s, d_ple), w_op, n_s)
        outs["rs_s"].append(s_new)
        piece = lambda k, wd: proj[:, k * PIECE:k * PIECE + wd]
        outs["sks"].append(piece(5, D_SB)); outs["svs"].append(piece(6, D_SB))
        outs["fks"].append(kf_new); outs["fvs"].append(piece(10, D_FOX))
        outs["fls"].append(lf_new[:, :H_FOX])

    st = lambda k: jnp.stack(outs[k], axis=0)
    kv_p = lambda a, h: jnp.transpose(a.reshape(depth, bp, h, HEAD_DIM, tp), (0, 1, 4, 2, 3))
    kv_s = lambda k, h: st(k).reshape(depth, bs, ts, h, HEAD_DIM)
    return (y_p, y_s.reshape(bs, ts, d), st("rs_p"), st("rs_s"),
            kv_p(kv_stacked[0], H_SB), kv_p(kv_stacked[1], H_SB), kv_s("sks", H_SB), kv_s("svs", H_SB),
            kv_p(kv_stacked[2], H_FOX), kv_p(kv_stacked[3], H_FOX), jnp.transpose(st("flp"), (0, 1, 3, 2)),
            kv_s("fks", H_FOX), kv_s("fvs", H_FOX), jnp.transpose(st("fls"), (0, 1, 3, 2)))
```

```python
import functools

import jax
import jax.numpy as jnp
import numpy as np
from jax import lax
from jax.experimental import pallas as pl
from jax.experimental.pallas import tpu as pltpu

F32 = jnp.float32
BF16 = jnp.bfloat16

HEAD_DIM = 64
H_RET, H_SB, H_FOX = 6, 5, 5
D_RET, D_SB, D_FOX = H_RET * HEAD_DIM, H_SB * HEAD_DIM, H_FOX * HEAD_DIM
PAGE_SIZE = 128
RET_CHUNK = 128
ATT_BLOCK = 128
ROW_CHUNK = 32
LANES = 128
SUBLANES = 8
ROPE_BASE = 10000.0
EPS = 1e-6
QK_SCALE = HEAD_DIM ** -0.5
SB_DEAD = -104.0
NEG_BIG = -1e30
PIECE = 384
VMEM_LIMIT = 48 * 1024 * 1024


def _dot(a, b):
    return jnp.dot(a, b, preferred_element_type=F32)


def _dot_nt(a, b):
    return lax.dot_general(a, b, (((1,), (1,)), ((), ())), preferred_element_type=F32)


def _dot_tn(a, b):
    return lax.dot_general(a, b, (((0,), (0,)), ((), ())), preferred_element_type=F32)


def _split2(x):
    hi = x.astype(BF16)
    lo = (x - hi.astype(F32)).astype(BF16)
    return hi, lo


def _split3(x):
    p1 = x.astype(BF16)
    r1 = x - p1.astype(F32)
    p2 = r1.astype(BF16)
    p3 = (r1 - p2.astype(F32)).astype(BF16)
    return p1, p2, p3


def _dot_exact2(x, m):
    hi, lo = _split2(x)
    return _dot(hi, m) + _dot(lo, m)


def _dot_exact3(x, m):
    p1, p2, p3 = _split3(x)
    return _dot(p1, m) + _dot(p2, m) + _dot(p3, m)


def _softplus(z):
    return jnp.maximum(z, 0.0) + jnp.log(1.0 + jnp.exp(-jnp.abs(z)))


def _log_sigmoid(x):
    return jnp.minimum(x, 0.0) - jnp.log1p(jnp.exp(-jnp.abs(x)))


def _silu(g):
    return g * jax.nn.sigmoid(g)


def _rotary_rows(x, cos, sin_signed):
    lane = lax.broadcasted_iota(jnp.int32, (x.shape[0], LANES), 1)
    first_half = (lane % HEAD_DIM) < (HEAD_DIM // 2)
    out = []
    for c in range(x.shape[1] // LANES):
        xs = x[:, c * LANES:(c + 1) * LANES]
        partner = jnp.where(first_half,
                            pltpu.roll(xs, LANES - HEAD_DIM // 2, axis=1),
                            pltpu.roll(xs, HEAD_DIM // 2, axis=1))
        out.append(xs * cos + partner * sin_signed)
    return jnp.concatenate(out, axis=1)


def _inproj_prompt_kernel(x_ref, g_ref, cos_ref, sin_ref, cost_ref, sint_ref, bf_ref, gk_ref,
                          wrow_ref, wkr_ref, wks_ref, wvs_ref, wkf_ref, wvf_ref, wfl_ref, *rest, first_of):
    (qr_ref, vr_ref, gr_ref, qs_ref, gs_ref, qf_ref, gf_ref,
     krt_ref, kst_ref, vst_ref, kft_ref, vft_ref, lft_ref) = rest[-13:]
    if first_of is not None:
        for ref in (kst_ref, vst_ref, kft_ref, vft_ref):
            if first_of > 1:
                ref[1:] = jnp.zeros((first_of - 1,) + ref.shape[1:], F32)
        kst_ref, vst_ref, kft_ref, vft_ref = (r.at[0] for r in (kst_ref, vst_ref, kft_ref, vft_ref))
    x = x_ref[...]
    ms = jnp.mean(x * x, axis=-1, keepdims=True)
    h = (x * lax.rsqrt(ms + EPS) * g_ref[...]).astype(BF16)

    row = _dot_nt(h, wrow_ref[...])
    qr_ref[...] = _rotary_rows(row[:, 0:D_RET], cos_ref[...], sin_ref[...])
    vr_ref[...] = row[:, D_RET:2 * D_RET]
    gr_ref[...] = _silu(row[:, 2 * D_RET:3 * D_RET])
    o = 3 * D_RET
    qs_ref[...] = row[:, o:o + D_SB]
    gs_ref[...] = _silu(row[:, o + D_SB:o + 2 * D_SB])
    qf_ref[...] = row[:, o + 2 * D_SB:o + 3 * D_SB]
    gf_ref[...] = _silu(row[:, o + 3 * D_SB:o + 4 * D_SB])

    half = HEAD_DIM // 2
    krt = _dot_nt(wkr_ref[...], h)
    cost, sint = cost_ref[...], sint_ref[...]
    for hd in range(H_RET):
        x1 = krt[hd * HEAD_DIM:hd * HEAD_DIM + half]
        x2 = krt[hd * HEAD_DIM + half:(hd + 1) * HEAD_DIM]
        krt_ref[hd * HEAD_DIM:hd * HEAD_DIM + half, :] = (x1 * cost - x2 * sint) * QK_SCALE
        krt_ref[hd * HEAD_DIM + half:(hd + 1) * HEAD_DIM, :] = (x1 * sint + x2 * cost) * QK_SCALE

    kst_ref[...] = _dot_nt(wks_ref[...], h)
    vst_ref[...] = _dot_nt(wvs_ref[...], h)
    vft_ref[...] = _dot_nt(wvf_ref[...], h)
    kft = _dot_nt(wkf_ref[...], h)
    gk = gk_ref[...]
    for hd in range(H_FOX):
        xh = kft[hd * HEAD_DIM:(hd + 1) * HEAD_DIM]
        msh = jnp.mean(xh * xh, axis=0, keepdims=True)
        kft_ref[hd * HEAD_DIM:(hd + 1) * HEAD_DIM, :] = xh * lax.rsqrt(msh + EPS) * gk
    lft_ref[...] = _log_sigmoid(_dot_nt(wfl_ref[...], h) + bf_ref[...])


def _inproj_prompt(x, g_norm, tabs, b_f8, gk_col, w, tm, layer, depth, stacked):
    bsz, t, d = x.shape
    n_row = 3 * D_RET + 4 * D_SB
    grid = (bsz, t // tm)
    row_spec = lambda n: pl.BlockSpec((None, tm, n), lambda b, i: (b, i, 0))
    col_spec = lambda n: pl.BlockSpec((None, n, tm), lambda b, i: (b, 0, i))
    full = lambda a: pl.BlockSpec(a.shape, lambda b, i: (0,) * a.ndim)
    cos, sin, cost, sint = tabs
    in_specs = [
        row_spec(d), full(g_norm),
        pl.BlockSpec((tm, LANES), lambda b, i: (i, 0)), pl.BlockSpec((tm, LANES), lambda b, i: (i, 0)),
        pl.BlockSpec((HEAD_DIM // 2, tm), lambda b, i: (0, i)), pl.BlockSpec((HEAD_DIM // 2, tm), lambda b, i: (0, i)),
        full(b_f8), full(gk_col),
    ] + [full(a) for a in w]
    rs = lambda n: jax.ShapeDtypeStruct((bsz, t, n), F32)
    cs = lambda n: jax.ShapeDtypeStruct((bsz, n, t), F32)
    stk = jax.ShapeDtypeStruct((depth, bsz, D_SB, t), F32)
    aliases, extra = {}, []
    if stacked is None:
        stk_spec = pl.BlockSpec((depth, None, D_SB, tm), lambda b, i: (0, b, 0, i))
    else:
        stk_spec = pl.BlockSpec((None, None, D_SB, tm), lambda b, i: (layer, b, 0, i))
        extra = list(stacked)
        in_specs += [pl.BlockSpec(memory_space=pl.ANY)] * len(extra)
        first_in, first_out = len(in_specs) - len(extra), 8
        aliases = {first_in + k: first_out + k for k in range(len(extra))}
    out_shape = [rs(D_RET), rs(D_RET), rs(D_RET), rs(D_SB), rs(D_SB), rs(D_FOX), rs(D_FOX),
                 cs(D_RET), stk, stk, stk, stk, cs(SUBLANES)]
    out_specs = [row_spec(D_RET)] * 3 + [row_spec(D_SB)] * 4 + \
                [col_spec(D_RET)] + [stk_spec] * 4 + [col_spec(SUBLANES)]
    return pl.pallas_call(
        functools.partial(_inproj_prompt_kernel, first_of=depth if stacked is None else None),
        grid=grid, in_specs=in_specs, out_specs=out_specs, out_shape=out_shape,
        input_output_aliases=aliases,
        compiler_params=pltpu.CompilerParams(dimension_semantics=("parallel", "parallel"),
                                             vmem_limit_bytes=VMEM_LIMIT),
        name="inproj_prompt",
    )(x, g_norm, cos, sin, cost, sint, b_f8, gk_col, *w, *extra)


def _ret_prompt_kernel(qr_ref, krt_ref, vr_ref, gr_ref, dmask_ref, qdec_ref, kdec_ref, cdec_ref, ggn_ref,
                       o_ref, sfin_ref, s_scr):
    c = pl.program_id(1)

    @pl.when(c == 0)
    def _():
        s_scr[...] = jnp.zeros_like(s_scr)

    for sub in range(qr_ref.shape[0] // RET_CHUNK):
        rows = slice(sub * RET_CHUNK, (sub + 1) * RET_CHUNK)
        for hd in range(H_RET):
            sl = slice(hd * HEAD_DIM, (hd + 1) * HEAD_DIM)
            q = qr_ref[rows, sl].astype(BF16)
            kt = krt_ref[sl, rows]
            v = vr_ref[rows, sl].astype(BF16)
            sc = _dot(q, kt.astype(BF16)) * dmask_ref[hd]
            inner = _dot(sc.astype(BF16), v)
            s_old = s_scr[hd]
            cross = _dot(q, s_old.astype(BF16)) * qdec_ref[hd]
            o = inner + cross
            s_scr[hd] = s_old * cdec_ref[hd] + _dot((kt * kdec_ref[hd]).astype(BF16), v)
            mu = jnp.mean(o, axis=-1, keepdims=True)
            dev = o - mu
            var = jnp.mean(dev * dev, axis=-1, keepdims=True)
            o_ref[rows, sl] = dev * lax.rsqrt(var + EPS) * ggn_ref[:, sl] * gr_ref[rows, sl]

    @pl.when(c == pl.num_programs(1) - 1)
    def _():
        sfin_ref[...] = s_scr[...]


def _ret_prompt(qr, krt, vr, gr, dec, ggn):
    bsz, t, _ = qr.shape
    blk = RET_CHUNK
    dmask, qdec, kdec, cdec = dec
    row = pl.BlockSpec((None, blk, D_RET), lambda b, c: (b, c, 0))
    full = lambda a: pl.BlockSpec(a.shape, lambda b, c: (0,) * a.ndim)
    return pl.pallas_call(
        _ret_prompt_kernel, grid=(bsz, t // blk),
        in_specs=[row, pl.BlockSpec((None, D_RET, blk), lambda b, c: (b, 0, c)), row, row,
                  full(dmask), full(qdec), full(kdec), full(cdec), full(ggn)],
        out_specs=[row, pl.BlockSpec((None, H_RET, HEAD_DIM, HEAD_DIM), lambda b, c: (b, 0, 0, 0))],
        out_shape=[jax.ShapeDtypeStruct((bsz, t, D_RET), F32),
                   jax.ShapeDtypeStruct((bsz, H_RET, HEAD_DIM, HEAD_DIM), F32)],
        scratch_shapes=[pltpu.VMEM((H_RET, HEAD_DIM, HEAD_DIM), F32)],
        compiler_params=pltpu.CompilerParams(dimension_semantics=("parallel", "arbitrary"),
                                             vmem_limit_bytes=VMEM_LIMIT),
        name="ret_prompt",
    )(qr, krt, vr, gr, dmask, qdec, kdec, cdec, ggn)


def _sb_prompt_kernel(q_ref, kt_ref, vt_ref, g_ref, lmat_ref, o_ref,
                      q_scr, car_scr, acc_scr, z_scr, ls_scr, hi_scr, lo_scr, a_scr, tot_scr):
    i = pl.program_id(1)
    tq = q_ref.shape[0]
    reps = tq // LANES
    for hd in range(H_SB):
        sl = slice(hd * HEAD_DIM, (hd + 1) * HEAD_DIM)
        q_scr[hd] = (q_ref[:, sl] * QK_SCALE).astype(BF16)
    car_scr[...] = jnp.zeros_like(car_scr)
    acc_scr[...] = jnp.zeros_like(acc_scr)

    def block(j, diag):
        off = pl.multiple_of(j * tq, tq)
        lmat = lmat_ref[...]
        for hd in range(H_SB):
            sl = slice(hd * HEAD_DIM, (hd + 1) * HEAD_DIM)
            z_scr[hd] = _dot(q_scr[hd], kt_ref[sl, pl.ds(off, tq)].astype(BF16))
        chunk_masks = []
        for r in range(0, tq, ROW_CHUNK):
            if diag:
                rowi = lax.broadcasted_iota(jnp.int32, (ROW_CHUNK, tq), 0) + r
                coli = lax.broadcasted_iota(jnp.int32, (ROW_CHUNK, tq), 1)
                chunk_masks.append(coli < rowi)
            else:
                chunk_masks.append(None)
        for hd in range(H_SB):
            for ci, r in enumerate(range(0, tq, ROW_CHUNK)):
                rows = slice(r, r + ROW_CHUNK)
                z = z_scr[hd, rows, :]
                sp = _softplus(z)
                ls_scr[hd, rows, :] = z - sp
                if diag:
                    sp = jnp.where(chunk_masks[ci], sp, 0.0)
                hi, lo = _split2(sp)
                hi_scr[hd, rows, :] = hi
                lo_scr[hd, rows, :] = lo
                tot = jnp.sum(sp, axis=-1, keepdims=True)
                tot_scr[hd, rows, :] = jnp.broadcast_to(tot, (ROW_CHUNK, LANES))
        for hd in range(H_SB):
            z_scr[hd] = _dot(hi_scr[hd], lmat) + _dot(lo_scr[hd], lmat)
        for hd in range(H_SB):
            for ci, r in enumerate(range(0, tq, ROW_CHUNK)):
                rows = slice(r, r + ROW_CHUNK)
                carry = car_scr[hd, rows, :]
                a = jnp.exp(ls_scr[hd, rows, :] - z_scr[hd, rows, :] + jnp.tile(carry, (1, reps)))
                if diag:
                    a = jnp.where(chunk_masks[ci], a, 0.0)
                a_scr[hd, rows, :] = a.astype(BF16)
                car_scr[hd, rows, :] = carry - tot_scr[hd, rows, :]
        for hd in range(H_SB):
            sl = slice(hd * HEAD_DIM, (hd + 1) * HEAD_DIM)
            acc_scr[hd] += _dot_nt(a_scr[hd], vt_ref[sl, pl.ds(off, tq)].astype(BF16))
        return jnp.max(car_scr[...])

    cmax = block(i, True)

    def cond(st):
        j, cmax = st
        return jnp.logical_and(j >= 0, cmax > SB_DEAD)

    def body(st):
        j, _ = st
        return j - 1, block(j, False)

    lax.while_loop(cond, body, (i - 1, cmax))
    for hd in range(H_SB):
        sl = slice(hd * HEAD_DIM, (hd + 1) * HEAD_DIM)
        o_ref[:, sl] = acc_scr[hd] * g_ref[:, sl]


def _sb_prompt(qs, kst, vst, gs, lmat, tq, layer):
    bsz, t, _ = qs.shape
    row = pl.BlockSpec((None, tq, D_SB), lambda b, i: (b, i, 0))
    seq = pl.BlockSpec((None, None, D_SB, t), lambda b, i: (layer, b, 0, 0))
    return pl.pallas_call(
        _sb_prompt_kernel, grid=(bsz, t // tq),
        in_specs=[row, seq, seq, row, pl.BlockSpec(lmat.shape, lambda b, i: (0, 0))],
        out_specs=row, out_shape=jax.ShapeDtypeStruct((bsz, t, D_SB), F32),
        scratch_shapes=[pltpu.VMEM((H_SB, tq, HEAD_DIM), BF16), pltpu.VMEM((H_SB, tq, LANES), F32),
                        pltpu.VMEM((H_SB, tq, HEAD_DIM), F32),
                        pltpu.VMEM((H_SB, tq, tq), F32), pltpu.VMEM((H_SB, tq, tq), F32),
                        pltpu.VMEM((H_SB, tq, tq), BF16), pltpu.VMEM((H_SB, tq, tq), BF16),
                        pltpu.VMEM((H_SB, tq, tq), BF16), pltpu.VMEM((H_SB, tq, LANES), F32)],
        compiler_params=pltpu.CompilerParams(dimension_semantics=("parallel", "arbitrary"),
                                             vmem_limit_bytes=VMEM_LIMIT),
        name="sb_prompt",
    )(qs, kst, vst, gs, lmat)


def _fox_prompt_kernel(q_ref, kt_ref, vt_ref, lft_ref, g_ref, gq_ref, seg_ref, umat_ref, o_ref,
                       c_scr, q_scr, m_scr, acc_scr, s_scr, p_scr, a_scr):
    i = pl.program_id(1)
    tq = q_ref.shape[0]
    t = kt_ref.shape[1]
    cw = umat_ref.shape[0]

    @pl.when(i == 0)
    def _():
        umat = umat_ref[...]
        carry = jnp.zeros((SUBLANES, 1), F32)
        for c in range(t // cw):
            cs = _dot_exact3(lft_ref[:, c * cw:(c + 1) * cw], umat) + carry
            c_scr[:, c * cw:(c + 1) * cw] = cs
            carry = cs[:, cw - 1:cw]

    qraw = q_ref[...]
    ssq = _dot_exact2(qraw * qraw, seg_ref[...])
    qn = qraw * lax.rsqrt(ssq * (1.0 / HEAD_DIM) + EPS) * gq_ref[...] * QK_SCALE
    for hd in range(H_FOX):
        q_scr[hd] = qn[:, hd * HEAD_DIM:(hd + 1) * HEAD_DIM].astype(BF16)
    m_scr[...] = jnp.full_like(m_scr, NEG_BIG)
    acc_scr[...] = jnp.zeros_like(acc_scr)
    rowi = lax.broadcasted_iota(jnp.int32, (tq, tq), 0)
    coli = lax.broadcasted_iota(jnp.int32, (tq, tq), 1)
    diag_mask = coli <= rowi

    def block(off, width, mask):
        ones_rows = jnp.ones((HEAD_DIM, width), BF16)
        for hd in range(H_FOX):
            sl = slice(hd * HEAD_DIM, (hd + 1) * HEAD_DIM)
            kt = kt_ref[sl, pl.ds(off, width)].astype(BF16)
            s = _dot(q_scr[hd], kt) - c_scr[hd:hd + 1, pl.ds(off, width)]
            if mask is not None:
                s = jnp.where(mask, s, NEG_BIG)
            s_scr[hd, :, :width] = s
        for hd in range(H_FOX):
            for r in range(0, tq, ROW_CHUNK):
                rows = slice(r, r + ROW_CHUNK)
                s = s_scr[hd, rows, :width]
                m_old = m_scr[hd, rows, :]
                mx = jnp.max(s, axis=-1, keepdims=True)
                m_new = jnp.maximum(m_old, jnp.broadcast_to(mx, (ROW_CHUNK, LANES)))
                p_scr[hd, rows, :width] = jnp.exp(s - jnp.tile(m_new, (1, width // LANES))).astype(BF16)
                a_scr[hd, rows, :] = jnp.exp(m_old - m_new)
                m_scr[hd, rows, :] = m_new
        for hd in range(H_FOX):
            sl = slice(hd * HEAD_DIM, (hd + 1) * HEAD_DIM)
            vt = jnp.concatenate([vt_ref[sl, pl.ds(off, width)].astype(BF16), ones_rows], axis=0)
            acc_scr[hd] = a_scr[hd] * acc_scr[hd] + _dot_nt(p_scr[hd, :, :width], vt)

    def body(j, carry):
        block(pl.multiple_of(j * 2 * tq, 2 * tq), 2 * tq, None)
        return carry

    lax.fori_loop(0, i // 2, body, 0)

    @pl.when(i % 2 == 1)
    def _():
        block(pl.multiple_of((i - 1) * tq, tq), tq, None)

    block(pl.multiple_of(i * tq, tq), tq, diag_mask)
    for hd in range(H_FOX):
        sl = slice(hd * HEAD_DIM, (hd + 1) * HEAD_DIM)
        acc = acc_scr[hd]
        o_ref[:, sl] = acc[:, :HEAD_DIM] / acc[:, HEAD_DIM:HEAD_DIM + 1] * g_ref[:, sl]


def _fox_prompt(qf, kft, vft, lft, gf, gq_row, seg, umat, tq, layer):
    bsz, t, _ = qf.shape
    row = pl.BlockSpec((None, tq, D_FOX), lambda b, i: (b, i, 0))
    seq = pl.BlockSpec((None, None, D_FOX, t), lambda b, i: (layer, b, 0, 0))
    full = lambda a: pl.BlockSpec(a.shape, lambda b, i: (0,) * a.ndim)
    return pl.pallas_call(
        _fox_prompt_kernel, grid=(bsz, t // tq),
        in_specs=[row, seq, seq, pl.BlockSpec((None, SUBLANES, t), lambda b, i: (b, 0, 0)), row,
                  full(gq_row), full(seg), full(umat)],
        out_specs=row, out_shape=jax.ShapeDtypeStruct((bsz, t, D_FOX), F32),
        scratch_shapes=[pltpu.VMEM((SUBLANES, t), F32), pltpu.VMEM((H_FOX, tq, HEAD_DIM), BF16),
                        pltpu.VMEM((H_FOX, tq, LANES), F32), pltpu.VMEM((H_FOX, tq, 2 * HEAD_DIM), F32),
                        pltpu.VMEM((H_FOX, tq, 2 * tq), F32), pltpu.VMEM((H_FOX, tq, 2 * tq), BF16),
                        pltpu.VMEM((H_FOX, tq, LANES), F32)],
        compiler_params=pltpu.CompilerParams(dimension_semantics=("parallel", "arbitrary"),
                                             vmem_limit_bytes=VMEM_LIMIT),
        name="fox_prompt",
    )(qf, kft, vft, lft, gf, gq_row, seg, umat)


def _outproj_kernel(y_ref, or_ref, os_ref, of_ref, p_ref, wor_ref, wos_ref, wof_ref, wpe_ref, gpe_ref, wpg_ref,
                    o_ref):
    m = (_dot(or_ref[...].astype(BF16), wor_ref[...]) + _dot(os_ref[...].astype(BF16), wos_ref[...])
         + _dot(of_ref[...].astype(BF16), wof_ref[...]))
    y1 = y_ref[...] + m
    ms = jnp.mean(y1 * y1, axis=-1, keepdims=True)
    n = (y1 * lax.rsqrt(ms + EPS) * gpe_ref[...]).astype(BF16)
    gate = jax.nn.sigmoid(_dot(n, wpg_ref[...]))
    pe = _dot(p_ref[...].astype(BF16), wpe_ref[...])
    o_ref[...] = y1 + pe * gate


def _outproj(y, o_r, o_s, o_f, p, w, tm):
    n, d = y.shape
    row = lambda a: pl.BlockSpec((tm, a.shape[1]), lambda i: (i, 0))
    full = lambda a: pl.BlockSpec(a.shape, lambda i: (0,) * a.ndim)
    return pl.pallas_call(
        _outproj_kernel, grid=(n // tm,),
        in_specs=[row(y), row(o_r), row(o_s), row(o_f), row(p)] + [full(a) for a in w],
        out_specs=row(y), out_shape=jax.ShapeDtypeStruct((n, d), F32),
        compiler_params=pltpu.CompilerParams(dimension_semantics=("parallel",), vmem_limit_bytes=VMEM_LIMIT),
        name="outproj",
    )(y, o_r, o_s, o_f, p, *w)


def _inproj_sample_kernel(x_ref, g_ref, wrow_ref, wkr_ref, wks_ref, wvs_ref, wkf_ref, wvf_ref, wfl_ref,
                          o_ref, flt_ref):
    x = x_ref[...]
    ms = jnp.mean(x * x, axis=-1, keepdims=True)
    h = (x * lax.rsqrt(ms + EPS) * g_ref[...]).astype(BF16)
    row = _dot_nt(h, wrow_ref[...])
    o_ref[...] = jnp.zeros_like(o_ref)

    def put(slot, val):
        o_ref[:, slot * PIECE:slot * PIECE + val.shape[1]] = val

    put(0, row[:, 0:D_RET])
    put(2, row[:, D_RET:2 * D_RET])
    put(3, _silu(row[:, 2 * D_RET:3 * D_RET]))
    o = 3 * D_RET
    put(4, row[:, o:o + D_SB])
    put(7, _silu(row[:, o + D_SB:o + 2 * D_SB]))
    put(8, row[:, o + 2 * D_SB:o + 3 * D_SB])
    put(11, _silu(row[:, o + 3 * D_SB:o + 4 * D_SB]))
    put(1, _dot_nt(h, wkr_ref[...]))
    put(5, _dot_nt(h, wks_ref[...]))
    put(6, _dot_nt(h, wvs_ref[...]))
    put(9, _dot_nt(h, wkf_ref[...]))
    put(10, _dot_nt(h, wvf_ref[...]))
    flt_ref[...] = _dot_nt(wfl_ref[...], h)


def _inproj_sample(x, g_norm, w):
    n, d = x.shape
    ncol = 12 * PIECE
    full = lambda a: pl.BlockSpec(a.shape, lambda i: (0,) * a.ndim)
    return pl.pallas_call(
        _inproj_sample_kernel, grid=(1,),
        in_specs=[full(x), full(g_norm)] + [full(a) for a in w],
        out_specs=[pl.BlockSpec((n, ncol), lambda i: (0, 0)), pl.BlockSpec((SUBLANES, n), lambda i: (0, 0))],
        out_shape=[jax.ShapeDtypeStruct((n, ncol), F32), jax.ShapeDtypeStruct((SUBLANES, n), F32)],
        compiler_params=pltpu.CompilerParams(dimension_semantics=("arbitrary",), vmem_limit_bytes=VMEM_LIMIT),
        name="inproj_sample",
    )(x, g_norm, *w)


def _ret_sample_kernel(q_ref, k_ref, v_ref, g_ref, s0_ref, cos_ref, sin_ref, dmask_ref, qdec_ref, kdec_ref,
                       cdec_ref, ggn_ref, o_ref, s_ref, *, ts):
    n_seq = s0_ref.shape[0]
    cos = jnp.concatenate([cos_ref[...]] * n_seq, axis=0)
    sin = jnp.concatenate([sin_ref[...]] * n_seq, axis=0)
    qr = _rotary_rows(q_ref[...], cos, sin)
    kr = _rotary_rows(k_ref[...], cos, sin) * QK_SCALE
    for sq in range(n_seq):
        rows = slice(sq * ts, (sq + 1) * ts)
        for hd in range(H_RET):
            sl = slice(hd * HEAD_DIM, (hd + 1) * HEAD_DIM)
            q = qr[rows, sl].astype(BF16)
            k = kr[rows, sl]
            v = v_ref[rows, sl].astype(BF16)
            sc = _dot_nt(q, k.astype(BF16)) * dmask_ref[hd]
            inner = _dot(sc.astype(BF16), v)
            s_old = s0_ref[sq, hd]
            cross = _dot(q, s_old.astype(BF16)) * qdec_ref[hd]
            o = inner + cross
            s_ref[sq, hd] = s_old * cdec_ref[hd] + _dot_tn((k * kdec_ref[hd]).astype(BF16), v)
            mu = jnp.mean(o, axis=-1, keepdims=True)
            dev = o - mu
            var = jnp.mean(dev * dev, axis=-1, keepdims=True)
            y = dev * lax.rsqrt(var + EPS) * ggn_ref[:, sl]
            o_ref[rows, sl] = y * g_ref[rows, sl]


def _ret_sample(proj, s0, tabs, dec, ggn, ts):
    n = proj.shape[0]
    bsz = n // ts
    per = next(m for m in (4, 2, 1) if bsz % m == 0)
    cos, sin = tabs
    dmask, qdec, kdec, cdec = dec
    piece = lambda k: pl.BlockSpec((per * ts, PIECE), lambda b, k=k: (b, k))
    full = lambda a: pl.BlockSpec(a.shape, lambda b: (0,) * a.ndim)
    st = pl.BlockSpec((per, H_RET, HEAD_DIM, HEAD_DIM), lambda b: (b, 0, 0, 0))
    return pl.pallas_call(
        functools.partial(_ret_sample_kernel, ts=ts), grid=(bsz // per,),
        in_specs=[piece(0), piece(1), piece(2), piece(3), st, full(cos), full(sin),
                  full(dmask), full(qdec), full(kdec), full(cdec), full(ggn)],
        out_specs=[pl.BlockSpec((per * ts, D_RET), lambda b: (b, 0)), st],
        out_shape=[jax.ShapeDtypeStruct((n, D_RET), F32),
                   jax.ShapeDtypeStruct((bsz, H_RET, HEAD_DIM, HEAD_DIM), F32)],
        compiler_params=pltpu.CompilerParams(dimension_semantics=("parallel",), vmem_limit_bytes=VMEM_LIMIT),
        name="ret_sample",
    )(proj, proj, proj, proj, s0, cos, sin, dmask, qdec, kdec, cdec, ggn)


def _head_rows(x, ts):
    rows = H_SB * ts
    r = lax.broadcasted_iota(jnp.int32, (rows, D_SB), 0)
    c = lax.broadcasted_iota(jnp.int32, (rows, D_SB), 1)
    tiled = jnp.concatenate([x] * H_SB, axis=0)
    return jnp.where(r // ts == c // HEAD_DIM, tiled, 0.0)


def _head_cols(x, ts):
    c = lax.broadcasted_iota(jnp.int32, (ts, D_SB), 1)
    out = jnp.zeros((ts, D_SB), F32)
    for hd in range(H_SB):
        out = out + jnp.where(c // HEAD_DIM == hd, x[hd * ts:(hd + 1) * ts], 0.0)
    return out


def _new_token_masks(ts, strict):
    rows = H_SB * ts
    tpos = lax.broadcasted_iota(jnp.int32, (rows, PAGE_SIZE), 0) % ts
    coli = lax.broadcasted_iota(jnp.int32, (rows, PAGE_SIZE), 1)
    return coli < tpos if strict else coli <= tpos


def _sb_blocks(zs, v_dots, lmat, acc, car):
    for (z, mask), v_dot in zip(zs, v_dots):
        sp = _softplus(z)
        ls = z - sp
        if mask is not None:
            sp = jnp.where(mask, sp, 0.0)
        a = jnp.exp(ls - _dot_exact2(sp, lmat) + car)
        if mask is not None:
            a = jnp.where(mask, a, 0.0)
        acc = acc + v_dot(a.astype(BF16))
        car = car - jnp.sum(sp, axis=-1, keepdims=True)
    return acc, car


def _decode_sb_kernel(pt_ref, qs_ref, ks_ref, vs_ref, gs_ref, lmat_ref, *rest, n_grp, ts, head):
    if head:
        pages = rest[:2 * n_grp]
        os_ref, acco_ref, caro_ref = rest[2 * n_grp:]
    else:
        acci_ref, cari_ref = rest[:2]
        pages = rest[2:2 + 2 * n_grp]
        os_ref, qbs_scr, acc_scr, car_scr = rest[2 + 2 * n_grp:]
    lmat = lmat_ref[...]
    order = list(range(n_grp - 1, -1, -1))

    def page_blocks(qbs):
        zs, dots = [], []
        for gi in order:
            kt_ref, vt_ref = pages[2 * gi:2 * gi + 2]
            zs.append((_dot(qbs, kt_ref[...].astype(BF16)), None))
            dots.append(lambda a, r=vt_ref: _dot_nt(a, r[...].astype(BF16)))
        return zs, dots

    qbs_new = _head_rows(qs_ref[:, :D_SB] * QK_SCALE, ts).astype(BF16)
    if head:
        pad = jnp.zeros((PAGE_SIZE - ts, D_SB), F32)
        k_new = jnp.concatenate([ks_ref[:, :D_SB], pad], axis=0).astype(BF16)
        v_new = jnp.concatenate([vs_ref[:, :D_SB], pad], axis=0).astype(BF16)
        zs, dots = page_blocks(qbs_new)
        acc, car = _sb_blocks([(_dot_nt(qbs_new, k_new), _new_token_masks(ts, True))] + zs,
                              [lambda a: _dot(a, v_new)] + dots, lmat,
                              jnp.zeros((H_SB * ts, D_SB), F32), jnp.zeros((H_SB * ts, 1), F32))
        acco_ref[...] = acc
        caro_ref[...] = jnp.broadcast_to(car, caro_ref.shape)
        os_ref[...] = _head_cols(acc, ts) * gs_ref[:, :D_SB]
    else:
        s = pl.program_id(1)

        @pl.when(s == 0)
        def _():
            qbs_scr[...] = qbs_new
            acc_scr[...] = acci_ref[...]
            car_scr[...] = cari_ref[:, 0:1]

        @pl.when(jnp.max(car_scr[...]) > SB_DEAD)
        def _():
            zs, dots = page_blocks(qbs_scr[...])
            acc, car = _sb_blocks(zs, dots, lmat, acc_scr[...], car_scr[...])
            acc_scr[...] = acc
            car_scr[...] = car

        @pl.when(s == pl.num_programs(1) - 1)
        def _():
            os_ref[...] = _head_cols(acc_scr[...], ts) * gs_ref[:, :D_SB]


def _decode_sb(proj, pools, page_table, layer, lmat, ts, n_head, n_grp, state=None):
    n = proj.shape[0]
    bsz, n_pages = page_table.shape
    rows = H_SB * ts
    head = state is None
    per_step = n_head if head else n_grp
    n_steps = 1 if head else (n_pages - n_head) // n_grp
    last = n_pages - 1 if head else n_pages - n_head - 1
    piece = lambda k: pl.BlockSpec((ts, PIECE), lambda b, s, pt, k=k: (b, k))

    def page_spec(gi):
        return pl.BlockSpec((None, None, D_SB, PAGE_SIZE),
                            lambda b, s, pt, gi=gi: (layer, pt[b, last - s * per_step - (per_step - 1 - gi)], 0, 0))

    page_specs = [page_spec(gi) for gi in range(per_step) for _ in range(2)]
    page_args = [p for _ in range(per_step) for p in pools]
    out_row = pl.BlockSpec((ts, D_SB), lambda b, s, pt: (b, 0))
    st_acc = pl.BlockSpec((rows, D_SB), lambda b, s, pt: (b, 0))
    st_car = pl.BlockSpec((rows, LANES), lambda b, s, pt: (b, 0))
    in_specs = [piece(4), piece(5), piece(6), piece(7), pl.BlockSpec(lmat.shape, lambda b, s, pt: (0, 0))]
    args = [proj] * 4 + [lmat]
    if head:
        out_specs = [out_row, st_acc, st_car]
        out_shape = [jax.ShapeDtypeStruct((n, D_SB), F32), jax.ShapeDtypeStruct((bsz * rows, D_SB), F32),
                     jax.ShapeDtypeStruct((bsz * rows, LANES), F32)]
        scratch = []
    else:
        in_specs += [st_acc, st_car]
        args += list(state)
        out_specs = out_row
        out_shape = jax.ShapeDtypeStruct((n, D_SB), F32)
        scratch = [pltpu.VMEM((rows, D_SB), BF16), pltpu.VMEM((rows, D_SB), F32), pltpu.VMEM((rows, 1), F32)]
    grid_spec = pltpu.PrefetchScalarGridSpec(
        num_scalar_prefetch=1, grid=(bsz, n_steps), in_specs=in_specs + page_specs, out_specs=out_specs,
        scratch_shapes=scratch)
    return pl.pallas_call(
        functools.partial(_decode_sb_kernel, n_grp=per_step, ts=ts, head=head), grid_spec=grid_spec,
        out_shape=out_shape,
        compiler_params=pltpu.CompilerParams(dimension_semantics=("parallel", "arbitrary"),
                                             vmem_limit_bytes=VMEM_LIMIT),
        name="decode_sb_head" if head else "decode_sb_tail",
    )(page_table, *args, *page_args)


def _decode_fox_kernel(pt_ref, qf_ref, kf_ref, vf_ref, gf_ref, flt_ref, bf_ref, gq_ref, gk_ref, seg_ref, lmat_ref,
                       kpool_ref, vpool_ref, lfpool_ref, of_ref, kfn_ref, lfn_ref,
                       kbuf, vbuf, lfall, sem, lfsem, *, n_grp, ts, layer):
    b = pl.program_id(0)
    n_seq = pl.num_programs(0)
    n_pages = pt_ref.shape[1]
    n_groups = n_pages // n_grp
    rows = H_FOX * ts
    lmat = lmat_ref[...]

    def group_copies(seq, g, slot):
        out = []
        for gi in range(n_grp):
            page = pt_ref[seq, n_pages - 1 - g * n_grp - gi]
            out.append(pltpu.make_async_copy(kpool_ref.at[layer, page], kbuf.at[slot, gi], sem.at[slot, 0]))
            out.append(pltpu.make_async_copy(vpool_ref.at[layer, page], vbuf.at[slot, gi], sem.at[slot, 1]))
        return out

    lf_copy = pltpu.make_async_copy(lfpool_ref.at[layer], lfall, lfsem.at[0])

    @pl.when(b == 0)
    def _():
        lf_copy.start()
        for cp in group_copies(b, 0, 0):
            cp.start()

    def rep_heads(x):
        return jnp.concatenate([jnp.broadcast_to(x[hd:hd + 1], (ts, x.shape[1])) for hd in range(H_FOX)], axis=0)

    def fox_block(sc, mask, v_dot, m_old, l, acc):
        if mask is not None:
            sc = jnp.where(mask, sc, NEG_BIG)
        m_new = jnp.maximum(m_old, jnp.max(sc, axis=-1, keepdims=True))
        alpha = jnp.exp(m_old - m_new)
        p = jnp.exp(sc - m_new)
        l = alpha * l + jnp.sum(p, axis=-1, keepdims=True)
        acc = alpha * acc + v_dot(p.astype(BF16))
        return m_new, l, acc

    pad = jnp.zeros((PAGE_SIZE - ts, D_FOX), F32)
    seg = seg_ref[...]
    qraw = qf_ref[:, :D_FOX]
    qn = qraw * lax.rsqrt(_dot_exact2(qraw * qraw, seg) * (1.0 / HEAD_DIM) + EPS) * gq_ref[...] * QK_SCALE
    qbf = _head_rows(qn, ts).astype(BF16)
    kraw = kf_ref[:, :D_FOX]
    kn = kraw * lax.rsqrt(_dot_exact2(kraw * kraw, seg) * (1.0 / HEAD_DIM) + EPS) * gk_ref[...]
    kfn_ref[...] = kn
    lf_new = _log_sigmoid(flt_ref[...] + bf_ref[...])
    lfn_ref[...] = lf_new
    lf_pad = jnp.concatenate([lf_new, jnp.zeros((SUBLANES, PAGE_SIZE - ts), F32)], axis=1)
    lf_rows = rep_heads(lf_pad)
    cn = jnp.sum(lf_rows, axis=-1, keepdims=True) - _dot_exact3(lf_rows, lmat)
    kn_pad = jnp.concatenate([kn, pad], axis=0).astype(BF16)
    vf_new = jnp.concatenate([vf_ref[:, :D_FOX], pad], axis=0).astype(BF16)
    m, l, acc = fox_block(_dot_nt(qbf, kn_pad) - cn, _new_token_masks(ts, False), lambda p: _dot(p, vf_new),
                          jnp.full((rows, 1), NEG_BIG, F32), jnp.zeros((rows, 1), F32),
                          jnp.zeros((rows, D_FOX), F32))

    @pl.when(b == 0)
    def _():
        lf_copy.wait()

    rcar = jnp.zeros((rows, 1), F32)
    for g in range(n_groups):
        slot = g % 2
        if g + 1 < n_groups:
            for cp in group_copies(b, g + 1, 1 - slot):
                cp.start()
        else:
            @pl.when(b + 1 < n_seq)
            def _():
                for cp in group_copies(b + 1, 0, 1 - slot):
                    cp.start()
        for cp in group_copies(b, g, slot):
            cp.wait()
        lf_rows = []
        for gi in range(n_grp):
            page = pt_ref[b, n_pages - 1 - g * n_grp - gi]
            lf_pg = lfall[:, pl.ds(page, 1), :]
            lf_rows.append(jnp.concatenate([jnp.broadcast_to(lf_pg[hd], (ts, PAGE_SIZE)) for hd in range(H_FOX)],
                                           axis=0))
        later = _dot_exact3(jnp.concatenate(lf_rows, axis=0), lmat)
        biases = []
        for k, lr in enumerate(lf_rows):
            biases.append(later[k * rows:(k + 1) * rows] + rcar)
            rcar = rcar + jnp.sum(lr, axis=-1, keepdims=True)
        kt_all = jnp.concatenate([kbuf[slot, gi].astype(BF16) for gi in range(n_grp)], axis=1)
        vt_all = jnp.concatenate([vbuf[slot, gi].astype(BF16) for gi in range(n_grp)], axis=1)
        sc = _dot(qbf, kt_all) + jnp.concatenate(biases, axis=1)
        m, l, acc = fox_block(sc, None, lambda p, vt_all=vt_all: _dot_nt(p, vt_all), m, l, acc)

    of_ref[...] = _head_cols(acc / l, ts) * gf_ref[:, :D_FOX]


def _decode_fox(proj, flt, pools, lf_pool, page_table, layer, b_f8, gq_row, gk_row, seg, lmat, ts, n_grp):
    n = proj.shape[0]
    bsz, n_pages = page_table.shape
    assert n_pages % (2 * n_grp) == 0
    piece = lambda k: pl.BlockSpec((ts, PIECE), lambda b, pt, k=k: (b, k))
    full = lambda a: pl.BlockSpec(a.shape, lambda b, pt: (0,) * a.ndim)
    hbm = pl.BlockSpec(memory_space=pl.ANY)
    out_row = pl.BlockSpec((ts, D_FOX), lambda b, pt: (b, 0))
    small = pl.BlockSpec((None, SUBLANES, ts), lambda b, pt: (b, 0, 0))
    grid_spec = pltpu.PrefetchScalarGridSpec(
        num_scalar_prefetch=1, grid=(bsz,),
        in_specs=[piece(8), piece(9), piece(10), piece(11), small,
                  full(b_f8), full(gq_row), full(gk_row), full(seg), full(lmat), hbm, hbm, hbm],
        out_specs=[out_row, out_row, small],
        scratch_shapes=[pltpu.VMEM((2, n_grp, D_FOX, PAGE_SIZE), F32), pltpu.VMEM((2, n_grp, D_FOX, PAGE_SIZE), F32),
                        pltpu.VMEM(lf_pool.shape[1:], F32), pltpu.SemaphoreType.DMA((2, 2)),
                        pltpu.SemaphoreType.DMA((1,))],
    )
    return pl.pallas_call(
        functools.partial(_decode_fox_kernel, n_grp=n_grp, ts=ts, layer=layer), grid_spec=grid_spec,
        out_shape=[jax.ShapeDtypeStruct((n, D_FOX), F32), jax.ShapeDtypeStruct((n, D_FOX), F32),
                   jax.ShapeDtypeStruct((bsz, SUBLANES, ts), F32)],
        compiler_params=pltpu.CompilerParams(dimension_semantics=("arbitrary",), vmem_limit_bytes=VMEM_LIMIT),
        name="decode_fox",
    )(page_table, *([proj] * 4), flt, b_f8, gq_row, gk_row, seg, lmat, *pools, lf_pool)


def _rotary_tables(pos):
    half = HEAD_DIM // 2
    inv = ROPE_BASE ** (-jnp.arange(half, dtype=F32) / half)
    ang = pos.astype(F32)[:, None] * inv[None, :]
    cos, sin = jnp.cos(ang), jnp.sin(ang)
    reps = LANES // HEAD_DIM
    cos_row = jnp.tile(jnp.concatenate([cos, cos], axis=1), (1, reps))
    sin_row = jnp.tile(jnp.concatenate([-sin, sin], axis=1), (1, reps))
    return cos_row, sin_row, cos.T, sin.T


def _decay_tables(blk):
    log_g = jnp.log1p(-jnp.exp2(-5.0 - jnp.arange(H_RET, dtype=F32)))
    i = jnp.arange(blk, dtype=F32)
    diff = i[:, None] - i[None, :]
    dmask = jnp.where(diff[None] >= 0, jnp.exp(jnp.maximum(diff, 0.0)[None] * log_g[:, None, None]), 0.0)
    q_dec = jnp.exp((i + 1.0)[None, :] * log_g[:, None])
    k_dec = jnp.exp((blk - 1.0 - i)[None, :] * log_g[:, None])
    c_dec = jnp.exp(blk * log_g)
    return dmask, q_dec, k_dec, c_dec


def kernel(x_prompt, x_sample, state_ret, cache_sb_k, cache_sb_v, cache_fox_k, cache_fox_v, cache_fox_logf,
           page_table, p_prompt, p_sample, g_norm, w_in, b_f, g_ret_gn, g_fox_q, g_fox_k, w_out, w_pe, g_pe, w_pg):
    bp, tp, d = x_prompt.shape
    bs, ts, _ = x_sample.shape
    depth = w_in.shape[0]
    n_pages = page_table.shape[1]
    past_len = n_pages * PAGE_SIZE
    d_ple = p_prompt.shape[-1]
    assert tp % ATT_BLOCK == 0 and tp % RET_CHUNK == 0 and ts == SUBLANES

    cos_p, sin_p, cost_p, sint_p = _rotary_tables(jnp.arange(tp, dtype=jnp.int32))
    cos_s, sin_s, _, _ = _rotary_tables(past_len + jnp.arange(ts, dtype=jnp.int32))
    dm_p, qd_p, kd_p, cd_p = _decay_tables(RET_CHUNK)
    dec_p = (dm_p, qd_p[:, :, None], kd_p[:, None, :], cd_p[:, None, None])
    dm_s, qd_s, kd_s, cd_s = _decay_tables(ts)
    dec_s = (dm_s, qd_s[:, :, None], kd_s[:, :, None], cd_s[:, None, None])
    tq_att = 2 * ATT_BLOCK if tp % (2 * ATT_BLOCK) == 0 else ATT_BLOCK
    later = lambda n: (jnp.arange(n)[:, None] > jnp.arange(n)[None, :]).astype(BF16)
    lmat_page, lmat_att = later(PAGE_SIZE), later(tq_att)
    ar = jnp.arange(tq_att)
    umat = (ar[:, None] <= ar[None, :]).astype(BF16)
    lane_head = jnp.arange(D_FOX) // HEAD_DIM
    seg = (lane_head[:, None] == lane_head[None, :]).astype(BF16)

    def pool_t(c):
        c = jnp.transpose(c, (0, 1, 3, 4, 2))
        return c.reshape(c.shape[0], c.shape[1], c.shape[2] * c.shape[3], c.shape[4])
    pools = tuple(pool_t(c) for c in (cache_sb_k, cache_sb_v, cache_fox_k, cache_fox_v))
    lf_pool = jnp.transpose(cache_fox_logf, (0, 3, 1, 2))

    n_grp = next(g for g in (32, 16, 8, 4, 2, 1) if n_pages % (2 * g) == 0)
    n_head = min(2, n_pages)
    n_tail = next(g for g in (8, 7, 6, 5, 4, 3, 2, 1) if (n_pages - n_head) % g == 0)
    tm_in = next(m for m in (512, 256, ATT_BLOCK) if tp % m == 0)
    n_p, n_s = bp * tp, bs * ts
    tm_out = 512 if n_p % 512 == 0 else ATT_BLOCK

    y_p = x_prompt
    y_s = x_sample.reshape(n_s, d)
    outs = {k: [] for k in ("rs_p", "rs_s", "sks", "svs", "flp", "fks", "fvs", "fls")}
    kv_stacked = None
    w_in_t = jnp.transpose(w_in, (0, 2, 1)).astype(BF16)
    splits = np.cumsum([D_RET] * 4 + [D_SB] * 4 + [D_FOX] * 4).tolist()
    for i in range(depth):
        (w_qr, w_kr, w_vr, w_gr, w_qs, w_ks, w_vs, w_gs, w_qf, w_kf, w_vf, w_gf, w_fl) = jnp.split(
            w_in_t[i], splits, axis=0)
        w_row = jnp.concatenate([w_qr, w_vr, w_gr, w_qs, w_gs, w_qf, w_gf], axis=0)
        w_flt = jnp.pad(w_fl, ((0, SUBLANES - H_FOX), (0, 0)))
        w_t = [w_kr, w_ks, w_vs, w_kf, w_vf, w_flt]
        gn = g_norm[i][None, :]
        b_f8 = jnp.pad(b_f[i], (0, SUBLANES - H_FOX))[:, None]
        gq_row = jnp.tile(g_fox_q[i], H_FOX)[None, :]
        gk_row = jnp.tile(g_fox_k[i], H_FOX)[None, :]
        gk_col = g_fox_k[i][:, None]
        ggn = g_ret_gn[i][None, :]
        w_o = w_out[i].astype(BF16)
        w_op = (w_o[:D_RET], w_o[D_RET:D_RET + D_SB], w_o[D_RET + D_SB:],
                w_pe[i].astype(BF16), g_pe[i][None, :], w_pg[i].astype(BF16))

        (qr, vr, gr, qs, gs, qf, gf, krt, kst, vst, kft, vft, lft) = _inproj_prompt(
            y_p, gn, (cos_p, sin_p, cost_p, sint_p), b_f8, gk_col, [w_row] + w_t, tm_in, i, depth, kv_stacked)
        kv_stacked = (kst, vst, kft, vft)
        o_r, s_fin = _ret_prompt(qr, krt, vr, gr, dec_p, ggn)
        o_s = _sb_prompt(qs, kst, vst, gs, lmat_att, tq_att, i)
        o_f = _fox_prompt(qf, kft, vft, lft, gf, gq_row, seg, umat, tq_att, i)
        y_p = _outproj(y_p.reshape(n_p, d), o_r.reshape(n_p, D_RET), o_s.reshape(n_p, D_SB),
                       o_f.reshape(n_p, D_FOX), p_prompt[i].reshape(n_p, d_ple), w_op, tm_out).reshape(bp, tp, d)
        outs["rs_p"].append(s_fin)
        outs["flp"].append(lft[:, :H_FOX])

        proj, flt = _inproj_sample(y_s, gn, [w_row] + w_t)
        flt_b = jnp.transpose(flt.reshape(SUBLANES, bs, ts), (1, 0, 2))
        o_r, s_new = _ret_sample(proj, state_ret[i], (cos_s, sin_s), dec_s, ggn, ts)
        o_f, kf_new, lf_new = _decode_fox(proj, flt_b, pools[2:], lf_pool, page_table, i,
                                          b_f8, gq_row, gk_row, seg, lmat_page, ts, n_grp)
        o_s, sb_acc, sb_car = _decode_sb(proj, pools[:2], page_table, i, lmat_page, ts, n_head, n_tail)
        if n_pages > n_head:
            o_s = lax.cond(jnp.max(sb_car) > SB_DEAD,
                           lambda st: _decode_sb(proj, pools[:2], page_table, i, lmat_page, ts, n_head, n_tail, st),
                           lambda st: o_s, (sb_acc, sb_car))
        y_s = _outproj(y_s, o_r, o_s, o_f, p_sample[i].reshape(n_s, d_ple), w_op, n_s)
        outs["rs_s"].append(s_new)
        piece = lambda k, wd: proj[:, k * PIECE:k * PIECE + wd]
        outs["sks"].append(piece(5, D_SB)); outs["svs"].append(piece(6, D_SB))
        outs["fks"].append(kf_new); outs["fvs"].append(piece(10, D_FOX))
        outs["fls"].append(lf_new[:, :H_FOX])

    st = lambda k: jnp.stack(outs[k], axis=0)
    kv_p = lambda a, h: jnp.transpose(a.reshape(depth, bp, h, HEAD_DIM, tp), (0, 1, 4, 2, 3))
    kv_s = lambda k, h: st(k).reshape(depth, bs, ts, h, HEAD_DIM)
    return (y_p, y_s.reshape(bs, ts, d), st("rs_p"), st("rs_s"),
            kv_p(kv_stacked[0], H_SB), kv_p(kv_stacked[1], H_SB), kv_s("sks", H_SB), kv_s("svs", H_SB),
            kv_p(kv_stacked[2], H_FOX), kv_p(kv_stacked[3], H_FOX), jnp.transpose(st("flp"), (0, 1, 3, 2)),
            kv_s("fks", H_FOX), kv_s("fvs", H_FOX), jnp.transpose(st("fls"), (0, 1, 3, 2)))
```
